```python
import math
import jax, jax.numpy as jnp
from jax import lax
import numpy as np

D_MODEL = 1024
BATCH = 4
SEQ = 8192
DEPTH = 2
DEC_BATCH = 16
DEC_SEQ = 64
PAST_LEN = 1024

CHUNK = 64
PLE_DIM = 256
ALPHA = (2 * DEPTH) ** 0.25
BETA = (8 * DEPTH) ** -0.25
NORM_EPS = 1e-5
A_WIDTH = D_MODEL // 2
A_BLOCKS = 8
A_BLOCK_DIM = A_WIDTH // A_BLOCKS
A_CONV = 4
A_C = 8.0
B_HEADS = 4
B_HEAD_DIM = 128
B_WIDTH = B_HEADS * B_HEAD_DIM
B_CONV = 4
C_HEAD_DIM = 64
C_WIDTH = D_MODEL // 2
C_HEADS = C_WIDTH // C_HEAD_DIM
C_DECAY_LORA = 64
C_AAA_LORA = 64
C_GATE_LORA = 128
C_LNX_EPS = 64e-5
C_PROJ = 3 * C_WIDTH + C_DECAY_LORA + C_AAA_LORA + C_GATE_LORA
D_HEADS = 4
D_Q_LORA = 384
D_KV_LORA = 256
D_NOPE = 128
D_ROPE = 64
D_V = 128
D_WIDTH = D_HEADS * D_V
D_PROJ = D_Q_LORA + D_KV_LORA + D_ROPE
MLA_SCALE = (D_NOPE + D_ROPE) ** -0.5
ROPE_THETA = 10000.0
Q_BLOCK = 128
D_FF = 2816
FFN_CONV = 3
IN0 = 2 * A_WIDTH + 4 * B_WIDTH + 2 * B_HEADS
IN1 = C_PROJ + D_PROJ

kernel_name = 'hybrid_streaming_encoder_step'


def split_cols(x, sizes):
    offs = np.cumsum(sizes)[:-1].tolist()
    return jnp.split(x, offs, axis=-1)


def layer_norm(x, g, b):
    xf = x.astype(jnp.float32)
    mu = jnp.mean(xf, -1, keepdims=True)
    var = jnp.mean(jnp.square(xf - mu), -1, keepdims=True)
    return ((xf - mu) * lax.rsqrt(var + NORM_EPS) * g + b).astype(x.dtype)


def rms_norm(x, g, eps=1e-6):
    xf = x.astype(jnp.float32)
    return (xf * lax.rsqrt(jnp.mean(xf * xf, -1, keepdims=True) + eps) * g).astype(x.dtype)


def l2norm(x, eps=1e-6):
    xf = x.astype(jnp.float32)
    return xf * lax.rsqrt(jnp.sum(xf * xf, -1, keepdims=True) + eps)


def causal_dwconv(x, buf, w, b=None):
    width = w.shape[0]
    length = x.shape[1]
    xp = jnp.concatenate([buf.astype(x.dtype), x], axis=1)
    y = xp[:, 0:length] * w[0]
    for j in range(1, width):
        y = y + xp[:, j:j + length] * w[j]
    if b is not None:
        y = y + b
    return y, xp[:, length:]


def linear_scan(a, b, h0):
    b = b.at[:, 0].add(a[:, 0] * h0)

    def combine(left, right):
        a_l, b_l = left
        a_r, b_r = right
        return a_l * a_r, a_r * b_l + b_r

    _, h = lax.associative_scan(combine, (a, b), axis=1)
    return h


def rope(x, pos):
    half = x.shape[-1] // 2
    inv = ROPE_THETA ** (-jnp.arange(half, dtype=jnp.float32) / half)
    ang = pos.astype(jnp.float32)[:, None] * inv[None, :]
    cos = jnp.cos(ang)[None, :, None, :].astype(x.dtype)
    sin = jnp.sin(ang)[None, :, None, :].astype(x.dtype)
    x1, x2 = x[..., :half], x[..., half:]
    return jnp.concatenate([x1 * cos - x2 * sin, x2 * cos + x1 * sin], axis=-1)


def rglru_mixer(gate_in, x_in, conv_buf, h0, conv_w, conv_b, w_r, b_r, w_i, b_i, lam):
    bsz, length, _ = x_in.shape
    xc, new_buf = causal_dwconv(x_in, conv_buf, conv_w, conv_b)
    xb = xc.reshape(bsz, length, A_BLOCKS, A_BLOCK_DIM)
    r = jax.nn.sigmoid(jnp.einsum('blhi,hij->blhj', xb, w_r).reshape(bsz, length, A_WIDTH) + b_r)
    i = jax.nn.sigmoid(jnp.einsum('blhi,hij->blhj', xb, w_i).reshape(bsz, length, A_WIDTH) + b_i)
    log_a = -A_C * r.astype(jnp.float32) * jax.nn.softplus(-lam.astype(jnp.float32))
    a = jnp.exp(log_a)
    u = jnp.sqrt(-jnp.expm1(2.0 * log_a)) * (i * xc).astype(jnp.float32)
    h = linear_scan(a, u, h0.astype(jnp.float32))
    out = h.astype(x_in.dtype) * jax.nn.gelu(gate_in)
    return out, new_buf, h[:, -1].astype(h0.dtype)


def gated_delta_rule(q, k, v, g, beta, s0, chunk):
    bsz, length, heads, dk = q.shape
    n = length // chunk

    def blocks(t):
        t = t.astype(jnp.float32).reshape((bsz, n, chunk, heads) + t.shape[3:])
        return jnp.moveaxis(t, 3, 1)

    q = blocks(q) * dk ** -0.5
    k, v, g, beta = blocks(k), blocks(v), blocks(g), blocks(beta)
    gc = jnp.cumsum(g, axis=-1)
    idx = jnp.arange(chunk)
    decay = jnp.exp(jnp.where(idx[:, None] >= idx[None, :], gc[..., :, None] - gc[..., None, :], -jnp.inf))
    kb = k * beta[..., None]
    lmat = jnp.where(idx[:, None] > idx[None, :], jnp.einsum('bhncd,bhnsd->bhncs', kb, k) * decay, 0.0)
    eye = jnp.eye(chunk, dtype=jnp.float32)
    tmat = lax.linalg.triangular_solve(lmat + eye, jnp.broadcast_to(eye, lmat.shape),
                                       left_side=True, lower=True, unit_diagonal=True)
    u = jnp.einsum('bhncs,bhnsd->bhncd', tmat, v * beta[..., None])
    w = jnp.einsum('bhncs,bhnsd->bhncd', tmat, kb * jnp.exp(gc)[..., None])
    qk = jnp.einsum('bhncd,bhnsd->bhncs', q, k) * decay
    q_in = q * jnp.exp(gc)[..., None]
    g_last = gc[..., -1]
    k_out = k * jnp.exp(g_last[..., None] - gc)[..., None]

    def step(s, xs):
        u_c, w_c, qk_c, q_c, k_c, gl_c = xs
        v_new = u_c - jnp.einsum('bhcd,bhde->bhce', w_c, s)
        o = jnp.einsum('bhcd,bhde->bhce', q_c, s) + jnp.einsum('bhcs,bhse->bhce', qk_c, v_new)
        s = s * jnp.exp(gl_c)[..., None, None] + jnp.einsum('bhcd,bhce->bhde', k_c, v_new)
        return s, o

    xs = tuple(jnp.moveaxis(t, 2, 0) for t in (u, w, qk, q_in, k_out, g_last))
    s, o = lax.scan(step, s0.astype(jnp.float32), xs)
    o = jnp.moveaxis(jnp.moveaxis(o, 0, 2), 1, 3).reshape(bsz, length, heads, v.shape[-1])
    return o, s


def gdn_mixer(qkv, z, beta_in, alpha_in, conv_buf, s0, chunk, conv_w, a_log, dt_bias, norm_g):
    bsz, length, _ = qkv.shape
    qkv, new_buf = causal_dwconv(qkv, conv_buf, conv_w)
    qkv = jax.nn.silu(qkv).reshape(bsz, length, 3, B_HEADS, B_HEAD_DIM)
    q, k, v = l2norm(qkv[:, :, 0]), l2norm(qkv[:, :, 1]), qkv[:, :, 2]
    beta = jax.nn.sigmoid(beta_in.astype(jnp.float32))
    g = -jnp.exp(a_log.astype(jnp.float32)) * jax.nn.softplus((alpha_in + dt_bias).astype(jnp.float32))
    o, s = gated_delta_rule(q, k, v, g, beta, s0, chunk)
    o = rms_norm(o, norm_g).astype(z.dtype) * jax.nn.silu(z.reshape(bsz, length, B_HEADS, B_HEAD_DIM))
    return o.reshape(bsz, length, B_WIDTH), new_buf, s.astype(s0.dtype)


def rwkv7_scan(r, w, k, v, a, b, s0):
    def step(s, xs):
        r_t, w_t, k_t, v_t, a_t, b_t = xs
        sa = jnp.einsum('bhvk,bhk->bhv', s, a_t)
        s = s * w_t[:, :, None, :] + sa[..., None] * b_t[:, :, None, :] + v_t[..., None] * k_t[:, :, None, :]
        return s, jnp.einsum('bhvk,bhk->bhv', s, r_t)

    xs = tuple(jnp.moveaxis(t, 1, 0) for t in (r, w, k, v, a, b))
    s, o = lax.scan(step, s0, xs)
    return jnp.moveaxis(o, 0, 1), s


def rwkv7_mixer(proj, shift_buf, s0, mu, w0, w2, a0, a2, g2, k_k, k_a, r_k, lnx_g, lnx_b):
    bsz, length, _ = proj.shape
    xp = jnp.concatenate([shift_buf.astype(proj.dtype), proj], axis=1)
    xs = proj + (xp[:, :length] - proj) * mu
    r, k, v, xw, xa, xg = split_cols(xs, (C_WIDTH, C_WIDTH, C_WIDTH, C_DECAY_LORA, C_AAA_LORA, C_GATE_LORA))
    w = -jax.nn.softplus(-(w0 + jnp.tanh(xw) @ w2)) - 0.5
    a = jax.nn.sigmoid(a0 + xa @ a2)
    g = jax.nn.sigmoid(xg) @ g2

    def heads(t):
        return t.astype(jnp.float32).reshape(bsz, length, C_HEADS, C_HEAD_DIM)

    kk = l2norm(heads(k * k_k))
    k = k * (1.0 + (a - 1.0) * k_a)
    rh, kh, vh, ah = heads(r), heads(k), heads(v), heads(a)
    decay = jnp.exp(-jnp.exp(heads(w)))
    o, s = rwkv7_scan(rh, decay, kh, vh, -kk, kk * ah, s0.astype(jnp.float32))
    mu_o = jnp.mean(o, -1, keepdims=True)
    var_o = jnp.mean(jnp.square(o - mu_o), -1, keepdims=True)
    o = ((o - mu_o) * lax.rsqrt(var_o + C_LNX_EPS)).reshape(bsz, length, C_WIDTH) * lnx_g + lnx_b
    o = o + (jnp.sum(rh * kh * r_k, -1, keepdims=True) * vh).reshape(bsz, length, C_WIDTH)
    return o.astype(proj.dtype) * g, xp[:, length:], s.astype(s0.dtype)


def mla_mixer(proj, ckv_cache, krope_cache, pos, qa_g, w_qb, kva_g, w_kvb):
    bsz, length, _ = proj.shape
    q_a, c_raw, kr_raw = split_cols(proj, (D_Q_LORA, D_KV_LORA, D_ROPE))
    q = (rms_norm(q_a, qa_g) @ w_qb).reshape(bsz, length, D_HEADS, D_NOPE + D_ROPE)
    q_nope = q[..., :D_NOPE]
    q_rope = rope(q[..., D_NOPE:], pos)
    c_new = rms_norm(c_raw, kva_g)
    kr_new = rope(kr_raw[:, :, None, :], pos)[:, :, 0]
    c_all = jnp.concatenate([ckv_cache.astype(c_new.dtype), c_new], axis=1)
    kr_all = jnp.concatenate([krope_cache.astype(kr_new.dtype), kr_new], axis=1)
    kv = (c_all @ w_kvb).reshape(bsz, c_all.shape[1], D_HEADS, D_NOPE + D_V)
    k_nope, v = kv[..., :D_NOPE], kv[..., D_NOPE:]
    k_chunk = jnp.arange(c_all.shape[1]) // CHUNK
    qb = min(Q_BLOCK, length)
    nb = length // qb

    def to_blocks(t):
        return jnp.moveaxis(t.reshape((bsz, nb, qb) + t.shape[2:]), 1, 0)

    def attend(blk):
        qn, qr, qp = blk
        s = (jnp.einsum('bqhd,bkhd->bhqk', qn, k_nope)
             + jnp.einsum('bqhr,bkr->bhqk', qr, kr_all)).astype(jnp.float32) * MLA_SCALE
        visible = k_chunk[None, :] <= (qp // CHUNK)[:, None]
        p = jax.nn.softmax(jnp.where(visible, s, -jnp.inf), axis=-1)
        return jnp.einsum('bhqk,bkhd->bqhd', p.astype(v.dtype), v)

    o = lax.map(attend, (to_blocks(q_nope), to_blocks(q_rope), pos.reshape(nb, qb)))
    o = jnp.moveaxis(o, 0, 1).reshape(bsz, length, D_WIDTH)
    return o, c_new, kr_new


def conv_ffn(x, buf, w_up, conv_w, conv_b, w_down):
    h, new_buf = causal_dwconv(x @ w_up, buf, conv_w, conv_b)
    gate, val = jnp.split(h, 2, axis=-1)
    return (jax.nn.silu(gate) * val) @ w_down, new_buf


def trunk(x, pe, pos, gdn_chunk, a_conv, a_h, b_conv, b_s, c_shift, c_s, d_ckv, d_krope, f_conv, params):
    (w_in0, a_conv_w, a_conv_b, a_w_r, a_b_r, a_w_i, a_b_i, a_lambda, b_conv_w, b_a_log, b_dt_bias,
     b_norm_g, w_out0, w_in1, c_mu, c_w0, c_w2, c_a0, c_a2, c_g2, c_k_k, c_k_a, c_r_k, c_lnx_g, c_lnx_b,
     d_qa_g, d_w_qb, d_kva_g, d_w_kvb, w_out1, ln_mix_g, ln_mix_b, ffn_w_up, ffn_conv_w, ffn_conv_b,
     ffn_w_down, ln_ffn_g, ln_ffn_b, ple_w_gate, ple_b_gate, ple_w_proj) = params
    f_new = []
    for i in range(DEPTH):
        if i % 2 == 0:
            a_gate, a_x, b_qkv, b_z, b_beta, b_alpha = split_cols(
                x @ w_in0, (A_WIDTH, A_WIDTH, 3 * B_WIDTH, B_WIDTH, B_HEADS, B_HEADS))
            a_out, a_conv, a_h = rglru_mixer(a_gate, a_x, a_conv, a_h, a_conv_w, a_conv_b,
                                             a_w_r, a_b_r, a_w_i, a_b_i, a_lambda)
            b_out, b_conv, b_s = gdn_mixer(b_qkv, b_z, b_beta, b_alpha, b_conv, b_s, gdn_chunk,
                                           b_conv_w, b_a_log, b_dt_bias, b_norm_g)
            mix = jnp.concatenate([a_out, b_out], axis=-1) @ w_out0
        else:
            c_in, d_in = split_cols(x @ w_in1, (C_PROJ, D_PROJ))
            c_out, c_shift, c_s = rwkv7_mixer(c_in, c_shift, c_s, c_mu, c_w0, c_w2, c_a0, c_a2, c_g2,
                                              c_k_k, c_k_a, c_r_k, c_lnx_g, c_lnx_b)
            d_out, d_ckv, d_krope = mla_mixer(d_in, d_ckv, d_krope, pos, d_qa_g, d_w_qb, d_kva_g, d_w_kvb)
            mix = jnp.concatenate([c_out, d_out], axis=-1) @ w_out1
        x = layer_norm(ALPHA * x + mix, ln_mix_g[i], ln_mix_b[i])
        f_out, f_buf = conv_ffn(x, f_conv[i], ffn_w_up[i], ffn_conv_w[i], ffn_conv_b[i], ffn_w_down[i])
        f_new.append(f_buf)
        x = layer_norm(ALPHA * x + f_out, ln_ffn_g[i], ln_ffn_b[i])
        x = x + jax.nn.sigmoid(x @ ple_w_gate[i] + ple_b_gate[i]) * (pe[i] @ ple_w_proj[i])
    return x, a_conv, a_h, b_conv, b_s, c_shift, c_s, d_ckv, d_krope, jnp.stack(f_new)


def setup_inputs(seed: int = 0) -> dict:
    key = jax.random.key(seed)
    keys = iter(jax.random.split(key, 96))

    def nrm(shape, scale):
        return scale * jax.random.normal(next(keys), shape, jnp.float32)

    def uni(shape, lo, hi):
        return jax.random.uniform(next(keys), shape, jnp.float32, lo, hi)

    def gain(shape):
        return 1.0 + nrm(shape, 0.02)

    a_base = uni((A_WIDTH,), 0.9, 0.999) ** (1.0 / A_C)
    a_lambda = jnp.log(a_base) - jnp.log1p(-a_base)
    dt0 = jnp.exp(uni((B_HEADS,), math.log(1e-3), math.log(1e-1)))
    b_dt_bias = dt0 + jnp.log(-jnp.expm1(-dt0))
    d = D_MODEL
    return {
        'x_prompt': nrm((BATCH, SEQ, d), 1.0),
        'x_sample': nrm((DEC_BATCH, DEC_SEQ, d), 1.0),
        'state_a_conv': nrm((DEC_BATCH, A_CONV - 1, A_WIDTH), 1.0),
        'state_a_h': nrm((DEC_BATCH, A_WIDTH), 0.5),
        'state_b_conv': nrm((DEC_BATCH, B_CONV - 1, 3 * B_WIDTH), 1.0),
        'state_b_s': nrm((DEC_BATCH, B_HEADS, B_HEAD_DIM, B_HEAD_DIM), 0.1),
        'state_c_shift': nrm((DEC_BATCH, 1, C_PROJ), 1.0),
        'state_c_s': nrm((DEC_BATCH, C_HEADS, C_HEAD_DIM, C_HEAD_DIM), 0.5),
        'cache_d_ckv': nrm((DEC_BATCH, PAST_LEN, D_KV_LORA), 1.0),
        'cache_d_krope': nrm((DEC_BATCH, PAST_LEN, D_ROPE), 1.0),
        'state_ffn_conv': nrm((DEPTH, DEC_BATCH, FFN_CONV - 1, 2 * D_FF), 1.0),
        'p_prompt': nrm((DEPTH, BATCH, SEQ, PLE_DIM), 1.0),
        'p_sample': nrm((DEPTH, DEC_BATCH, DEC_SEQ, PLE_DIM), 1.0),
        'w_in0': nrm((d, IN0), d ** -0.5),
        'a_conv_w': nrm((A_CONV, A_WIDTH), A_CONV ** -0.5),
        'a_conv_b': nrm((A_WIDTH,), 0.02),
        'a_w_r': nrm((A_BLOCKS, A_BLOCK_DIM, A_BLOCK_DIM), A_BLOCK_DIM ** -0.5),
        'a_b_r': nrm((A_WIDTH,), 0.02),
        'a_w_i': nrm((A_BLOCKS, A_BLOCK_DIM, A_BLOCK_DIM), A_BLOCK_DIM ** -0.5),
        'a_b_i': nrm((A_WIDTH,), 0.02),
        'a_lambda': a_lambda,
        'b_conv_w': nrm((B_CONV, 3 * B_WIDTH), B_CONV ** -0.5),
        'b_a_log': jnp.log(uni((B_HEADS,), 1.0, 16.0)),
        'b_dt_bias': b_dt_bias,
        'b_norm_g': gain((B_HEAD_DIM,)),
        'w_out0': nrm((A_WIDTH + B_WIDTH, d), BETA * (A_WIDTH + B_WIDTH) ** -0.5),
        'w_in1': nrm((d, IN1), d ** -0.5),
        'c_mu': uni((C_PROJ,), 0.0, 1.0),
        'c_w0': uni((C_WIDTH,), -6.0, -1.0),
        'c_w2': nrm((C_DECAY_LORA, C_WIDTH), 0.1),
        'c_a0': nrm((C_WIDTH,), 0.1),
        'c_a2': nrm((C_AAA_LORA, C_WIDTH), 0.1),
        'c_g2': nrm((C_GATE_LORA, C_WIDTH), C_GATE_LORA ** -0.5),
        'c_k_k': 0.85 + nrm((C_WIDTH,), 0.02),
        'c_k_a': gain((C_WIDTH,)),
        'c_r_k': nrm((C_HEADS, C_HEAD_DIM), 0.1),
        'c_lnx_g': gain((C_WIDTH,)),
        'c_lnx_b': nrm((C_WIDTH,), 0.02),
        'd_qa_g': gain((D_Q_LORA,)),
        'd_w_qb': nrm((D_Q_LORA, D_HEADS * (D_NOPE + D_ROPE)), D_Q_LORA ** -0.5),
        'd_kva_g': gain((D_KV_LORA,)),
        'd_w_kvb': nrm((D_KV_LORA, D_HEADS * (D_NOPE + D_V)), D_KV_LORA ** -0.5),
        'w_out1': nrm((C_WIDTH + D_WIDTH, d), BETA * (C_WIDTH + D_WIDTH) ** -0.5),
        'ln_mix_g': gain((DEPTH, d)),
        'ln_mix_b': nrm((DEPTH, d), 0.02),
        'ffn_w_up': nrm((DEPTH, d, 2 * D_FF), d ** -0.5),
        'ffn_conv_w': nrm((DEPTH, FFN_CONV, 2 * D_FF), FFN_CONV ** -0.5),
        'ffn_conv_b': nrm((DEPTH, 2 * D_FF), 0.02),
        'ffn_w_down': nrm((DEPTH, D_FF, d), BETA * D_FF ** -0.5),
        'ln_ffn_g': gain((DEPTH, d)),
        'ln_ffn_b': nrm((DEPTH, d), 0.02),
        'ple_w_gate': nrm((DEPTH, d, d), d ** -0.5),
        'ple_b_gate': nrm((DEPTH, d), 0.02),
        'ple_w_proj': nrm((DEPTH, PLE_DIM, d), PLE_DIM ** -0.5),
    }


def reference(x_prompt, x_sample, state_a_conv, state_a_h, state_b_conv, state_b_s, state_c_shift,
              state_c_s, cache_d_ckv, cache_d_krope, state_ffn_conv, p_prompt, p_sample,
              w_in0, a_conv_w, a_conv_b, a_w_r, a_b_r, a_w_i, a_b_i, a_lambda, b_conv_w, b_a_log,
              b_dt_bias, b_norm_g, w_out0, w_in1, c_mu, c_w0, c_w2, c_a0, c_a2, c_g2, c_k_k, c_k_a,
              c_r_k, c_lnx_g, c_lnx_b, d_qa_g, d_w_qb, d_kva_g, d_w_kvb, w_out1, ln_mix_g, ln_mix_b,
              ffn_w_up, ffn_conv_w, ffn_conv_b, ffn_w_down, ln_ffn_g, ln_ffn_b, ple_w_gate,
              ple_b_gate, ple_w_proj):
    params = (w_in0, a_conv_w, a_conv_b, a_w_r, a_b_r, a_w_i, a_b_i, a_lambda, b_conv_w, b_a_log,
              b_dt_bias, b_norm_g, w_out0, w_in1, c_mu, c_w0, c_w2, c_a0, c_a2, c_g2, c_k_k, c_k_a,
              c_r_k, c_lnx_g, c_lnx_b, d_qa_g, d_w_qb, d_kva_g, d_w_kvb, w_out1, ln_mix_g, ln_mix_b,
              ffn_w_up, ffn_conv_w, ffn_conv_b, ffn_w_down, ln_ffn_g, ln_ffn_b, ple_w_gate,
              ple_b_gate, ple_w_proj)
    dt = x_prompt.dtype
    bp, lp = x_prompt.shape[0], x_prompt.shape[1]
    (y_prompt, a_conv_p, a_h_p, b_conv_p, b_s_p, c_shift_p, c_s_p, ckv_p, krope_p, ffn_conv_p) = trunk(
        x_prompt, p_prompt, jnp.arange(lp), CHUNK,
        jnp.zeros((bp, A_CONV - 1, A_WIDTH), dt), jnp.zeros((bp, A_WIDTH), dt),
        jnp.zeros((bp, B_CONV - 1, 3 * B_WIDTH), dt), jnp.zeros((bp, B_HEADS, B_HEAD_DIM, B_HEAD_DIM), dt),
        jnp.zeros((bp, 1, C_PROJ), dt), jnp.zeros((bp, C_HEADS, C_HEAD_DIM, C_HEAD_DIM), dt),
        jnp.zeros((bp, 0, D_KV_LORA), dt), jnp.zeros((bp, 0, D_ROPE), dt),
        jnp.zeros((DEPTH, bp, FFN_CONV - 1, 2 * D_FF), dt), params)
    ls = x_sample.shape[1]
    past = cache_d_ckv.shape[1]
    (y_sample, a_conv_s, a_h_s, b_conv_s, b_s_s, c_shift_s, c_s_s, ckv_s, krope_s, ffn_conv_s) = trunk(
        x_sample, p_sample, past + jnp.arange(ls), ls,
        state_a_conv, state_a_h, state_b_conv, state_b_s, state_c_shift, state_c_s,
        cache_d_ckv, cache_d_krope, state_ffn_conv, params)
    return (y_prompt, y_sample, a_conv_p, a_conv_s, a_h_p, a_h_s, b_conv_p, b_conv_s, b_s_p, b_s_s,
            c_shift_p, c_shift_s, c_s_p, c_s_s, ckv_p, ckv_s, krope_p, krope_s, ffn_conv_p, ffn_conv_s)
```

```python
import functools

import jax
import jax.numpy as jnp
from jax import lax
from jax.experimental import pallas as pl
from jax.experimental.pallas import tpu as pltpu

F32 = jnp.float32
BF16 = jnp.bfloat16

D_MODEL = 1024
DEPTH = 2
CHUNK = 64
CHUNK_SHIFT = CHUNK.bit_length() - 1
PLE_DIM = 256
ALPHA = (2 * DEPTH) ** 0.25
NORM_EPS = 1e-5
A_WIDTH = 512
A_BLOCKS = 8
A_C = 8.0
B_HEADS = 4
B_HEAD_DIM = 128
B_WIDTH = 512
C_HEAD_DIM = 64
C_WIDTH = 512
C_HEADS = 8
C_LNX_EPS = 64e-5
C_PROJ = 3 * C_WIDTH + 64 + 64 + 128
D_HEADS = 4
D_Q_LORA = 384
D_KV_LORA = 256
D_NOPE = 128
D_ROPE = 64
D_V = 128
D_QK_PAD = 256
MLA_SCALE = (D_NOPE + D_ROPE) ** -0.5
ROPE_THETA = 10000.0
D_FF = 2816
FF_BLOCK = 256
ROW_PAD = 8
VMEM_LIMIT = 56 * 1024 * 1024


def _params(sem):
    return pltpu.CompilerParams(dimension_semantics=sem, vmem_limit_bytes=VMEM_LIMIT)


def _sigmoid(x):
    return 1.0 / (1.0 + jnp.exp(-x))


def _softplus(x):
    return jnp.maximum(x, 0.0) + jnp.log(1.0 + jnp.exp(-jnp.abs(x)))


def _silu(x):
    return x * _sigmoid(x)


def _gelu_tanh(x):
    return 0.5 * x * (1.0 + jnp.tanh(0.7978845608028654 * (x + 0.044715 * x * x * x)))


def _dims(trans_a, trans_b):
    return (((0 if trans_a else 1,), (1 if trans_b else 0,)), ((), ()))


def _mm(a, b, trans_a=False, trans_b=False):
    return lax.dot_general(a.astype(BF16), b.astype(BF16), _dims(trans_a, trans_b),
                           preferred_element_type=F32)


def _split(a):
    hi = a.astype(BF16)
    return hi, (a - hi.astype(F32)).astype(BF16)


def _mm3(a, b, trans_a=False, trans_b=False):
    d = _dims(trans_a, trans_b)
    ah, al = _split(a)
    bh, bl = _split(b)
    out = lax.dot_general(ah, bl, d, preferred_element_type=F32)
    out = out + lax.dot_general(al, bh, d, preferred_element_type=F32)
    return out + lax.dot_general(ah, bh, d, preferred_element_type=F32)


def _split3(a):
    a1 = a.astype(BF16)
    r1 = a - a1.astype(F32)
    a2 = r1.astype(BF16)
    return a1, a2, (r1 - a2.astype(F32)).astype(BF16)


def _mm_exact_rhs(a, mask):
    a1, a2, a3 = _split3(a)
    mb = mask.astype(BF16)
    out = jnp.dot(a3, mb, preferred_element_type=F32)
    out = out + jnp.dot(a2, mb, preferred_element_type=F32)
    return out + jnp.dot(a1, mb, preferred_element_type=F32)


def _mm_exact_lhs(mask, b):
    b1, b2, b3 = _split3(b)
    mb = mask.astype(BF16)
    out = jnp.dot(mb, b3, preferred_element_type=F32)
    out = out + jnp.dot(mb, b2, preferred_element_type=F32)
    return out + jnp.dot(mb, b1, preferred_element_type=F32)


def _iota2(shape, axis):
    return lax.broadcasted_iota(jnp.int32, shape, axis)


def _neumann_inv(x):
    n = x.shape[0]
    eye = (_iota2(x.shape, 0) == _iota2(x.shape, 1)).astype(F32)
    t = eye + x
    p = x
    k = 2
    while k < n:
        p = _mm3(p, p)
        t = t + _mm3(t, p)
        k *= 2
    return t


def _layer_norm(x, g, b):
    mu = jnp.mean(x, axis=-1, keepdims=True)
    xc = x - mu
    var = jnp.mean(xc * xc, axis=-1, keepdims=True)
    return xc * lax.rsqrt(var + NORM_EPS) * g + b


def _rms_norm(x, g, eps=1e-6):
    return x * lax.rsqrt(jnp.mean(x * x, axis=-1, keepdims=True) + eps) * g


def _proj_kernel(x_ref, *refs):
    n = len(refs) // 2
    xb = x_ref[...].astype(BF16)
    for w_ref, o_ref in zip(refs[:n], refs[n:]):
        o_ref[...] = jnp.dot(xb, w_ref[...], preferred_element_type=F32)


def _proj(x, ws, tm):
    t, k = x.shape
    return pl.pallas_call(
        _proj_kernel,
        grid=(t // tm,),
        in_specs=[pl.BlockSpec((tm, k), lambda i: (i, 0))]
        + [pl.BlockSpec(w.shape, lambda i: (0, 0)) for w in ws],
        out_specs=[pl.BlockSpec((tm, w.shape[1]), lambda i: (i, 0)) for w in ws],
        out_shape=[jax.ShapeDtypeStruct((t, w.shape[1]), F32) for w in ws],
        compiler_params=_params(("parallel",)),
        name="proj",
    )(x, *ws)


def _scan_rows(a, b):
    n = a.shape[0]
    row = _iota2(a.shape, 0)
    s = 1
    while s < n:
        a_sh = pltpu.roll(a, s, 0)
        b_sh = pltpu.roll(b, s, 0)
        m = row >= s
        b = jnp.where(m, a * b_sh + b, b)
        a = jnp.where(m, a * a_sh, a)
        s *= 2
    return a, b


def _rglru_kernel(a2_ref, buf_ref, h0_ref, cw_ref, cb_ref, wg_ref, bg_ref, lam_ref,
                  out_ref, nbuf_ref, hl_ref, xp_ref, hc_ref, *, tt):
    w = A_WIDTH

    @pl.when(pl.program_id(1) == 0)
    def _():
        xp_ref[0:ROW_PAD, :] = buf_ref[...]
        hc_ref[...] = h0_ref[...]

    gate_in = a2_ref[:, 0:w]
    x_in = a2_ref[:, w:2 * w]
    xp_ref[ROW_PAD:ROW_PAD + tt, :] = x_in
    xc = x_in * cw_ref[3:4, :] + cb_ref[...]
    for j in range(3):
        xc = xc + xp_ref[pl.ds(ROW_PAD - 3 + j, tt), :] * cw_ref[j:j + 1, :]
    gates = jnp.dot(xc.astype(BF16), wg_ref[...], preferred_element_type=F32) + bg_ref[...]
    r = _sigmoid(gates[:, 0:w])
    ig = _sigmoid(gates[:, w:2 * w])
    log_a = (-A_C) * r * _softplus(-lam_ref[...])
    a = jnp.exp(log_a)
    u = jnp.sqrt(1.0 - a * a) * (ig * xc)
    a_cum, h_loc = _scan_rows(a, u)
    h = a_cum * hc_ref[...] + h_loc
    out_ref[...] = h * _gelu_tanh(gate_in)
    hc_ref[...] = h[tt - 1:tt, :]
    hl_ref[...] = h[tt - 1:tt, :]
    last = x_in[tt - ROW_PAD:tt, :]
    xp_ref[0:ROW_PAD, :] = last
    nbuf_ref[...] = last


def _rglru(a2, buf8, h0, cw, cb, wg, bg, lam, tt):
    b, l, _ = a2.shape
    w = A_WIDTH
    const = lambda shape: pl.BlockSpec(shape, lambda bi, i: (0,) * len(shape))
    return pl.pallas_call(
        functools.partial(_rglru_kernel, tt=tt),
        grid=(b, l // tt),
        in_specs=[pl.BlockSpec((None, tt, 2 * w), lambda bi, i: (bi, i, 0)),
                  pl.BlockSpec((None, ROW_PAD, w), lambda bi, i: (bi, 0, 0)),
                  pl.BlockSpec((None, 1, w), lambda bi, i: (bi, 0, 0)),
                  const((4, w)), const((1, w)), const((w, 2 * w)), const((1, 2 * w)), const((1, w))],
        out_specs=[pl.BlockSpec((None, tt, w), lambda bi, i: (bi, i, 0)),
                   pl.BlockSpec((None, ROW_PAD, w), lambda bi, i: (bi, 0, 0)),
                   pl.BlockSpec((None, 1, w), lambda bi, i: (bi, 0, 0))],
        out_shape=[jax.ShapeDtypeStruct((b, l, w), F32),
                   jax.ShapeDtypeStruct((b, ROW_PAD, w), F32),
                   jax.ShapeDtypeStruct((b, 1, w), F32)],
        scratch_shapes=[pltpu.VMEM((ROW_PAD + tt, w), F32), pltpu.VMEM((1, w), F32)],
        compiler_params=_params(("parallel", "arbitrary")),
        name="rglru",
    )(a2, buf8, h0, cw, cb, wg, bg, lam)


def _gdn_kernel(qkv_ref, z_ref, gs_ref, buf_ref, s0_ref, cw_ref, alog_ref, dtb_ref, ng_ref,
                out_ref, nbuf_ref, sl_ref, xp_ref, s_ref):
    c = CHUNK
    dk = B_HEAD_DIM
    nh = B_HEADS

    @pl.when(pl.program_id(1) == 0)
    def _():
        xp_ref[0:ROW_PAD, :] = buf_ref[...]
        s_ref[...] = s0_ref[...]

    x_in = qkv_ref[...]
    xp_ref[ROW_PAD:ROW_PAD + c, :] = x_in
    xc = x_in * cw_ref[3:4, :]
    for j in range(3):
        xc = xc + xp_ref[pl.ds(ROW_PAD - 3 + j, c), :] * cw_ref[j:j + 1, :]
    xc = _silu(xc)
    last = x_in[c - ROW_PAD:c, :]
    xp_ref[0:ROW_PAD, :] = last
    nbuf_ref[...] = last

    gs = gs_ref[...]
    beta_all = _sigmoid(gs)
    g_all = -jnp.exp(alog_ref[...]) * _softplus(gs + dtb_ref[...])
    ri = _iota2((c, c), 0)
    ci = _iota2((c, c), 1)
    tri = (ri >= ci).astype(F32)
    upper = (ri > ci).astype(F32)
    for h in range(nh):
        q = xc[:, h * dk:(h + 1) * dk]
        k = xc[:, (nh + h) * dk:(nh + h + 1) * dk]
        v = xc[:, (2 * nh + h) * dk:(2 * nh + h + 1) * dk]
        q = q * lax.rsqrt(jnp.sum(q * q, axis=-1, keepdims=True) + 1e-6) * (dk ** -0.5)
        k = k * lax.rsqrt(jnp.sum(k * k, axis=-1, keepdims=True) + 1e-6)
        beta = beta_all[:, h:h + 1]
        g = g_all[:, nh + h:nh + h + 1]
        gmat = _mm_exact_lhs(tri, jnp.concatenate([g * upper, jnp.broadcast_to(g, (c, c))], axis=1))
        dlog = gmat[:, 0:c]
        gc = gmat[:, c:c + 1]
        g_last = gc[c - 1:c, :]
        decay = jnp.where(ri >= ci, jnp.exp(dlog), 0.0)
        kb = k * beta
        lmat = jnp.where(ri > ci, _mm(kb, k, trans_b=True) * decay, 0.0)
        tmat = _neumann_inv(-lmat)
        egc = jnp.exp(gc)
        uw = _mm(tmat, jnp.concatenate([v * beta, kb * egc], axis=1))
        u = uw[:, 0:dk]
        w = uw[:, dk:2 * dk]
        qk = _mm(q, k, trans_b=True) * decay
        q_in = q * egc
        k_out = k * jnp.exp(g_last - gc)
        s = s_ref[h]
        v_new = u - _mm(w, s)
        o = _mm(q_in, s) + _mm(qk, v_new)
        s_new = s * jnp.exp(g_last) + _mm(k_out, v_new, trans_a=True)
        s_ref[h] = s_new
        sl_ref[h] = s_new
        zh = z_ref[:, h * dk:(h + 1) * dk]
        out_ref[:, h * dk:(h + 1) * dk] = _rms_norm(o, ng_ref[...]) * _silu(zh)


def _gdn(qkv, z, gs, buf8, s0, cw, alog, dtb, ng):
    b, l, _ = qkv.shape
    c = CHUNK
    w3 = 3 * B_WIDTH
    const = lambda shape: pl.BlockSpec(shape, lambda bi, i: (0,) * len(shape))
    st = (B_HEADS, B_HEAD_DIM, B_HEAD_DIM)
    return pl.pallas_call(
        _gdn_kernel,
        grid=(b, l // c),
        in_specs=[pl.BlockSpec((None, c, w3), lambda bi, i: (bi, i, 0)),
                  pl.BlockSpec((None, c, B_WIDTH), lambda bi, i: (bi, i, 0)),
                  pl.BlockSpec((None, c, 128), lambda bi, i: (bi, i, 0)),
                  pl.BlockSpec((None, ROW_PAD, w3), lambda bi, i: (bi, 0, 0)),
                  pl.BlockSpec((None,) + st, lambda bi, i: (bi, 0, 0, 0)),
                  const((4, w3)), const((1, 128)), const((1, 128)), const((1, B_HEAD_DIM))],
        out_specs=[pl.BlockSpec((None, c, B_WIDTH), lambda bi, i: (bi, i, 0)),
                   pl.BlockSpec((None, ROW_PAD, w3), lambda bi, i: (bi, 0, 0)),
                   pl.BlockSpec((None,) + st, lambda bi, i: (bi, 0, 0, 0))],
        out_shape=[jax.ShapeDtypeStruct((b, l, B_WIDTH), F32),
                   jax.ShapeDtypeStruct((b, ROW_PAD, w3), F32),
                   jax.ShapeDtypeStruct((b,) + st, F32)],
        scratch_shapes=[pltpu.VMEM((ROW_PAD + c, w3), F32), pltpu.VMEM(st, F32)],
        compiler_params=_params(("parallel", "arbitrary")),
        name="gdn",
    )(qkv, z, gs, buf8, s0, cw, alog, dtb, ng)


def _outproj_kernel(a_ref, b_ref, x_ref, w_ref, g_ref, bb_ref, o_ref):
    half = a_ref.shape[1]
    mix = jnp.dot(a_ref[...].astype(BF16), w_ref[0:half, :], preferred_element_type=F32)
    mix = mix + jnp.dot(b_ref[...].astype(BF16), w_ref[half:2 * half, :], preferred_element_type=F32)
    o_ref[...] = _layer_norm(ALPHA * x_ref[...] + mix, g_ref[...], bb_ref[...])


def _outproj_ln(a, b, x, w, g, bb, tm):
    t, d = x.shape
    half = a.shape[1]
    const = lambda shape: pl.BlockSpec(shape, lambda i: (0,) * len(shape))
    return pl.pallas_call(
        _outproj_kernel,
        grid=(t // tm,),
        in_specs=[pl.BlockSpec((tm, half), lambda i: (i, 0)), pl.BlockSpec((tm, half), lambda i: (i, 0)),
                  pl.BlockSpec((tm, d), lambda i: (i, 0)), const((2 * half, d)), const((1, d)), const((1, d))],
        out_specs=pl.BlockSpec((tm, d), lambda i: (i, 0)),
        out_shape=jax.ShapeDtypeStruct((t, d), F32),
        compiler_params=_params(("parallel",)),
        name="outproj_ln",
    )(a, b, x, w, g, bb)


def _ffn_kernel(x_ref, pe_ref, buf_ref, wup_ref, cw_ref, cb_ref, wdn_ref, g_ref, b_ref,
                wg_ref, bg_ref, wp_ref, o_ref, nbuf_ref, prev_ref, act_ref, *, tt):
    ff = D_FF
    nb = ff // FF_BLOCK

    @pl.when(pl.program_id(1) == 0)
    def _():
        prev_ref[...] = buf_ref[...]

    x = x_ref[...]
    xb = x.astype(BF16)
    row = _iota2((tt, FF_BLOCK), 0)

    def conv(col):
        sl = slice(col, col + FF_BLOCK)
        h = jnp.dot(xb, wup_ref[:, sl], preferred_element_type=F32)
        p1 = prev_ref[ROW_PAD - 1:ROW_PAD, sl]
        p2 = prev_ref[ROW_PAD - 2:ROW_PAD - 1, sl]
        h1 = jnp.where(row == 0, p1, pltpu.roll(h, 1, 0))
        h2 = jnp.where(row == 0, p2, jnp.where(row == 1, p1, pltpu.roll(h, 2, 0)))
        prev_ref[:, sl] = h[tt - ROW_PAD:tt, :]
        return h2 * cw_ref[0:1, sl] + h1 * cw_ref[1:2, sl] + h * cw_ref[2:3, sl] + cb_ref[:, sl]

    for j in range(nb):
        gate = conv(j * FF_BLOCK)
        val = conv(ff + j * FF_BLOCK)
        act_ref[:, j * FF_BLOCK:(j + 1) * FF_BLOCK] = (_silu(gate) * val).astype(BF16)
    nbuf_ref[...] = prev_ref[...]
    f = jnp.dot(act_ref[...], wdn_ref[...], preferred_element_type=F32)
    y = _layer_norm(ALPHA * x + f, g_ref[...], b_ref[...])
    gate = _sigmoid(jnp.dot(y.astype(BF16), wg_ref[...], preferred_element_type=F32) + bg_ref[...])
    emb = jnp.dot(pe_ref[...].astype(BF16), wp_ref[...], preferred_element_type=F32)
    o_ref[...] = y + gate * emb


def _ffn(x, pe, buf8, wup, cw, cb, wdn, g, bb, wg, bg, wp, tt):
    b, l, d = x.shape
    ff2 = 2 * D_FF
    const = lambda shape: pl.BlockSpec(shape, lambda bi, i: (0,) * len(shape),
                                       pipeline_mode=pl.Buffered(1))
    return pl.pallas_call(
        functools.partial(_ffn_kernel, tt=tt),
        grid=(b, l // tt),
        in_specs=[pl.BlockSpec((None, tt, d), lambda bi, i: (bi, i, 0)),
                  pl.BlockSpec((None, tt, PLE_DIM), lambda bi, i: (bi, i, 0)),
                  pl.BlockSpec((None, ROW_PAD, ff2), lambda bi, i: (bi, 0, 0)),
                  const((d, ff2)), const((3, ff2)), const((1, ff2)), const((D_FF, d)),
                  const((1, d)), const((1, d)), const((d, d)), const((1, d)), const((PLE_DIM, d))],
        out_specs=[pl.BlockSpec((None, tt, d), lambda bi, i: (bi, i, 0)),
                   pl.BlockSpec((None, ROW_PAD, ff2), lambda bi, i: (bi, 0, 0))],
        out_shape=[jax.ShapeDtypeStruct((b, l, d), F32),
                   jax.ShapeDtypeStruct((b, ROW_PAD, ff2), F32)],
        scratch_shapes=[pltpu.VMEM((ROW_PAD, ff2), F32), pltpu.VMEM((tt, D_FF), BF16)],
        compiler_params=_params(("parallel", "arbitrary")),
        name="ffn",
    )(x, pe, buf8, wup, cw, cb, wdn, g, bb, wg, bg, wp)


def _rwkv_kernel(p_ref, sh_ref, s0_ref, mu_ref, w0_ref, w2_ref, a0_ref, a2_ref, g2_ref, kk_ref,
                 ka_ref, rk_ref, lg_ref, lb_ref, seg_ref,
                 out_ref, nsh_ref, sl_ref, xp_ref, s_ref, o_ref):
    c = CHUNK
    w = C_WIDTH
    n = C_HEAD_DIM

    @pl.when(pl.program_id(1) == 0)
    def _():
        xp_ref[0:ROW_PAD, :] = sh_ref[...]
        s_ref[...] = s0_ref[...]

    proj = p_ref[...]
    xp_ref[ROW_PAD:ROW_PAD + c, :] = proj
    prev = xp_ref[pl.ds(ROW_PAD - 1, c), :]
    last = proj[c - ROW_PAD:c, :]
    xp_ref[0:ROW_PAD, :] = last
    nsh_ref[...] = last
    xs = proj + (prev - proj) * mu_ref[...]

    r = xs[:, 0:w]
    k = xs[:, w:2 * w]
    v = xs[:, 2 * w:3 * w]
    lora_in = xs[:, 3 * w:3 * w + 128]
    ww = -_softplus(-(w0_ref[...] + _mm(jnp.tanh(lora_in), w2_ref[...]))) - 0.5
    a = _sigmoid(a0_ref[...] + _mm(lora_in, a2_ref[...]))
    g = _mm(_sigmoid(xs[:, 3 * w + 128:3 * w + 256]), g2_ref[...])
    seg = seg_ref[...]
    kx = k * kk_ref[...]
    kk = kx * lax.rsqrt(_mm_exact_rhs(kx * kx, seg) + 1e-6)
    k = k * (1.0 + (a - 1.0) * ka_ref[...])
    lw = -jnp.exp(ww)
    sa = -kk
    sb = kk * a

    ri = _iota2((c, c), 0)
    ci = _iota2((c, c), 1)
    tri = (ri >= ci).astype(F32)
    cum = _mm_exact_lhs(tri, lw)
    cum_last = cum[c - 1:c, :]
    e_in = jnp.exp(cum)
    e_out = jnp.exp(-cum)
    e_end = jnp.exp(cum_last - cum)
    a_t = sa * jnp.exp(cum - lw)
    r_t = r * e_in
    b_t = sb * e_out
    k_t = k * e_out
    b_o = sb * e_end
    k_o = k * e_end
    e_last = jnp.exp(cum_last)

    for h in range(C_HEADS):
        sl = slice(h * n, (h + 1) * n)
        s = s_ref[h]
        lr = jnp.concatenate([a_t[:, sl], r_t[:, sl]], axis=0)
        g1 = _mm(lr, b_t[:, sl], trans_b=True)
        g2 = _mm(lr, k_t[:, sl], trans_b=True)
        m_ab = jnp.where(ri > ci, g1[0:c], 0.0)
        m_rb = jnp.where(ri >= ci, g1[c:2 * c], 0.0)
        m_ak = jnp.where(ri > ci, g2[0:c], 0.0)
        m_rk = jnp.where(ri >= ci, g2[c:2 * c], 0.0)
        tmat = _neumann_inv(m_ab)
        vh = v[:, sl]
        ls = _mm(lr, s, trans_b=True)
        u = _mm(tmat, ls[0:c] + _mm(m_ak, vh))
        o_ref[:, sl] = ls[c:2 * c] + _mm(m_rb, u) + _mm(m_rk, vh)
        uv = jnp.concatenate([u, vh], axis=0)
        bk = jnp.concatenate([b_o[:, sl], k_o[:, sl]], axis=0)
        s_new = s * e_last[:, sl] + _mm(uv, bk, trans_a=True)
        s_ref[h] = s_new
        sl_ref[h] = s_new

    o = o_ref[...]
    inv_n = 1.0 / n
    mean = _mm_exact_rhs(o, seg) * inv_n
    oc = o - mean
    var = _mm_exact_rhs(oc * oc, seg) * inv_n
    on = oc * lax.rsqrt(var + C_LNX_EPS) * lg_ref[...] + lb_ref[...]
    bonus = _mm_exact_rhs(r * k * rk_ref[...], seg) * v
    out_ref[...] = (on + bonus) * g


def _rwkv(proj, sh8, s0, mu, w0, w2p, a0, a2p, g2, kk, ka, rk, lg, lb, seg):
    b, l, cp = proj.shape
    c = CHUNK
    w = C_WIDTH
    const = lambda shape: pl.BlockSpec(shape, lambda bi, i: (0,) * len(shape))
    st = (C_HEADS, C_HEAD_DIM, C_HEAD_DIM)
    return pl.pallas_call(
        _rwkv_kernel,
        grid=(b, l // c),
        in_specs=[pl.BlockSpec((None, c, cp), lambda bi, i: (bi, i, 0)),
                  pl.BlockSpec((None, ROW_PAD, cp), lambda bi, i: (bi, 0, 0)),
                  pl.BlockSpec((None,) + st, lambda bi, i: (bi, 0, 0, 0)),
                  const((1, cp)), const((1, w)), const((128, w)), const((1, w)), const((128, w)),
                  const((128, w)), const((1, w)), const((1, w)), const((1, w)), const((1, w)),
                  const((1, w)), const((w, w))],
        out_specs=[pl.BlockSpec((None, c, w), lambda bi, i: (bi, i, 0)),
                   pl.BlockSpec((None, ROW_PAD, cp), lambda bi, i: (bi, 0, 0)),
                   pl.BlockSpec((None,) + st, lambda bi, i: (bi, 0, 0, 0))],
        out_shape=[jax.ShapeDtypeStruct((b, l, w), F32),
                   jax.ShapeDtypeStruct((b, ROW_PAD, cp), F32),
                   jax.ShapeDtypeStruct((b,) + st, F32)],
        scratch_shapes=[pltpu.VMEM((ROW_PAD + c, cp), F32), pltpu.VMEM(st, F32),
                        pltpu.VMEM((c, w), F32)],
        compiler_params=_params(("parallel", "arbitrary")),
        name="rwkv7",
    )(proj, sh8, s0, mu, w0, w2p, a0, a2p, g2, kk, ka, rk, lg, lb, seg)


def _mla_prep_kernel(d_ref, ct_ref, st_ref, kc_ref, qg_ref, wq_ref, wqs_ref, kg_ref,
                     q_ref, c_ref, kr_ref):
    lq = D_Q_LORA
    lkv = D_KV_LORA
    qn = _rms_norm(d_ref[:, 0:lq], qg_ref[...]).astype(BF16)
    ct = ct_ref[...]
    st = st_ref[...]
    for h in range(D_HEADS):
        sl = slice(h * D_QK_PAD, (h + 1) * D_QK_PAD)
        q = jnp.dot(qn, wq_ref[:, sl], preferred_element_type=F32) * ct
        q = q + jnp.dot(qn, wqs_ref[:, sl], preferred_element_type=F32) * st
        q_ref[:, sl] = (q * MLA_SCALE).astype(BF16)
    c_ref[...] = _rms_norm(d_ref[:, lq:lq + lkv], kg_ref[...])
    o = lq + lkv
    kr_ref[...] = d_ref[:, o:o + D_QK_PAD] * kc_ref[...] + d_ref[:, o + D_QK_PAD:o + 2 * D_QK_PAD] * st


def _mla_prep(d, ctab, stab, kctab, qg, wq, wqs, kg, tm):
    b, l, dw = d.shape
    hq = D_HEADS * D_QK_PAD
    const = lambda shape: pl.BlockSpec(shape, lambda bi, i: (0,) * len(shape))
    tab = pl.BlockSpec((tm, D_QK_PAD), lambda bi, i: (i, 0))
    return pl.pallas_call(
        _mla_prep_kernel,
        grid=(b, l // tm),
        in_specs=[pl.BlockSpec((None, tm, dw), lambda bi, i: (bi, i, 0)), tab, tab, tab,
                  const((1, D_Q_LORA)), const((D_Q_LORA, hq)), const((D_Q_LORA, hq)),
                  const((1, D_KV_LORA))],
        out_specs=[pl.BlockSpec((None, tm, hq), lambda bi, i: (bi, i, 0)),
                   pl.BlockSpec((None, tm, D_KV_LORA), lambda bi, i: (bi, i, 0)),
                   pl.BlockSpec((None, tm, D_QK_PAD), lambda bi, i: (bi, i, 0))],
        out_shape=[jax.ShapeDtypeStruct((b, l, hq), BF16),
                   jax.ShapeDtypeStruct((b, l, D_KV_LORA), F32),
                   jax.ShapeDtypeStruct((b, l, D_QK_PAD), F32)],
        compiler_params=_params(("parallel", "parallel")),
        name="mla_prep",
    )(d, ctab, stab, kctab, qg, wq, wqs, kg)


def _kv_kernel(c_ref, kr_ref, wk_ref, wv_ref, k_ref, v_ref):
    cb = c_ref[...].astype(BF16)
    kr = kr_ref[...]
    for h in range(D_HEADS):
        sl = slice(h * D_QK_PAD, (h + 1) * D_QK_PAD)
        k_ref[:, sl] = (jnp.dot(cb, wk_ref[:, sl], preferred_element_type=F32) + kr).astype(BF16)
    v_ref[...] = jnp.dot(cb, wv_ref[...], preferred_element_type=F32).astype(BF16)


def _kv(c, krp, wk, wv, tm):
    t = c.shape[0]
    hk = D_HEADS * D_QK_PAD
    hv = D_HEADS * D_V
    const = lambda shape: pl.BlockSpec(shape, lambda i: (0,) * len(shape))
    return pl.pallas_call(
        _kv_kernel,
        grid=(t // tm,),
        in_specs=[pl.BlockSpec((tm, D_KV_LORA), lambda i: (i, 0)),
                  pl.BlockSpec((tm, D_QK_PAD), lambda i: (i, 0)),
                  const((D_KV_LORA, hk)), const((D_KV_LORA, hv))],
        out_specs=[pl.BlockSpec((tm, hk), lambda i: (i, 0)), pl.BlockSpec((tm, hv), lambda i: (i, 0))],
        out_shape=[jax.ShapeDtypeStruct((t, hk), BF16), jax.ShapeDtypeStruct((t, hv), BF16)],
        compiler_params=_params(("parallel",)),
        name="mla_kv",
    )(c, krp, wk, wv)


def _attn_kernel(q_ref, k_ref, v_ref, o_ref, m_ref, l_ref, acc_ref, *, tq, tk, nk, q_off):
    q_first = q_off + pl.program_id(2) * tq
    last_chunk_end = (q_first + tq - 1) // CHUNK * CHUNK + CHUNK
    n_live = jnp.minimum(nk, (last_chunk_end + tk - 1) // tk)
    q = q_ref[...]
    m_ref[...] = jnp.full(m_ref.shape, -jnp.inf, F32)
    l_ref[...] = jnp.zeros(l_ref.shape, F32)
    acc_ref[...] = jnp.zeros(acc_ref.shape, F32)
    q_chunk = jnp.right_shift(q_first + _iota2((tq, tk), 0), CHUNK_SHIFT)
    k_index = _iota2((tq, tk), 1)

    def body(j, carry):
        start = pl.multiple_of(j * tk, tk)
        s = lax.dot_general(q, k_ref[pl.ds(start, tk), :], _dims(False, True), preferred_element_type=F32)
        s = jnp.where(jnp.right_shift(k_index + start, CHUNK_SHIFT) <= q_chunk, s, -jnp.inf)
        m_old = m_ref[...]
        m_new = jnp.maximum(m_old, jnp.max(s, axis=-1, keepdims=True))
        alpha = jnp.exp(m_old - m_new)
        p = jnp.exp(s - m_new)
        l_ref[...] = alpha * l_ref[...] + jnp.sum(p, axis=-1, keepdims=True)
        acc_ref[...] = alpha * acc_ref[...] + jnp.dot(p.astype(BF16), v_ref[pl.ds(start, tk), :],
                                                      preferred_element_type=F32)
        m_ref[...] = m_new
        return carry

    lax.fori_loop(0, n_live, body, 0)
    o_ref[...] = acc_ref[...] / l_ref[...]


def _attention(q, k, v, tq, tk, q_off):
    b, l, _ = q.shape
    lk = k.shape[1]
    return pl.pallas_call(
        functools.partial(_attn_kernel, tq=tq, tk=tk, nk=lk // tk, q_off=q_off),
        grid=(b, D_HEADS, l // tq),
        in_specs=[pl.BlockSpec((None, tq, D_QK_PAD), lambda bi, h, i: (bi, i, h)),
                  pl.BlockSpec((None, lk, D_QK_PAD), lambda bi, h, i: (bi, 0, h)),
                  pl.BlockSpec((None, lk, D_V), lambda bi, h, i: (bi, 0, h))],
        out_specs=pl.BlockSpec((None, tq, D_V), lambda bi, h, i: (bi, i, h)),
        out_shape=jax.ShapeDtypeStruct((b, l, D_HEADS * D_V), F32),
        scratch_shapes=[pltpu.VMEM((tq, 1), F32), pltpu.VMEM((tq, 1), F32), pltpu.VMEM((tq, D_V), F32)],
        compiler_params=_params(("parallel", "parallel", "arbitrary")),
        name="mla_attention",
    )(q, k, v)


def _pad_rows_front(buf):
    return jnp.pad(buf, ((0, 0), (ROW_PAD - buf.shape[1], 0), (0, 0)))


def _block_diag(w):
    nb, bi, bj = w.shape
    eye = jnp.eye(nb, dtype=w.dtype)
    return (eye[:, None, :, None] * w[:, :, None, :]).reshape(nb * bi, nb * bj)


def _rope_swap(w):
    half = w.shape[-1] // 2
    return jnp.concatenate([-w[..., half:], w[..., :half]], axis=-1)


def _prepare(w_in0, a_w_r, a_w_i, a_b_r, a_b_i, w_in1, c_w2, c_a2, d_w_qb, d_w_kvb):
    p = {}
    o = 0
    p["w_a"] = w_in0[:, o:o + 2 * A_WIDTH].astype(BF16); o += 2 * A_WIDTH
    p["w_qkv"] = w_in0[:, o:o + 3 * B_WIDTH].astype(BF16); o += 3 * B_WIDTH
    p["w_z"] = w_in0[:, o:o + B_WIDTH].astype(BF16); o += B_WIDTH
    p["w_gs"] = jnp.pad(w_in0[:, o:o + 2 * B_HEADS], ((0, 0), (0, 128 - 2 * B_HEADS))).astype(BF16)
    p["a_wg"] = jnp.concatenate([_block_diag(a_w_r), _block_diag(a_w_i)], axis=1).astype(BF16)
    p["a_bg"] = jnp.concatenate([a_b_r, a_b_i])[None, :]
    p["w_c"] = w_in1[:, 0:C_PROJ].astype(BF16)
    o = C_PROJ
    w_qa = w_in1[:, o:o + D_Q_LORA]; o += D_Q_LORA
    w_craw = w_in1[:, o:o + D_KV_LORA]; o += D_KV_LORA
    w_kr = w_in1[:, o:o + D_ROPE]
    place = lambda w: jnp.pad(w, ((0, 0), (D_NOPE, D_QK_PAD - D_NOPE - D_ROPE)))
    p["w_d"] = jnp.concatenate([w_qa, w_craw, place(w_kr), place(_rope_swap(w_kr))], axis=1).astype(BF16)
    p["c_w2p"] = jnp.pad(c_w2, ((0, 64), (0, 0)))
    p["c_a2p"] = jnp.pad(c_a2, ((64, 0), (0, 0)))
    wq = d_w_qb.reshape(D_Q_LORA, D_HEADS, D_NOPE + D_ROPE)
    zpad = jnp.zeros((D_Q_LORA, D_HEADS, D_QK_PAD - D_NOPE - D_ROPE), F32)
    p["wq"] = jnp.concatenate([wq, zpad], axis=-1).reshape(D_Q_LORA, D_HEADS * D_QK_PAD).astype(BF16)
    wq_sw = jnp.concatenate([jnp.zeros((D_Q_LORA, D_HEADS, D_NOPE), F32), _rope_swap(wq[..., D_NOPE:]), zpad], axis=-1)
    p["wqs"] = wq_sw.reshape(D_Q_LORA, D_HEADS * D_QK_PAD).astype(BF16)
    wkv = d_w_kvb.reshape(D_KV_LORA, D_HEADS, D_NOPE + D_V)
    wk = jnp.pad(wkv[..., :D_NOPE], ((0, 0), (0, 0), (0, D_QK_PAD - D_NOPE)))
    p["wk"] = wk.reshape(D_KV_LORA, D_HEADS * D_QK_PAD).astype(BF16)
    p["wv"] = wkv[..., D_NOPE:].reshape(D_KV_LORA, D_HEADS * D_V).astype(BF16)
    lane = jnp.arange(C_WIDTH) // C_HEAD_DIM
    p["seg"] = (lane[:, None] == lane[None, :]).astype(BF16)
    return p


def _rope_tables(pos):
    half = D_ROPE // 2
    inv = ROPE_THETA ** (-jnp.arange(half, dtype=F32) / half)
    ang = pos.astype(F32)[:, None] * inv[None, :]
    cos = jnp.cos(ang)
    sin = jnp.sin(ang)
    n = pos.shape[0]
    tail = jnp.zeros((n, D_QK_PAD - D_NOPE - D_ROPE), F32)
    cos2 = jnp.concatenate([cos, cos], axis=1)
    sin2 = jnp.concatenate([sin, sin], axis=1)
    q_cos = jnp.concatenate([jnp.ones((n, D_NOPE), F32), cos2, tail], axis=1)
    q_sin = jnp.concatenate([jnp.zeros((n, D_NOPE), F32), sin2, tail], axis=1)
    k_cos = jnp.concatenate([jnp.zeros((n, D_NOPE), F32), cos2, tail], axis=1)
    return q_cos, q_sin, k_cos


def _trunk(x, pe, pos, a_conv, a_h, b_conv, b_s, c_shift, c_s, d_ckv, d_krope, f_conv, wts, prm):
    (a_conv_w, a_conv_b, a_lambda, b_conv_w, b_a_log, b_dt_bias, b_norm_g, w_out0, c_mu, c_w0,
     c_a0, c_g2, c_k_k, c_k_a, c_r_k, c_lnx_g, c_lnx_b, d_qa_g, d_kva_g, w_out1, ln_mix_g, ln_mix_b,
     ffn_w_up, ffn_conv_w, ffn_conv_b, ffn_w_down, ln_ffn_g, ln_ffn_b, ple_w_gate, ple_b_gate,
     ple_w_proj) = wts
    b, l, d = x.shape
    t = b * l
    tm = min(512, t)
    tt = min(512, l)
    row = lambda vec: vec.reshape(1, -1)
    f_new = []

    def ffn_block(i, xin):
        y, nbuf = _ffn(xin.reshape(b, l, d), pe[i], _pad_rows_front(f_conv[i]),
                       ffn_w_up[i].astype(BF16), ffn_conv_w[i], row(ffn_conv_b[i]),
                       ffn_w_down[i].astype(BF16), row(ln_ffn_g[i]), row(ln_ffn_b[i]),
                       ple_w_gate[i].astype(BF16), row(ple_b_gate[i]), ple_w_proj[i].astype(BF16), tt)
        f_new.append(nbuf[:, ROW_PAD - 2:, :])
        return y.reshape(t, d)

    x2 = x.reshape(t, d)
    a2, qkv, z, gs = _proj(x2, [prm["w_a"], prm["w_qkv"], prm["w_z"], prm["w_gs"]], tm)
    a_out, a_buf, a_hl = _rglru(a2.reshape(b, l, -1), _pad_rows_front(a_conv), a_h[:, None, :],
                                a_conv_w, row(a_conv_b), prm["a_wg"], prm["a_bg"], row(a_lambda),
                                min(256, l))
    pad8 = lambda vec: jnp.pad(vec, (B_HEADS, 128 - 2 * B_HEADS))[None, :]
    b_out, b_buf, b_sl = _gdn(qkv.reshape(b, l, -1), z.reshape(b, l, -1), gs.reshape(b, l, -1),
                              _pad_rows_front(b_conv), b_s, b_conv_w, pad8(b_a_log), pad8(b_dt_bias),
                              row(b_norm_g))
    x2 = _outproj_ln(a_out.reshape(t, -1), b_out.reshape(t, -1), x2, w_out0.astype(BF16),
                     row(ln_mix_g[0]), row(ln_mix_b[0]), tm)
    x2 = ffn_block(0, x2)

    c_in, d_in = _proj(x2, [prm["w_c"], prm["w_d"]], tm)
    c_out, c_sh, c_sl = _rwkv(c_in.reshape(b, l, -1), _pad_rows_front(c_shift), c_s, row(c_mu), row(c_w0),
                              prm["c_w2p"], row(c_a0), prm["c_a2p"], c_g2, row(c_k_k), row(c_k_a),
                              row(c_r_k.reshape(-1)), row(c_lnx_g), row(c_lnx_b), prm["seg"])
    q_cos, q_sin, k_cos = _rope_tables(pos)
    q, c_new, kr_pad = _mla_prep(d_in.reshape(b, l, -1), q_cos, q_sin, k_cos, row(d_qa_g), prm["wq"],
                                 prm["wqs"], row(d_kva_g), min(512, l))
    past = d_ckv.shape[1]
    if past:
        c_all = jnp.concatenate([d_ckv, c_new], axis=1)
        kr_old = jnp.pad(d_krope, ((0, 0), (0, 0), (D_NOPE, D_QK_PAD - D_NOPE - D_ROPE)))
        kr_all = jnp.concatenate([kr_old, kr_pad], axis=1)
    else:
        c_all, kr_all = c_new, kr_pad
    lk = past + l
    tkv = b * lk
    k, v = _kv(c_all.reshape(tkv, -1), kr_all.reshape(tkv, -1), prm["wk"], prm["wv"],
               512 if tkv % 512 == 0 else lk)
    tq = min(512, l)
    tk = 512 if lk % 512 == 0 else lk
    d_out = _attention(q, k.reshape(b, lk, -1), v.reshape(b, lk, -1), tq, tk, past)
    x2 = _outproj_ln(c_out.reshape(t, -1), d_out.reshape(t, -1), x2, w_out1.astype(BF16),
                     row(ln_mix_g[1]), row(ln_mix_b[1]), tm)
    x2 = ffn_block(1, x2)

    return (x2.reshape(b, l, d), a_buf[:, ROW_PAD - 3:, :], a_hl[:, 0, :], b_buf[:, ROW_PAD - 3:, :], b_sl,
            c_sh[:, ROW_PAD - 1:, :], c_sl, c_new, kr_pad[:, :, D_NOPE:D_NOPE + D_ROPE], jnp.stack(f_new))


def kernel(x_prompt, x_sample, state_a_conv, state_a_h, state_b_conv, state_b_s, state_c_shift, state_c_s, cache_d_ckv, cache_d_krope, state_ffn_conv, p_prompt, p_sample, w_in0, a_conv_w, a_conv_b, a_w_r, a_b_r, a_w_i, a_b_i, a_lambda, b_conv_w, b_a_log, b_dt_bias, b_norm_g, w_out0, w_in1, c_mu, c_w0, c_w2, c_a0, c_a2, c_g2, c_k_k, c_k_a, c_r_k, c_lnx_g, c_lnx_b, d_qa_g, d_w_qb, d_kva_g, d_w_kvb, w_out1, ln_mix_g, ln_mix_b, ffn_w_up, ffn_conv_w, ffn_conv_b, ffn_w_down, ln_ffn_g, ln_ffn_b, ple_w_gate, ple_b_gate, ple_w_proj):
    prm = _prepare(w_in0, a_w_r, a_w_i, a_b_r, a_b_i, w_in1, c_w2, c_a2, d_w_qb, d_w_kvb)
    wts = (a_conv_w, a_conv_b, a_lambda, b_conv_w, b_a_log, b_dt_bias, b_norm_g, w_out0, c_mu, c_w0,
           c_a0, c_g2, c_k_k, c_k_a, c_r_k, c_lnx_g, c_lnx_b, d_qa_g, d_kva_g, w_out1, ln_mix_g, ln_mix_b,
           ffn_w_up, ffn_conv_w, ffn_conv_b, ffn_w_down, ln_ffn_g, ln_ffn_b, ple_w_gate, ple_b_gate,
           ple_w_proj)
    dt = x_prompt.dtype
    bp, lp = x_prompt.shape[0], x_prompt.shape[1]
    zeros = lambda *shape: jnp.zeros(shape, dt)
    prompt = _trunk(x_prompt, p_prompt, jnp.arange(lp),
                    zeros(bp, 3, A_WIDTH), zeros(bp, A_WIDTH), zeros(bp, 3, 3 * B_WIDTH),
                    zeros(bp, B_HEADS, B_HEAD_DIM, B_HEAD_DIM), zeros(bp, 1, C_PROJ),
                    zeros(bp, C_HEADS, C_HEAD_DIM, C_HEAD_DIM), zeros(bp, 0, D_KV_LORA),
                    zeros(bp, 0, D_ROPE), zeros(DEPTH, bp, 2, 2 * D_FF), wts, prm)
    ls = x_sample.shape[1]
    past = cache_d_ckv.shape[1]
    sample = _trunk(x_sample, p_sample, past + jnp.arange(ls), state_a_conv, state_a_h, state_b_conv,
                    state_b_s, state_c_shift, state_c_s, cache_d_ckv, cache_d_krope, state_ffn_conv,
                    wts, prm)
    out = [prompt[0], sample[0]]
    for ps, ss in zip(prompt[1:], sample[1:]):
        out += [ps, ss]
    return tuple(out)
```

```python
import functools

import jax
import jax.numpy as jnp
from jax import lax
from jax.experimental import pallas as pl
from jax.experimental.pallas import tpu as pltpu

F32 = jnp.float32
BF16 = jnp.bfloat16

D_MODEL = 1024
DEPTH = 2
CHUNK = 64
CHUNK_SHIFT = CHUNK.bit_length() - 1
PLE_DIM = 256
ALPHA = (2 * DEPTH) ** 0.25
NORM_EPS = 1e-5
A_WIDTH = 512
A_BLOCKS = 8
A_C = 8.0
B_HEADS = 4
B_HEAD_DIM = 128
B_WIDTH = 512
C_HEAD_DIM = 64
C_WIDTH = 512
C_HEADS = 8
C_LNX_EPS = 64e-5
C_PROJ = 3 * C_WIDTH + 64 + 64 + 128
D_HEADS = 4
D_Q_LORA = 384
D_KV_LORA = 256
D_NOPE = 128
D_ROPE = 64
D_V = 128
D_QK_PAD = 256
D_V_PAD = 256
MLA_SCALE = (D_NOPE + D_ROPE) ** -0.5
ROPE_THETA = 10000.0
D_FF = 2816
FF_BLOCK = 256
MXU_TILE = 256
ROW_PAD = 8
VMEM_LIMIT = 56 * 1024 * 1024
B_PAIR = MXU_TILE // B_HEAD_DIM
C_GROUP = MXU_TILE // C_HEAD_DIM


def _params(sem):
    return pltpu.CompilerParams(dimension_semantics=sem, vmem_limit_bytes=VMEM_LIMIT)


def _sigmoid(x):
    return 1.0 / (1.0 + jnp.exp(-x))


def _softplus(x):
    return jnp.maximum(x, 0.0) + jnp.log(1.0 + jnp.exp(-jnp.abs(x)))


def _silu(x):
    return x * _sigmoid(x)


def _gelu_tanh(x):
    return 0.5 * x * (1.0 + jnp.tanh(0.7978845608028654 * (x + 0.044715 * x * x * x)))


def _dims(trans_a, trans_b):
    return (((0 if trans_a else 1,), (1 if trans_b else 0,)), ((), ()))


def _mm(a, b, trans_a=False, trans_b=False):
    return lax.dot_general(a.astype(BF16), b.astype(BF16), _dims(trans_a, trans_b),
                           preferred_element_type=F32)


def _split3(a):
    a1 = a.astype(BF16)
    r1 = a - a1.astype(F32)
    a2 = r1.astype(BF16)
    return a1, a2, (r1 - a2.astype(F32)).astype(BF16)


def _mm_exact_rhs(a, mask):
    a1, a2, a3 = _split3(a)
    mb = mask.astype(BF16)
    out = jnp.dot(a3, mb, preferred_element_type=F32)
    out = out + jnp.dot(a2, mb, preferred_element_type=F32)
    return out + jnp.dot(a1, mb, preferred_element_type=F32)


def _mm_exact_lhs(mask, b):
    b1, b2, b3 = _split3(b)
    mb = mask.astype(BF16)
    out = jnp.dot(mb, b3, preferred_element_type=F32)
    out = out + jnp.dot(mb, b2, preferred_element_type=F32)
    return out + jnp.dot(mb, b1, preferred_element_type=F32)


def _iota2(shape, axis):
    return lax.broadcasted_iota(jnp.int32, shape, axis)


def _blk(idx, size):
    return jnp.right_shift(idx, size.bit_length() - 1)


def _bd_mask(shape, row_block, col_block):
    return _blk(_iota2(shape, 0), row_block) == _blk(_iota2(shape, 1), col_block)


def _bd_rows(x, mask):
    reps = mask.shape[0] // x.shape[0]
    return jnp.where(mask, jnp.concatenate([x] * reps, axis=0), 0.0)


def _split_f32(a):
    hi = a.astype(BF16).astype(F32)
    return hi, a - hi


def _neumann_inv_cat(xs, bd):
    c, wd = xs[0].shape
    eye = (_iota2((c, wd), 0) == (_iota2((c, wd), 1) & (c - 1))).astype(F32)

    def operands(a):
        hi, lo = _split_f32(a)
        return hi.astype(BF16), lo.astype(BF16), _bd_rows(hi, bd).astype(BF16), _bd_rows(lo, bd).astype(BF16)

    def prod(lh, ll, bh, bl):
        out = jnp.dot(lh, bl, preferred_element_type=F32)
        out = out + jnp.dot(ll, bh, preferred_element_type=F32)
        return out + jnp.dot(lh, bh, preferred_element_type=F32)

    ts = [eye + x for x in xs]
    ps = [operands(x) for x in xs]
    k = 2
    while k < c:
        ps = [operands(prod(*p)) for p in ps]
        nxt = []
        for t, p in zip(ts, ps):
            hi, lo = _split_f32(t)
            nxt.append(t + prod(hi.astype(BF16), lo.astype(BF16), p[2], p[3]))
        ts = nxt
        k *= 2
    return ts


def _layer_norm(x, g, b):
    mu = jnp.mean(x, axis=-1, keepdims=True)
    xc = x - mu
    var = jnp.mean(xc * xc, axis=-1, keepdims=True)
    return xc * lax.rsqrt(var + NORM_EPS) * g + b


def _rms_norm(x, g, eps=1e-6):
    return x * lax.rsqrt(jnp.mean(x * x, axis=-1, keepdims=True) + eps) * g


def _proj_kernel(x_ref, *refs):
    n = len(refs) // 2
    xb = x_ref[...].astype(BF16)
    for w_ref, o_ref in zip(refs[:n], refs[n:]):
        o_ref[...] = jnp.dot(xb, w_ref[...], preferred_element_type=F32)


def _proj(x, ws, tm):
    t, k = x.shape
    return pl.pallas_call(
        _proj_kernel,
        grid=(t // tm,),
        in_specs=[pl.BlockSpec((tm, k), lambda i: (i, 0))]
        + [pl.BlockSpec(w.shape, lambda i: (0, 0)) for w in ws],
        out_specs=[pl.BlockSpec((tm, w.shape[1]), lambda i: (i, 0)) for w in ws],
        out_shape=[jax.ShapeDtypeStruct((t, w.shape[1]), F32) for w in ws],
        compiler_params=_params(("parallel",)),
        name="proj",
    )(x, *ws)


def _scan_rows(a, b):
    n = a.shape[0]
    row = _iota2(a.shape, 0)
    s = 1
    while s < n:
        a_sh = pltpu.roll(a, s, 0)
        b_sh = pltpu.roll(b, s, 0)
        m = row >= s
        b = jnp.where(m, a * b_sh + b, b)
        a = jnp.where(m, a * a_sh, a)
        s *= 2
    return a, b


def _rglru_kernel(a2_ref, buf_ref, h0_ref, cw_ref, cb_ref, wg_ref, bg_ref, lam_ref,
                  out_ref, nbuf_ref, hl_ref, xp_ref, hc_ref, *, tt):
    w = A_WIDTH

    @pl.when(pl.program_id(1) == 0)
    def _():
        xp_ref[0:ROW_PAD, :] = buf_ref[...]
        hc_ref[...] = h0_ref[...]

    gate_in = a2_ref[:, 0:w]
    x_in = a2_ref[:, w:2 * w]
    xp_ref[ROW_PAD:ROW_PAD + tt, :] = x_in
    xc = x_in * cw_ref[3:4, :] + cb_ref[...]
    for j in range(3):
        xc = xc + xp_ref[ROW_PAD - 3 + j:ROW_PAD - 3 + j + tt, :] * cw_ref[j:j + 1, :]
    gates = jnp.dot(xc.astype(BF16), wg_ref[...], preferred_element_type=F32) + bg_ref[...]
    r = _sigmoid(gates[:, 0:w])
    ig = _sigmoid(gates[:, w:2 * w])
    log_a = (-A_C) * r * _softplus(-lam_ref[...])
    a = jnp.exp(log_a)
    u = jnp.sqrt(1.0 - a * a) * (ig * xc)
    a_cum, h_loc = _scan_rows(a, u)
    h = a_cum * hc_ref[...] + h_loc
    out_ref[...] = h * _gelu_tanh(gate_in)
    hc_ref[...] = h[tt - 1:tt, :]
    hl_ref[...] = h[tt - 1:tt, :]
    last = x_in[tt - ROW_PAD:tt, :]
    xp_ref[0:ROW_PAD, :] = last
    nbuf_ref[...] = last


def _rglru(a2, buf8, h0, cw, cb, wg, bg, lam, tt):
    b, l, _ = a2.shape
    w = A_WIDTH
    const = lambda shape: pl.BlockSpec(shape, lambda bi, i: (0,) * len(shape))
    return pl.pallas_call(
        functools.partial(_rglru_kernel, tt=tt),
        grid=(b, l // tt),
        in_specs=[pl.BlockSpec((None, tt, 2 * w), lambda bi, i: (bi, i, 0)),
                  pl.BlockSpec((None, ROW_PAD, w), lambda bi, i: (bi, 0, 0)),
                  pl.BlockSpec((None, 1, w), lambda bi, i: (bi, 0, 0)),
                  const((4, w)), const((1, w)), const((w, 2 * w)), const((1, 2 * w)), const((1, w))],
        out_specs=[pl.BlockSpec((None, tt, w), lambda bi, i: (bi, i, 0)),
                   pl.BlockSpec((None, ROW_PAD, w), lambda bi, i: (bi, 0, 0)),
                   pl.BlockSpec((None, 1, w), lambda bi, i: (bi, 0, 0))],
        out_shape=[jax.ShapeDtypeStruct((b, l, w), F32),
                   jax.ShapeDtypeStruct((b, ROW_PAD, w), F32),
                   jax.ShapeDtypeStruct((b, 1, w), F32)],
        scratch_shapes=[pltpu.VMEM((ROW_PAD + tt, w), F32), pltpu.VMEM((1, w), F32)],
        compiler_params=_params(("parallel", "arbitrary")),
        name="rglru",
    )(a2, buf8, h0, cw, cb, wg, bg, lam)


def _gdn_kernel(qkv_ref, z_ref, gs_ref, buf_ref, s0_ref, cw_ref, alog_ref, dtb_ref, ng_ref,
                out_ref, nbuf_ref, sl_ref, xp_ref, s_ref, *, cps):
    c = CHUNK
    dk = B_HEAD_DIM
    nh = B_HEADS
    hw = nh * dk
    gw = MXU_TILE
    tt = cps * c

    @pl.when(pl.program_id(1) == 0)
    def _():
        xp_ref[0:ROW_PAD, :] = buf_ref[...]
        s_ref[...] = s0_ref[...]

    x_in = qkv_ref[...]
    xp_ref[ROW_PAD:ROW_PAD + tt, :] = x_in
    xc = x_in * cw_ref[3:4, :]
    for j in range(3):
        xc = xc + xp_ref[ROW_PAD - 3 + j:ROW_PAD - 3 + j + tt, :] * cw_ref[j:j + 1, :]
    xc = _silu(xc)
    last = x_in[tt - ROW_PAD:tt, :]
    xp_ref[0:ROW_PAD, :] = last
    nbuf_ref[...] = last

    def unit(x, scale):
        return x * (lax.rsqrt(jnp.sum(x * x, axis=-1, keepdims=True) + 1e-6) * scale)

    q4 = jnp.concatenate([unit(xc[:, h * dk:(h + 1) * dk], dk ** -0.5) for h in range(nh)], axis=1)
    k4 = jnp.concatenate([unit(xc[:, hw + h * dk:hw + (h + 1) * dk], 1.0) for h in range(nh)], axis=1)
    v4 = xc[:, 2 * hw:3 * hw]
    gs = gs_ref[...]
    beta_all = _sigmoid(gs)
    g_all = -jnp.exp(alog_ref[...]) * _softplus(gs + dtb_ref[...])

    def per_head(x, first):
        return jnp.concatenate([jnp.broadcast_to(x[:, first + h:first + h + 1], (x.shape[0], dk))
                                for h in range(nh)], axis=1)

    ri = _iota2((c, c), 0)
    ci = _iota2((c, c), 1)
    tri = (ri >= ci).astype(BF16)
    upper = (ri > ci).astype(F32)
    row_i = _iota2((c, gw), 0)
    lane_j = _iota2((c, gw), 1) & (c - 1)
    lane_head = _blk(_iota2((c, gw), 1), c)
    low_i = row_i >= lane_j
    low_s = row_i > lane_j
    sel = (row_i == lane_j).astype(BF16)
    bd_tok = _bd_mask((gw, gw), c, c)
    bd_kt = _bd_mask((hw, gw), dk, c)
    bd_vn = _bd_mask((B_PAIR * c, gw), c, dk)
    bd_st = _bd_mask((gw, gw), dk, dk)

    pre = []
    for cc in range(cps):
        rs = slice(cc * c, (cc + 1) * c)
        g_c = g_all[rs]
        gmat = _mm_exact_lhs(tri, jnp.concatenate(
            [g_c[:, nh + h:nh + h + 1] * upper for h in range(nh)] + [g_c], axis=1))
        decay = jnp.where(low_i, jnp.exp(gmat[:, 0:gw]), 0.0)
        gc_all = gmat[:, gw:gw + 128]
        gc4 = per_head(gc_all, nh)
        gl4 = per_head(gc_all[c - 1:c, :], nh)
        egc4 = jnp.exp(gc4)
        beta4 = per_head(beta_all[rs], 0)
        kc = k4[rs]
        qc = q4[rs]
        kb = kc * beta4
        kt = jnp.where(bd_kt, _mm(kc, sel, trans_a=True), 0.0)
        gkq = _mm(jnp.concatenate([kb, qc], axis=0), kt)
        pre.append(dict(lmat=jnp.where(low_s, gkq[0:c] * decay, 0.0), qk=gkq[c:2 * c] * decay,
                        vb=v4[rs] * beta4, kbe=kb * egc4, q_in=qc * egc4,
                        k_out=kc * jnp.exp(gl4 - gc4), egl=jnp.exp(gl4)))
    tmats = _neumann_inv_cat([-p["lmat"] for p in pre], bd_tok)
    for p, tmat in zip(pre, tmats):
        us, ws = [], []
        for h in range(nh):
            hs = slice(h * dk, (h + 1) * dk)
            rhs = jnp.concatenate([p["vb"][:, hs], p["kbe"][:, hs]], axis=1)
            uw = _mm(jnp.where(lane_head == h, tmat, 0.0), jnp.concatenate([rhs] * nh, axis=0))
            us.append(uw[:, 0:dk])
            ws.append(uw[:, dk:2 * dk])
        p["u"] = jnp.concatenate(us, axis=1)
        p["w"] = jnp.concatenate(ws, axis=1)

    state = [s_ref[i] for i in range(nh // B_PAIR)]
    for cc, p in enumerate(pre):
        rs = slice(cc * c, (cc + 1) * c)
        outs = []
        for i in range(nh // B_PAIR):
            ps = slice(i * gw, (i + 1) * gw)
            ws = _mm(jnp.concatenate([p["w"][:, ps], p["q_in"][:, ps]], axis=0), state[i])
            v_new = p["u"][:, ps] - ws[0:c]
            qk = p["qk"][:, i * B_PAIR * c:(i + 1) * B_PAIR * c]
            outs.append(ws[c:2 * c] + _mm(qk, _bd_rows(v_new, bd_vn)))
            state[i] = state[i] * p["egl"][:, ps] + jnp.where(
                bd_st, _mm(p["k_out"][:, ps], v_new, trans_a=True), 0.0)
        o = jnp.concatenate(outs, axis=1)
        for h in range(nh):
            hs = slice(h * dk, (h + 1) * dk)
            out_ref[rs, hs] = _rms_norm(o[:, hs], ng_ref[...]) * _silu(z_ref[rs, hs])
    for i in range(nh // B_PAIR):
        s_ref[i] = state[i]
        sl_ref[i] = state[i]


def _gdn(qkv, z, gs, buf8, s0, cw, alog, dtb, ng, cps):
    b, l, _ = qkv.shape
    tt = cps * CHUNK
    w3 = 3 * B_WIDTH
    const = lambda shape: pl.BlockSpec(shape, lambda bi, i: (0,) * len(shape))
    st = (B_HEADS // B_PAIR, MXU_TILE, MXU_TILE)
    return pl.pallas_call(
        functools.partial(_gdn_kernel, cps=cps),
        grid=(b, l // tt),
        in_specs=[pl.BlockSpec((None, tt, w3), lambda bi, i: (bi, i, 0)),
                  pl.BlockSpec((None, tt, B_WIDTH), lambda bi, i: (bi, i, 0)),
                  pl.BlockSpec((None, tt, 128), lambda bi, i: (bi, i, 0)),
                  pl.BlockSpec((None, ROW_PAD, w3), lambda bi, i: (bi, 0, 0)),
                  pl.BlockSpec((None,) + st, lambda bi, i: (bi, 0, 0, 0)),
                  const((4, w3)), const((1, 128)), const((1, 128)), const((1, B_HEAD_DIM))],
        out_specs=[pl.BlockSpec((None, tt, B_WIDTH), lambda bi, i: (bi, i, 0)),
                   pl.BlockSpec((None, ROW_PAD, w3), lambda bi, i: (bi, 0, 0)),
                   pl.BlockSpec((None,) + st, lambda bi, i: (bi, 0, 0, 0))],
        out_shape=[jax.ShapeDtypeStruct((b, l, B_WIDTH), F32),
                   jax.ShapeDtypeStruct((b, ROW_PAD, w3), F32),
                   jax.ShapeDtypeStruct((b,) + st, F32)],
        scratch_shapes=[pltpu.VMEM((ROW_PAD + tt, w3), F32), pltpu.VMEM(st, F32)],
        compiler_params=_params(("parallel", "arbitrary")),
        name="gdn",
    )(qkv, z, gs, buf8, s0, cw, alog, dtb, ng)


def _outproj_kernel(a_ref, b_ref, x_ref, w_ref, g_ref, bb_ref, o_ref):
    half = a_ref.shape[1]
    mix = jnp.dot(a_ref[...].astype(BF16), w_ref[0:half, :], preferred_element_type=F32)
    mix = mix + jnp.dot(b_ref[...].astype(BF16), w_ref[half:2 * half, :], preferred_element_type=F32)
    o_ref[...] = _layer_norm(ALPHA * x_ref[...] + mix, g_ref[...], bb_ref[...])


def _outproj_ln(a, b, x, w, g, bb, tm):
    t, d = x.shape
    half = a.shape[1]
    const = lambda shape: pl.BlockSpec(shape, lambda i: (0,) * len(shape))
    return pl.pallas_call(
        _outproj_kernel,
        grid=(t // tm,),
        in_specs=[pl.BlockSpec((tm, half), lambda i: (i, 0)), pl.BlockSpec((tm, half), lambda i: (i, 0)),
                  pl.BlockSpec((tm, d), lambda i: (i, 0)), const((2 * half, d)), const((1, d)), const((1, d))],
        out_specs=pl.BlockSpec((tm, d), lambda i: (i, 0)),
        out_shape=jax.ShapeDtypeStruct((t, d), F32),
        compiler_params=_params(("parallel",)),
        name="outproj_ln",
    )(a, b, x, w, g, bb)


def _ffn_kernel(x_ref, pe_ref, buf_ref, wup_ref, cw_ref, cb_ref, wdn_ref, g_ref, b_ref,
                wg_ref, bg_ref, wp_ref, o_ref, nbuf_ref, prev_ref, act_ref, *, tt):
    ff = D_FF
    nb = ff // FF_BLOCK

    @pl.when(pl.program_id(1) == 0)
    def _():
        prev_ref[...] = buf_ref[...]

    x = x_ref[...]
    xb = x.astype(BF16)
    row = _iota2((tt, FF_BLOCK), 0)

    def conv(col):
        sl = slice(col, col + FF_BLOCK)
        h = jnp.dot(xb, wup_ref[:, sl], preferred_element_type=F32)
        p1 = prev_ref[ROW_PAD - 1:ROW_PAD, sl]
        p2 = prev_ref[ROW_PAD - 2:ROW_PAD - 1, sl]
        h1 = jnp.where(row == 0, p1, pltpu.roll(h, 1, 0))
        h2 = jnp.where(row == 0, p2, jnp.where(row == 1, p1, pltpu.roll(h, 2, 0)))
        prev_ref[:, sl] = h[tt - ROW_PAD:tt, :]
        return h2 * cw_ref[0:1, sl] + h1 * cw_ref[1:2, sl] + h * cw_ref[2:3, sl] + cb_ref[:, sl]

    for j in range(nb):
        gate = conv(j * FF_BLOCK)
        val = conv(ff + j * FF_BLOCK)
        act_ref[:, j * FF_BLOCK:(j + 1) * FF_BLOCK] = (_silu(gate) * val).astype(BF16)
    nbuf_ref[...] = prev_ref[...]
    f = jnp.dot(act_ref[...], wdn_ref[...], preferred_element_type=F32)
    y = _layer_norm(ALPHA * x + f, g_ref[...], b_ref[...])
    gate = _sigmoid(jnp.dot(y.astype(BF16), wg_ref[...], preferred_element_type=F32) + bg_ref[...])
    emb = jnp.dot(pe_ref[...].astype(BF16), wp_ref[...], preferred_element_type=F32)
    o_ref[...] = y + gate * emb


def _ffn(x, pe, buf8, wup, cw, cb, wdn, g, bb, wg, bg, wp, tt):
    b, l, d = x.shape
    ff2 = 2 * D_FF
    const = lambda shape: pl.BlockSpec(shape, lambda bi, i: (0,) * len(shape),
                                       pipeline_mode=pl.Buffered(1))
    return pl.pallas_call(
        functools.partial(_ffn_kernel, tt=tt),
        grid=(b, l // tt),
        in_specs=[pl.BlockSpec((None, tt, d), lambda bi, i: (bi, i, 0)),
                  pl.BlockSpec((None, tt, PLE_DIM), lambda bi, i: (bi, i, 0)),
                  pl.BlockSpec((None, ROW_PAD, ff2), lambda bi, i: (bi, 0, 0)),
                  const((d, ff2)), const((3, ff2)), const((1, ff2)), const((D_FF, d)),
                  const((1, d)), const((1, d)), const((d, d)), const((1, d)), const((PLE_DIM, d))],
        out_specs=[pl.BlockSpec((None, tt, d), lambda bi, i: (bi, i, 0)),
                   pl.BlockSpec((None, ROW_PAD, ff2), lambda bi, i: (bi, 0, 0))],
        out_shape=[jax.ShapeDtypeStruct((b, l, d), F32),
                   jax.ShapeDtypeStruct((b, ROW_PAD, ff2), F32)],
        scratch_shapes=[pltpu.VMEM((ROW_PAD, ff2), F32), pltpu.VMEM((tt, D_FF), BF16)],
        compiler_params=_params(("parallel", "arbitrary")),
        name="ffn",
    )(x, pe, buf8, wup, cw, cb, wdn, g, bb, wg, bg, wp)


def _rwkv_kernel(p_ref, sh_ref, s0_ref, mu_ref, w0_ref, w2_ref, a0_ref, a2_ref, g2_ref, kk_ref,
                 ka_ref, rk_ref, lg_ref, lb_ref, seg_ref,
                 out_ref, nsh_ref, sl_ref, xp_ref, s_ref, *, cps):
    c = CHUNK
    w = C_WIDTH
    n = C_HEAD_DIM
    gw = MXU_TILE
    ng = w // gw
    tt = cps * c

    @pl.when(pl.program_id(1) == 0)
    def _():
        xp_ref[0:ROW_PAD, :] = sh_ref[...]
        s_ref[...] = s0_ref[...]

    proj = p_ref[...]
    xp_ref[ROW_PAD:ROW_PAD + tt, :] = proj
    prev = xp_ref[ROW_PAD - 1:ROW_PAD - 1 + tt, :]
    last = proj[tt - ROW_PAD:tt, :]
    xp_ref[0:ROW_PAD, :] = last
    nsh_ref[...] = last
    xs = proj + (prev - proj) * mu_ref[...]

    seg = seg_ref[...]

    def head_sum(x):
        return jnp.concatenate([_mm_exact_rhs(x[:, i * gw:(i + 1) * gw], seg) for i in range(ng)], axis=1)

    r = xs[:, 0:w]
    k = xs[:, w:2 * w]
    v = xs[:, 2 * w:3 * w]
    lora_in = xs[:, 3 * w:3 * w + 128]
    ww = -_softplus(-(w0_ref[...] + _mm(jnp.tanh(lora_in), w2_ref[...]))) - 0.5
    a = _sigmoid(a0_ref[...] + _mm(lora_in, a2_ref[...]))
    g = _mm(_sigmoid(xs[:, 3 * w + 128:3 * w + 256]), g2_ref[...])
    kx = k * kk_ref[...]
    k = k * (1.0 + (a - 1.0) * ka_ref[...])
    sums = head_sum(jnp.concatenate([kx * kx, r * k * rk_ref[...]], axis=0))
    kk = kx * lax.rsqrt(sums[0:tt] + 1e-6)
    bonus = sums[tt:2 * tt] * v
    lw = -jnp.exp(ww)
    sa = -kk
    sb = kk * a

    ri = _iota2((c, c), 0)
    ci = _iota2((c, c), 1)
    tri = (ri >= ci).astype(BF16)
    row_i = _iota2((c, gw), 0)
    lane_j = _iota2((c, gw), 1) & (c - 1)
    low_i = row_i >= lane_j
    low_s = row_i > lane_j
    sel = (row_i == lane_j).astype(BF16)
    bd = _bd_mask((gw, gw), n, n)

    units = []
    for cc in range(cps):
        rs = slice(cc * c, (cc + 1) * c)
        lw_c = lw[rs]
        cum = _mm_exact_lhs(tri, lw_c)
        cum_last = cum[c - 1:c, :]
        e_out = jnp.exp(-cum)
        e_end = jnp.exp(cum_last - cum)
        a_t = sa[rs] * jnp.exp(cum - lw_c)
        r_t = r[rs] * jnp.exp(cum)
        b_t = sb[rs] * e_out
        k_t = k[rs] * e_out
        b_o = sb[rs] * e_end
        k_o = k[rs] * e_end
        e_last = jnp.exp(cum_last)
        for i in range(ng):
            gs = slice(i * gw, (i + 1) * gw)
            units.append(dict(cc=cc, i=i, lr=jnp.concatenate([a_t[:, gs], r_t[:, gs]], axis=0),
                              b_t=b_t[:, gs], k_t=k_t[:, gs], v=v[rs, gs],
                              bk=jnp.concatenate([b_o[:, gs], k_o[:, gs]], axis=0), e_last=e_last[:, gs]))
    for un in units:
        un["bt"] = jnp.where(bd, _mm(un["b_t"], sel, trans_a=True), 0.0)
        un["kt"] = jnp.where(bd, _mm(un["k_t"], sel, trans_a=True), 0.0)
    for un in units:
        g1 = _mm(un["lr"], un["bt"])
        g2 = _mm(un["lr"], un["kt"])
        un["m_ab"] = jnp.where(low_s, g1[0:c], 0.0)
        un["m_ak"] = jnp.where(low_s, g2[0:c], 0.0)
        un["m_r"] = jnp.concatenate([jnp.where(low_i, g1[c:2 * c], 0.0),
                                     jnp.where(low_i, g2[c:2 * c], 0.0)], axis=1)
        un["bdv"] = _bd_rows(un["v"], bd)
    for un, tmat in zip(units, _neumann_inv_cat([un["m_ab"] for un in units], bd)):
        un["tmat"] = tmat
        un["akv"] = _mm(un["m_ak"], un["bdv"])

    state = [s_ref[i] for i in range(ng)]
    outs = [[None] * ng for _ in range(cps)]
    for un in units:
        i = un["i"]
        ls = _mm(un["lr"], state[i], trans_b=True)
        u = _mm(un["tmat"], _bd_rows(ls[0:c] + un["akv"], bd))
        outs[un["cc"]][i] = ls[c:2 * c] + _mm(
            un["m_r"], jnp.concatenate([_bd_rows(u, bd), un["bdv"]], axis=0))
        uv = jnp.concatenate([u, un["v"]], axis=0)
        state[i] = state[i] * un["e_last"] + jnp.where(bd, _mm(uv, un["bk"], trans_a=True), 0.0)
    for i in range(ng):
        s_ref[i] = state[i]
        sl_ref[i] = state[i]

    o = jnp.concatenate([jnp.concatenate(row, axis=1) for row in outs], axis=0)
    inv_n = 1.0 / n
    oc = o - head_sum(o) * inv_n
    var = head_sum(oc * oc) * inv_n
    on = oc * lax.rsqrt(var + C_LNX_EPS) * lg_ref[...] + lb_ref[...]
    out_ref[...] = (on + bonus) * g


def _rwkv(proj, sh8, s0, mu, w0, w2p, a0, a2p, g2, kk, ka, rk, lg, lb, seg, cps):
    b, l, cp = proj.shape
    tt = cps * CHUNK
    w = C_WIDTH
    const = lambda shape: pl.BlockSpec(shape, lambda bi, i: (0,) * len(shape))
    st = (C_HEADS // C_GROUP, MXU_TILE, MXU_TILE)
    return pl.pallas_call(
        functools.partial(_rwkv_kernel, cps=cps),
        grid=(b, l // tt),
        in_specs=[pl.BlockSpec((None, tt, cp), lambda bi, i: (bi, i, 0)),
                  pl.BlockSpec((None, ROW_PAD, cp), lambda bi, i: (bi, 0, 0)),
                  pl.BlockSpec((None,) + st, lambda bi, i: (bi, 0, 0, 0)),
                  const((1, cp)), const((1, w)), const((128, w)), const((1, w)), const((128, w)),
                  const((128, w)), const((1, w)), const((1, w)), const((1, w)), const((1, w)),
                  const((1, w)), const((MXU_TILE, MXU_TILE))],
        out_specs=[pl.BlockSpec((None, tt, w), lambda bi, i: (bi, i, 0)),
                   pl.BlockSpec((None, ROW_PAD, cp), lambda bi, i: (bi, 0, 0)),
                   pl.BlockSpec((None,) + st, lambda bi, i: (bi, 0, 0, 0))],
        out_shape=[jax.ShapeDtypeStruct((b, l, w), F32),
                   jax.ShapeDtypeStruct((b, ROW_PAD, cp), F32),
                   jax.ShapeDtypeStruct((b,) + st, F32)],
        scratch_shapes=[pltpu.VMEM((ROW_PAD + tt, cp), F32), pltpu.VMEM(st, F32)],
        compiler_params=_params(("parallel", "arbitrary")),
        name="rwkv7",
    )(proj, sh8, s0, mu, w0, w2p, a0, a2p, g2, kk, ka, rk, lg, lb, seg)


def _mla_prep_kernel(d_ref, ct_ref, st_ref, kc_ref, qg_ref, wq_ref, wqs_ref, kg_ref,
                     q_ref, c_ref, kr_ref):
    lq = D_Q_LORA
    lkv = D_KV_LORA
    qn = _rms_norm(d_ref[:, 0:lq], qg_ref[...]).astype(BF16)
    ct = ct_ref[...]
    st = st_ref[...]
    for h in range(D_HEADS):
        sl = slice(h * D_QK_PAD, (h + 1) * D_QK_PAD)
        q = jnp.dot(qn, wq_ref[:, sl], preferred_element_type=F32) * ct
        q = q + jnp.dot(qn, wqs_ref[:, sl], preferred_element_type=F32) * st
        q_ref[:, sl] = (q * MLA_SCALE).astype(BF16)
    c_ref[...] = _rms_norm(d_ref[:, lq:lq + lkv], kg_ref[...])
    o = lq + lkv
    kr_ref[...] = d_ref[:, o:o + D_QK_PAD] * kc_ref[...] + d_ref[:, o + D_QK_PAD:o + 2 * D_QK_PAD] * st


def _mla_prep(d, ctab, stab, kctab, qg, wq, wqs, kg, tm):
    b, l, dw = d.shape
    hq = D_HEADS * D_QK_PAD
    const = lambda shape: pl.BlockSpec(shape, lambda bi, i: (0,) * len(shape))
    tab = pl.BlockSpec((tm, D_QK_PAD), lambda bi, i: (i, 0))
    return pl.pallas_call(
        _mla_prep_kernel,
        grid=(b, l // tm),
        in_specs=[pl.BlockSpec((None, tm, dw), lambda bi, i: (bi, i, 0)), tab, tab, tab,
                  const((1, D_Q_LORA)), const((D_Q_LORA, hq)), const((D_Q_LORA, hq)),
                  const((1, D_KV_LORA))],
        out_specs=[pl.BlockSpec((None, tm, hq), lambda bi, i: (bi, i, 0)),
                   pl.BlockSpec((None, tm, D_KV_LORA), lambda bi, i: (bi, i, 0)),
                   pl.BlockSpec((None, tm, D_QK_PAD), lambda bi, i: (bi, i, 0))],
        out_shape=[jax.ShapeDtypeStruct((b, l, hq), BF16),
                   jax.ShapeDtypeStruct((b, l, D_KV_LORA), F32),
                   jax.ShapeDtypeStruct((b, l, D_QK_PAD), F32)],
        compiler_params=_params(("parallel", "parallel")),
        name="mla_prep",
    )(d, ctab, stab, kctab, qg, wq, wqs, kg)


def _kv_kernel(c_ref, kr_ref, wk_ref, wv_ref, k_ref, v_ref):
    cb = c_ref[...].astype(BF16)
    kr = kr_ref[...]
    ones = jnp.ones((cb.shape[0], D_V_PAD - D_V), BF16)
    for h in range(D_HEADS):
        sl = slice(h * D_QK_PAD, (h + 1) * D_QK_PAD)
        k_ref[:, sl] = (jnp.dot(cb, wk_ref[:, sl], preferred_element_type=F32) + kr).astype(BF16)
        vh = jnp.dot(cb, wv_ref[:, h * D_V:(h + 1) * D_V], preferred_element_type=F32).astype(BF16)
        v_ref[:, h * D_V_PAD:h * D_V_PAD + D_V] = vh
        v_ref[:, h * D_V_PAD + D_V:(h + 1) * D_V_PAD] = ones


def _kv(c, krp, wk, wv, tm):
    t = c.shape[0]
    hk = D_HEADS * D_QK_PAD
    hv = D_HEADS * D_V_PAD
    const = lambda shape: pl.BlockSpec(shape, lambda i: (0,) * len(shape))
    return pl.pallas_call(
        _kv_kernel,
        grid=(t // tm,),
        in_specs=[pl.BlockSpec((tm, D_KV_LORA), lambda i: (i, 0)),
                  pl.BlockSpec((tm, D_QK_PAD), lambda i: (i, 0)),
                  const((D_KV_LORA, hk)), const((D_KV_LORA, D_HEADS * D_V))],
        out_specs=[pl.BlockSpec((tm, hk), lambda i: (i, 0)), pl.BlockSpec((tm, hv), lambda i: (i, 0))],
        out_shape=[jax.ShapeDtypeStruct((t, hk), BF16), jax.ShapeDtypeStruct((t, hv), BF16)],
        compiler_params=_params(("parallel",)),
        name="mla_kv",
    )(c, krp, wk, wv)


def _attn_kernel(q_ref, k_ref, v_ref, o_ref, m_ref, acc_ref, *, tq, tk, nk, q_off):
    q_first = q_off + pl.program_id(2) * tq
    first_chunk_end = q_first // CHUNK * CHUNK + CHUNK
    last_chunk_end = (q_first + tq - 1) // CHUNK * CHUNK + CHUNK
    n_full = jnp.minimum(nk, first_chunk_end // tk)
    n_live = jnp.minimum(nk, (last_chunk_end + tk - 1) // tk)
    q = q_ref[...]
    m_ref[...] = jnp.full(m_ref.shape, -jnp.inf, F32)
    acc_ref[...] = jnp.zeros(acc_ref.shape, F32)

    def step(j, masked):
        start = pl.multiple_of(j * tk, tk)
        s = lax.dot_general(q, k_ref[pl.ds(start, tk), :], _dims(False, True), preferred_element_type=F32)
        if masked:
            q_chunk = jnp.right_shift(q_first + _iota2((tq, tk), 0), CHUNK_SHIFT)
            k_chunk = jnp.right_shift(start + _iota2((tq, tk), 1), CHUNK_SHIFT)
            s = jnp.where(k_chunk <= q_chunk, s, -jnp.inf)
        m_old = m_ref[...]
        m_new = jnp.maximum(m_old, jnp.max(s, axis=-1, keepdims=True))
        p = jnp.exp(s - m_new).astype(BF16)
        acc_ref[...] = jnp.exp(m_old - m_new) * acc_ref[...] + jnp.dot(
            p, v_ref[pl.ds(start, tk), :], preferred_element_type=F32)
        m_ref[...] = m_new

    def full_body(j, carry):
        step(j, False)
        return carry

    def edge_body(j, carry):
        step(j, True)
        return carry

    lax.fori_loop(0, n_full, full_body, 0)
    lax.fori_loop(n_full, n_live, edge_body, 0)
    acc = acc_ref[...]
    o_ref[...] = acc[:, 0:D_V] / acc[:, D_V:D_V + 1]


def _attention(q, k, v, tq, tk, q_off):
    b, l, _ = q.shape
    lk = k.shape[1]
    return pl.pallas_call(
        functools.partial(_attn_kernel, tq=tq, tk=tk, nk=lk // tk, q_off=q_off),
        grid=(b, D_HEADS, l // tq),
        in_specs=[pl.BlockSpec((None, tq, D_QK_PAD), lambda bi, h, i: (bi, i, h)),
                  pl.BlockSpec((None, lk, D_QK_PAD), lambda bi, h, i: (bi, 0, h)),
                  pl.BlockSpec((None, lk, D_V_PAD), lambda bi, h, i: (bi, 0, h))],
        out_specs=pl.BlockSpec((None, tq, D_V), lambda bi, h, i: (bi, i, h)),
        out_shape=jax.ShapeDtypeStruct((b, l, D_HEADS * D_V), F32),
        scratch_shapes=[pltpu.VMEM((tq, 1), F32), pltpu.VMEM((tq, D_V_PAD), F32)],
        compiler_params=_params(("parallel", "parallel", "arbitrary")),
        name="mla_attention",
    )(q, k, v)


def _pad_rows_front(buf):
    return jnp.pad(buf, ((0, 0), (ROW_PAD - buf.shape[1], 0), (0, 0)))


def _block_diag(w):
    nb, bi, bj = w.shape
    eye = jnp.eye(nb, dtype=w.dtype)
    return (eye[:, None, :, None] * w[:, :, None, :]).reshape(nb * bi, nb * bj)


def _state_to_block_diag(s, per):
    b, h, n, m = s.shape
    eye = jnp.eye(per, dtype=s.dtype)
    s6 = s.reshape(b, h // per, per, n, 1, m) * eye[None, None, :, None, :, None]
    return s6.reshape(b, h // per, per * n, per * m)


def _state_from_block_diag(sbd, per):
    b, g, pn, pm = sbd.shape
    n, m = pn // per, pm // per
    s6 = sbd.reshape(b, g, per, n, per, m)
    return jnp.stack([s6[:, :, p, :, p, :] for p in range(per)], axis=2).reshape(b, g * per, n, m)


def _rope_swap(w):
    half = w.shape[-1] // 2
    return jnp.concatenate([-w[..., half:], w[..., :half]], axis=-1)


def _prepare(w_in0, a_w_r, a_w_i, a_b_r, a_b_i, w_in1, c_w2, c_a2, d_w_qb, d_w_kvb):
    p = {}
    o = 0
    p["w_a"] = w_in0[:, o:o + 2 * A_WIDTH].astype(BF16); o += 2 * A_WIDTH
    p["w_qkv"] = w_in0[:, o:o + 3 * B_WIDTH].astype(BF16); o += 3 * B_WIDTH
    p["w_z"] = w_in0[:, o:o + B_WIDTH].astype(BF16); o += B_WIDTH
    p["w_gs"] = jnp.pad(w_in0[:, o:o + 2 * B_HEADS], ((0, 0), (0, 128 - 2 * B_HEADS))).astype(BF16)
    p["a_wg"] = jnp.concatenate([_block_diag(a_w_r), _block_diag(a_w_i)], axis=1).astype(BF16)
    p["a_bg"] = jnp.concatenate([a_b_r, a_b_i])[None, :]
    p["w_c"] = w_in1[:, 0:C_PROJ].astype(BF16)
    o = C_PROJ
    w_qa = w_in1[:, o:o + D_Q_LORA]; o += D_Q_LORA
    w_craw = w_in1[:, o:o + D_KV_LORA]; o += D_KV_LORA
    w_kr = w_in1[:, o:o + D_ROPE]
    place = lambda w: jnp.pad(w, ((0, 0), (D_NOPE, D_QK_PAD - D_NOPE - D_ROPE)))
    p["w_d"] = jnp.concatenate([w_qa, w_craw, place(w_kr), place(_rope_swap(w_kr))], axis=1).astype(BF16)
    p["c_w2p"] = jnp.pad(c_w2, ((0, 64), (0, 0)))
    p["c_a2p"] = jnp.pad(c_a2, ((64, 0), (0, 0)))
    wq = d_w_qb.reshape(D_Q_LORA, D_HEADS, D_NOPE + D_ROPE)
    zpad = jnp.zeros((D_Q_LORA, D_HEADS, D_QK_PAD - D_NOPE - D_ROPE), F32)
    p["wq"] = jnp.concatenate([wq, zpad], axis=-1).reshape(D_Q_LORA, D_HEADS * D_QK_PAD).astype(BF16)
    wq_sw = jnp.concatenate([jnp.zeros((D_Q_LORA, D_HEADS, D_NOPE), F32), _rope_swap(wq[..., D_NOPE:]), zpad], axis=-1)
    p["wqs"] = wq_sw.reshape(D_Q_LORA, D_HEADS * D_QK_PAD).astype(BF16)
    wkv = d_w_kvb.reshape(D_KV_LORA, D_HEADS, D_NOPE + D_V)
    wk = jnp.pad(wkv[..., :D_NOPE], ((0, 0), (0, 0), (0, D_QK_PAD - D_NOPE)))
    p["wk"] = wk.reshape(D_KV_LORA, D_HEADS * D_QK_PAD).astype(BF16)
    p["wv"] = wkv[..., D_NOPE:].reshape(D_KV_LORA, D_HEADS * D_V).astype(BF16)
    lane = jnp.arange(MXU_TILE) // C_HEAD_DIM
    p["seg"] = (lane[:, None] == lane[None, :]).astype(BF16)
    return p


def _rope_tables(pos):
    half = D_ROPE // 2
    inv = ROPE_THETA ** (-jnp.arange(half, dtype=F32) / half)
    ang = pos.astype(F32)[:, None] * inv[None, :]
    cos = jnp.cos(ang)
    sin = jnp.sin(ang)
    n = pos.shape[0]
    tail = jnp.zeros((n, D_QK_PAD - D_NOPE - D_ROPE), F32)
    cos2 = jnp.concatenate([cos, cos], axis=1)
    sin2 = jnp.concatenate([sin, sin], axis=1)
    q_cos = jnp.concatenate([jnp.ones((n, D_NOPE), F32), cos2, tail], axis=1)
    q_sin = jnp.concatenate([jnp.zeros((n, D_NOPE), F32), sin2, tail], axis=1)
    k_cos = jnp.concatenate([jnp.zeros((n, D_NOPE), F32), cos2, tail], axis=1)
    return q_cos, q_sin, k_cos


def _trunk(x, pe, pos, a_conv, a_h, b_conv, b_s, c_shift, c_s, d_ckv, d_krope, f_conv, wts, prm):
    (a_conv_w, a_conv_b, a_lambda, b_conv_w, b_a_log, b_dt_bias, b_norm_g, w_out0, c_mu, c_w0,
     c_a0, c_g2, c_k_k, c_k_a, c_r_k, c_lnx_g, c_lnx_b, d_qa_g, d_kva_g, w_out1, ln_mix_g, ln_mix_b,
     ffn_w_up, ffn_conv_w, ffn_conv_b, ffn_w_down, ln_ffn_g, ln_ffn_b, ple_w_gate, ple_b_gate,
     ple_w_proj) = wts
    b, l, d = x.shape
    t = b * l
    tm = min(512, t)
    tt = min(512, l)
    cps = 2 if l % (2 * CHUNK) == 0 else 1
    row = lambda vec: vec.reshape(1, -1)
    f_new = []

    def ffn_block(i, xin):
        y, nbuf = _ffn(xin.reshape(b, l, d), pe[i], _pad_rows_front(f_conv[i]),
                       ffn_w_up[i].astype(BF16), ffn_conv_w[i], row(ffn_conv_b[i]),
                       ffn_w_down[i].astype(BF16), row(ln_ffn_g[i]), row(ln_ffn_b[i]),
                       ple_w_gate[i].astype(BF16), row(ple_b_gate[i]), ple_w_proj[i].astype(BF16), tt)
        f_new.append(nbuf[:, ROW_PAD - 2:, :])
        return y.reshape(t, d)

    x2 = x.reshape(t, d)
    a2, qkv, z, gs = _proj(x2, [prm["w_a"], prm["w_qkv"], prm["w_z"], prm["w_gs"]], tm)
    a_out, a_buf, a_hl = _rglru(a2.reshape(b, l, -1), _pad_rows_front(a_conv), a_h[:, None, :],
                                a_conv_w, row(a_conv_b), prm["a_wg"], prm["a_bg"], row(a_lambda),
                                min(256, l))
    pad8 = lambda vec: jnp.pad(vec, (B_HEADS, 128 - 2 * B_HEADS))[None, :]
    b_out, b_buf, b_sl = _gdn(qkv.reshape(b, l, -1), z.reshape(b, l, -1), gs.reshape(b, l, -1),
                              _pad_rows_front(b_conv), _state_to_block_diag(b_s, B_PAIR), b_conv_w,
                              pad8(b_a_log), pad8(b_dt_bias), row(b_norm_g), cps)
    x2 = _outproj_ln(a_out.reshape(t, -1), b_out.reshape(t, -1), x2, w_out0.astype(BF16),
                     row(ln_mix_g[0]), row(ln_mix_b[0]), tm)
    x2 = ffn_block(0, x2)

    c_in, d_in = _proj(x2, [prm["w_c"], prm["w_d"]], tm)
    c_out, c_sh, c_sl = _rwkv(c_in.reshape(b, l, -1), _pad_rows_front(c_shift),
                              _state_to_block_diag(c_s, C_GROUP), row(c_mu), row(c_w0),
                              prm["c_w2p"], row(c_a0), prm["c_a2p"], c_g2, row(c_k_k), row(c_k_a),
                              row(c_r_k.reshape(-1)), row(c_lnx_g), row(c_lnx_b), prm["seg"], cps)
    q_cos, q_sin, k_cos = _rope_tables(pos)
    q, c_new, kr_pad = _mla_prep(d_in.reshape(b, l, -1), q_cos, q_sin, k_cos, row(d_qa_g), prm["wq"],
                                 prm["wqs"], row(d_kva_g), min(512, l))
    past = d_ckv.shape[1]
    if past:
        c_all = jnp.concatenate([d_ckv, c_new], axis=1)
        kr_old = jnp.pad(d_krope, ((0, 0), (0, 0), (D_NOPE, D_QK_PAD - D_NOPE - D_ROPE)))
        kr_all = jnp.concatenate([kr_old, kr_pad], axis=1)
    else:
        c_all, kr_all = c_new, kr_pad
    lk = past + l
    tkv = b * lk
    k, v = _kv(c_all.reshape(tkv, -1), kr_all.reshape(tkv, -1), prm["wk"], prm["wv"],
               512 if tkv % 512 == 0 else lk)
    tq = min(512, l)
    tk = 512 if lk % 512 == 0 else lk
    d_out = _attention(q, k.reshape(b, lk, -1), v.reshape(b, lk, -1), tq, tk, past)
    x2 = _outproj_ln(c_out.reshape(t, -1), d_out.reshape(t, -1), x2, w_out1.astype(BF16),
                     row(ln_mix_g[1]), row(ln_mix_b[1]), tm)
    x2 = ffn_block(1, x2)

    return (x2.reshape(b, l, d), a_buf[:, ROW_PAD - 3:, :], a_hl[:, 0, :], b_buf[:, ROW_PAD - 3:, :],
            _state_from_block_diag(b_sl, B_PAIR), c_sh[:, ROW_PAD - 1:, :],
            _state_from_block_diag(c_sl, C_GROUP), c_new, kr_pad[:, :, D_NOPE:D_NOPE + D_ROPE],
            jnp.stack(f_new))


def kernel(x_prompt, x_sample, state_a_conv, state_a_h, state_b_conv, state_b_s, state_c_shift, state_c_s, cache_d_ckv, cache_d_krope, state_ffn_conv, p_prompt, p_sample, w_in0, a_conv_w, a_conv_b, a_w_r, a_b_r, a_w_i, a_b_i, a_lambda, b_conv_w, b_a_log, b_dt_bias, b_norm_g, w_out0, w_in1, c_mu, c_w0, c_w2, c_a0, c_a2, c_g2, c_k_k, c_k_a, c_r_k, c_lnx_g, c_lnx_b, d_qa_g, d_w_qb, d_kva_g, d_w_kvb, w_out1, ln_mix_g, ln_mix_b, ffn_w_up, ffn_conv_w, ffn_conv_b, ffn_w_down, ln_ffn_g, ln_ffn_b, ple_w_gate, ple_b_gate, ple_w_proj):
    prm = _prepare(w_in0, a_w_r, a_w_i, a_b_r, a_b_i, w_in1, c_w2, c_a2, d_w_qb, d_w_kvb)
    wts = (a_conv_w, a_conv_b, a_lambda, b_conv_w, b_a_log, b_dt_bias, b_norm_g, w_out0, c_mu, c_w0,
           c_a0, c_g2, c_k_k, c_k_a, c_r_k, c_lnx_g, c_lnx_b, d_qa_g, d_kva_g, w_out1, ln_mix_g, ln_mix_b,
           ffn_w_up, ffn_conv_w, ffn_conv_b, ffn_w_down, ln_ffn_g, ln_ffn_b, ple_w_gate, ple_b_gate,
           ple_w_proj)
    dt = x_prompt.dtype
    bp, lp = x_prompt.shape[0], x_prompt.shape[1]
    zeros = lambda *shape: jnp.zeros(shape, dt)
    prompt = _trunk(x_prompt, p_prompt, jnp.arange(lp),
                    zeros(bp, 3, A_WIDTH), zeros(bp, A_WIDTH), zeros(bp, 3, 3 * B_WIDTH),
                    zeros(bp, B_HEADS, B_HEAD_DIM, B_HEAD_DIM), zeros(bp, 1, C_PROJ),
                    zeros(bp, C_HEADS, C_HEAD_DIM, C_HEAD_DIM), zeros(bp, 0, D_KV_LORA),
                    zeros(bp, 0, D_ROPE), zeros(DEPTH, bp, 2, 2 * D_FF), wts, prm)
    ls = x_sample.shape[1]
    past = cache_d_ckv.shape[1]
    sample = _trunk(x_sample, p_sample, past + jnp.arange(ls), state_a_conv, state_a_h, state_b_conv,
                    state_b_s, state_c_shift, state_c_s, cache_d_ckv, cache_d_krope, state_ffn_conv,
                    wts, prm)
    out = [prompt[0], sample[0]]
    for ps, ss in zip(prompt[1:], sample[1:]):
        out += [ps, ss]
    return tuple(out)
```

```python
import functools

import jax
import jax.numpy as jnp
from jax import lax
from jax.experimental import pallas as pl
from jax.experimental.pallas import tpu as pltpu

F32 = jnp.float32
BF16 = jnp.bfloat16

D_MODEL = 1024
DEPTH = 2
CHUNK = 64
CHUNK_SHIFT = CHUNK.bit_length() - 1
PLE_DIM = 256
ALPHA = (2 * DEPTH) ** 0.25
NORM_EPS = 1e-5
A_WIDTH = 512
A_BLOCKS = 8
A_C = 8.0
B_HEADS = 4
B_HEAD_DIM = 128
B_WIDTH = 512
C_HEAD_DIM = 64
C_WIDTH = 512
C_HEADS = 8
C_LNX_EPS = 64e-5
C_PROJ = 3 * C_WIDTH + 64 + 64 + 128
D_HEADS = 4
D_Q_LORA = 384
D_KV_LORA = 256
D_NOPE = 128
D_ROPE = 64
D_V = 128
D_QK_PAD = 256
D_V_PAD = 256
MLA_SCALE = (D_NOPE + D_ROPE) ** -0.5
ROPE_THETA = 10000.0
D_FF = 2816
FF_BLOCK = 256
MXU_TILE = 256
ROW_PAD = 8
VMEM_LIMIT = 56 * 1024 * 1024
ATTN_HEADS = 2
ATTN_TQ = 256
ATTN_TK = 2048
B_PAIR = MXU_TILE // B_HEAD_DIM
C_GROUP = MXU_TILE // C_HEAD_DIM


def _params(sem):
    return pltpu.CompilerParams(dimension_semantics=sem, vmem_limit_bytes=VMEM_LIMIT)


def _sigmoid(x):
    return 1.0 / (1.0 + jnp.exp(-x))


def _softplus(x):
    return jnp.maximum(x, 0.0) + jnp.log(1.0 + jnp.exp(-jnp.abs(x)))


def _silu(x):
    return x * _sigmoid(x)


def _gelu_tanh(x):
    return 0.5 * x * (1.0 + jnp.tanh(0.7978845608028654 * (x + 0.044715 * x * x * x)))


def _dims(trans_a, trans_b):
    return (((0 if trans_a else 1,), (1 if trans_b else 0,)), ((), ()))


def _mm(a, b, trans_a=False, trans_b=False):
    return lax.dot_general(a.astype(BF16), b.astype(BF16), _dims(trans_a, trans_b),
                           preferred_element_type=F32)


def _split3(a):
    a1 = a.astype(BF16)
    r1 = a - a1.astype(F32)
    a2 = r1.astype(BF16)
    return a1, a2, (r1 - a2.astype(F32)).astype(BF16)


def _mm_exact_rhs(a, mask):
    a1, a2, a3 = _split3(a)
    mb = mask.astype(BF16)
    out = jnp.dot(a3, mb, preferred_element_type=F32)
    out = out + jnp.dot(a2, mb, preferred_element_type=F32)
    return out + jnp.dot(a1, mb, preferred_element_type=F32)


def _mm_exact_lhs(mask, b):
    b1, b2, b3 = _split3(b)
    mb = mask.astype(BF16)
    out = jnp.dot(mb, b3, preferred_element_type=F32)
    out = out + jnp.dot(mb, b2, preferred_element_type=F32)
    return out + jnp.dot(mb, b1, preferred_element_type=F32)


def _iota2(shape, axis):
    return lax.broadcasted_iota(jnp.int32, shape, axis)


def _blk(idx, size):
    return jnp.right_shift(idx, size.bit_length() - 1)


def _bd_mask(shape, row_block, col_block):
    return _blk(_iota2(shape, 0), row_block) == _blk(_iota2(shape, 1), col_block)


def _bd_rows(x, mask):
    reps = mask.shape[0] // x.shape[0]
    return jnp.where(mask, jnp.concatenate([x] * reps, axis=0), 0.0)


def _split_f32(a):
    hi = a.astype(BF16).astype(F32)
    return hi, a - hi


def _neumann_inv_cat(xs, bd):
    c, wd = xs[0].shape
    eye = (_iota2((c, wd), 0) == (_iota2((c, wd), 1) & (c - 1))).astype(F32)

    def operands(a):
        hi, lo = _split_f32(a)
        return hi.astype(BF16), lo.astype(BF16), _bd_rows(hi, bd).astype(BF16), _bd_rows(lo, bd).astype(BF16)

    def prod(lh, ll, bh, bl):
        out = jnp.dot(lh, bl, preferred_element_type=F32)
        out = out + jnp.dot(ll, bh, preferred_element_type=F32)
        return out + jnp.dot(lh, bh, preferred_element_type=F32)

    ts = [eye + x for x in xs]
    ps = [operands(x) for x in xs]
    k = 2
    while k < c:
        ps = [operands(prod(*p)) for p in ps]
        nxt = []
        for t, p in zip(ts, ps):
            hi, lo = _split_f32(t)
            nxt.append(t + prod(hi.astype(BF16), lo.astype(BF16), p[2], p[3]))
        ts = nxt
        k *= 2
    return ts


def _layer_norm(x, g, b):
    mu = jnp.mean(x, axis=-1, keepdims=True)
    xc = x - mu
    var = jnp.mean(xc * xc, axis=-1, keepdims=True)
    return xc * lax.rsqrt(var + NORM_EPS) * g + b


def _rms_norm(x, g, eps=1e-6):
    return x * lax.rsqrt(jnp.mean(x * x, axis=-1, keepdims=True) + eps) * g


def _proj_kernel(x_ref, *refs):
    n = len(refs) // 2
    xb = x_ref[...].astype(BF16)
    for w_ref, o_ref in zip(refs[:n], refs[n:]):
        o_ref[...] = jnp.dot(xb, w_ref[...], preferred_element_type=F32)


def _proj(x, ws, tm):
    t, k = x.shape
    return pl.pallas_call(
        _proj_kernel,
        grid=(t // tm,),
        in_specs=[pl.BlockSpec((tm, k), lambda i: (i, 0))]
        + [pl.BlockSpec(w.shape, lambda i: (0, 0)) for w in ws],
        out_specs=[pl.BlockSpec((tm, w.shape[1]), lambda i: (i, 0)) for w in ws],
        out_shape=[jax.ShapeDtypeStruct((t, w.shape[1]), F32) for w in ws],
        compiler_params=_params(("parallel",)),
        name="proj",
    )(x, *ws)


def _scan_rows(a, b):
    n = a.shape[0]
    row = _iota2(a.shape, 0)
    s = 1
    while s < n:
        a_sh = pltpu.roll(a, s, 0)
        b_sh = pltpu.roll(b, s, 0)
        m = row >= s
        b = jnp.where(m, a * b_sh + b, b)
        a = jnp.where(m, a * a_sh, a)
        s *= 2
    return a, b


def _rglru_kernel(a2_ref, buf_ref, h0_ref, cw_ref, cb_ref, wg_ref, bg_ref, lam_ref,
                  out_ref, nbuf_ref, hl_ref, xp_ref, hc_ref, *, tt):
    w = A_WIDTH

    @pl.when(pl.program_id(1) == 0)
    def _():
        xp_ref[0:ROW_PAD, :] = buf_ref[...]
        hc_ref[...] = h0_ref[...]

    gate_in = a2_ref[:, 0:w]
    x_in = a2_ref[:, w:2 * w]
    xp_ref[ROW_PAD:ROW_PAD + tt, :] = x_in
    xc = x_in * cw_ref[3:4, :] + cb_ref[...]
    for j in range(3):
        xc = xc + xp_ref[ROW_PAD - 3 + j:ROW_PAD - 3 + j + tt, :] * cw_ref[j:j + 1, :]
    gates = jnp.dot(xc.astype(BF16), wg_ref[...], preferred_element_type=F32) + bg_ref[...]
    r = _sigmoid(gates[:, 0:w])
    ig = _sigmoid(gates[:, w:2 * w])
    log_a = (-A_C) * r * _softplus(-lam_ref[...])
    a = jnp.exp(log_a)
    u = jnp.sqrt(1.0 - a * a) * (ig * xc)
    a_cum, h_loc = _scan_rows(a, u)
    h = a_cum * hc_ref[...] + h_loc
    out_ref[...] = h * _gelu_tanh(gate_in)
    hc_ref[...] = h[tt - 1:tt, :]
    hl_ref[...] = h[tt - 1:tt, :]
    last = x_in[tt - ROW_PAD:tt, :]
    xp_ref[0:ROW_PAD, :] = last
    nbuf_ref[...] = last


def _rglru(a2, buf8, h0, cw, cb, wg, bg, lam, tt):
    b, l, _ = a2.shape
    w = A_WIDTH
    const = lambda shape: pl.BlockSpec(shape, lambda bi, i: (0,) * len(shape))
    return pl.pallas_call(
        functools.partial(_rglru_kernel, tt=tt),
        grid=(b, l // tt),
        in_specs=[pl.BlockSpec((None, tt, 2 * w), lambda bi, i: (bi, i, 0)),
                  pl.BlockSpec((None, ROW_PAD, w), lambda bi, i: (bi, 0, 0)),
                  pl.BlockSpec((None, 1, w), lambda bi, i: (bi, 0, 0)),
                  const((4, w)), const((1, w)), const((w, 2 * w)), const((1, 2 * w)), const((1, w))],
        out_specs=[pl.BlockSpec((None, tt, w), lambda bi, i: (bi, i, 0)),
                   pl.BlockSpec((None, ROW_PAD, w), lambda bi, i: (bi, 0, 0)),
                   pl.BlockSpec((None, 1, w), lambda bi, i: (bi, 0, 0))],
        out_shape=[jax.ShapeDtypeStruct((b, l, w), F32),
                   jax.ShapeDtypeStruct((b, ROW_PAD, w), F32),
                   jax.ShapeDtypeStruct((b, 1, w), F32)],
        scratch_shapes=[pltpu.VMEM((ROW_PAD + tt, w), F32), pltpu.VMEM((1, w), F32)],
        compiler_params=_params(("parallel", "arbitrary")),
        name="rglru",
    )(a2, buf8, h0, cw, cb, wg, bg, lam)


def _gdn_kernel(qkv_ref, z_ref, gs_ref, buf_ref, s0_ref, cw_ref, alog_ref, dtb_ref, ng_ref,
                out_ref, nbuf_ref, sl_ref, xp_ref, s_ref, *, cps):
    c = CHUNK
    dk = B_HEAD_DIM
    nh = B_HEADS
    hw = nh * dk
    gw = MXU_TILE
    tt = cps * c

    @pl.when(pl.program_id(1) == 0)
    def _():
        xp_ref[0:ROW_PAD, :] = buf_ref[...]
        s_ref[...] = s0_ref[...]

    x_in = qkv_ref[...]
    xp_ref[ROW_PAD:ROW_PAD + tt, :] = x_in
    xc = x_in * cw_ref[3:4, :]
    for j in range(3):
        xc = xc + xp_ref[ROW_PAD - 3 + j:ROW_PAD - 3 + j + tt, :] * cw_ref[j:j + 1, :]
    xc = _silu(xc)
    last = x_in[tt - ROW_PAD:tt, :]
    xp_ref[0:ROW_PAD, :] = last
    nbuf_ref[...] = last

    def unit(x, scale):
        return x * (lax.rsqrt(jnp.sum(x * x, axis=-1, keepdims=True) + 1e-6) * scale)

    q4 = jnp.concatenate([unit(xc[:, h * dk:(h + 1) * dk], dk ** -0.5) for h in range(nh)], axis=1)
    k4 = jnp.concatenate([unit(xc[:, hw + h * dk:hw + (h + 1) * dk], 1.0) for h in range(nh)], axis=1)
    v4 = xc[:, 2 * hw:3 * hw]
    gs = gs_ref[...]
    beta_all = _sigmoid(gs)
    g_all = -jnp.exp(alog_ref[...]) * _softplus(gs + dtb_ref[...])

    def per_head(x, first):
        return jnp.concatenate([jnp.broadcast_to(x[:, first + h:first + h + 1], (x.shape[0], dk))
                                for h in range(nh)], axis=1)

    ri = _iota2((c, c), 0)
    ci = _iota2((c, c), 1)
    tri = (ri >= ci).astype(BF16)
    upper = (ri > ci).astype(F32)
    row_i = _iota2((c, gw), 0)
    lane_j = _iota2((c, gw), 1) & (c - 1)
    lane_head = _blk(_iota2((c, gw), 1), c)
    low_i = row_i >= lane_j
    low_s = row_i > lane_j
    sel = (row_i == lane_j).astype(BF16)
    bd_tok = _bd_mask((gw, gw), c, c)
    bd_kt = _bd_mask((hw, gw), dk, c)
    bd_vn = _bd_mask((B_PAIR * c, gw), c, dk)
    bd_st = _bd_mask((gw, gw), dk, dk)

    pre = []
    for cc in range(cps):
        rs = slice(cc * c, (cc + 1) * c)
        g_c = g_all[rs]
        gmat = _mm_exact_lhs(tri, jnp.concatenate(
            [g_c[:, nh + h:nh + h + 1] * upper for h in range(nh)] + [g_c], axis=1))
        decay = jnp.where(low_i, jnp.exp(gmat[:, 0:gw]), 0.0)
        gc_all = gmat[:, gw:gw + 128]
        gc4 = per_head(gc_all, nh)
        gl4 = per_head(gc_all[c - 1:c, :], nh)
        egc4 = jnp.exp(gc4)
        beta4 = per_head(beta_all[rs], 0)
        kc = k4[rs]
        qc = q4[rs]
        kb = kc * beta4
        kt = jnp.where(bd_kt, _mm(kc, sel, trans_a=True), 0.0)
        gkq = _mm(jnp.concatenate([kb, qc], axis=0), kt)
        pre.append(dict(lmat=jnp.where(low_s, gkq[0:c] * decay, 0.0), qk=gkq[c:2 * c] * decay,
                        vb=v4[rs] * beta4, kbe=kb * egc4, q_in=qc * egc4,
                        k_out=kc * jnp.exp(gl4 - gc4), egl=jnp.exp(gl4)))
    tmats = _neumann_inv_cat([-p["lmat"] for p in pre], bd_tok)
    for p, tmat in zip(pre, tmats):
        us, ws = [], []
        for h in range(nh):
            hs = slice(h * dk, (h + 1) * dk)
            rhs = jnp.concatenate([p["vb"][:, hs], p["kbe"][:, hs]], axis=1)
            uw = _mm(jnp.where(lane_head == h, tmat, 0.0), jnp.concatenate([rhs] * nh, axis=0))
            us.append(uw[:, 0:dk])
            ws.append(uw[:, dk:2 * dk])
        p["u"] = jnp.concatenate(us, axis=1)
        p["w"] = jnp.concatenate(ws, axis=1)

    state = [s_ref[i] for i in range(nh // B_PAIR)]
    for cc, p in enumerate(pre):
        rs = slice(cc * c, (cc + 1) * c)
        outs = []
        for i in range(nh // B_PAIR):
            ps = slice(i * gw, (i + 1) * gw)
            ws = _mm(jnp.concatenate([p["w"][:, ps], p["q_in"][:, ps]], axis=0), state[i])
            v_new = p["u"][:, ps] - ws[0:c]
            qk = p["qk"][:, i * B_PAIR * c:(i + 1) * B_PAIR * c]
            outs.append(ws[c:2 * c] + _mm(qk, _bd_rows(v_new, bd_vn)))
            state[i] = state[i] * p["egl"][:, ps] + jnp.where(
                bd_st, _mm(p["k_out"][:, ps], v_new, trans_a=True), 0.0)
        o = jnp.concatenate(outs, axis=1)
        for h in range(nh):
            hs = slice(h * dk, (h + 1) * dk)
            out_ref[rs, hs] = _rms_norm(o[:, hs], ng_ref[...]) * _silu(z_ref[rs, hs])
    for i in range(nh // B_PAIR):
        s_ref[i] = state[i]
        sl_ref[i] = state[i]


def _gdn(qkv, z, gs, buf8, s0, cw, alog, dtb, ng, cps):
    b, l, _ = qkv.shape
    tt = cps * CHUNK
    w3 = 3 * B_WIDTH
    const = lambda shape: pl.BlockSpec(shape, lambda bi, i: (0,) * len(shape))
    st = (B_HEADS // B_PAIR, MXU_TILE, MXU_TILE)
    return pl.pallas_call(
        functools.partial(_gdn_kernel, cps=cps),
        grid=(b, l // tt),
        in_specs=[pl.BlockSpec((None, tt, w3), lambda bi, i: (bi, i, 0)),
                  pl.BlockSpec((None, tt, B_WIDTH), lambda bi, i: (bi, i, 0)),
                  pl.BlockSpec((None, tt, 128), lambda bi, i: (bi, i, 0)),
                  pl.BlockSpec((None, ROW_PAD, w3), lambda bi, i: (bi, 0, 0)),
                  pl.BlockSpec((None,) + st, lambda bi, i: (bi, 0, 0, 0)),
                  const((4, w3)), const((1, 128)), const((1, 128)), const((1, B_HEAD_DIM))],
        out_specs=[pl.BlockSpec((None, tt, B_WIDTH), lambda bi, i: (bi, i, 0)),
                   pl.BlockSpec((None, ROW_PAD, w3), lambda bi, i: (bi, 0, 0)),
                   pl.BlockSpec((None,) + st, lambda bi, i: (bi, 0, 0, 0))],
        out_shape=[jax.ShapeDtypeStruct((b, l, B_WIDTH), F32),
                   jax.ShapeDtypeStruct((b, ROW_PAD, w3), F32),
                   jax.ShapeDtypeStruct((b,) + st, F32)],
        scratch_shapes=[pltpu.VMEM((ROW_PAD + tt, w3), F32), pltpu.VMEM(st, F32)],
        compiler_params=_params(("parallel", "arbitrary")),
        name="gdn",
    )(qkv, z, gs, buf8, s0, cw, alog, dtb, ng)


def _ffn_kernel(ma_ref, mb_ref, x_ref, pe_ref, buf_ref, wo_ref, g0_ref, b0_ref, wup_ref, cw_ref, cb_ref,
                wdn_ref, g_ref, b_ref, wg_ref, bg_ref, wp_ref, o_ref, nbuf_ref, prev_ref, act_ref, *, tt):
    ff = D_FF
    nb = ff // FF_BLOCK
    half = ma_ref.shape[1]

    @pl.when(pl.program_id(1) == 0)
    def _():
        prev_ref[...] = buf_ref[...]

    mix = jnp.dot(ma_ref[...].astype(BF16), wo_ref[0:half, :], preferred_element_type=F32)
    mix = mix + jnp.dot(mb_ref[...].astype(BF16), wo_ref[half:2 * half, :], preferred_element_type=F32)
    x = _layer_norm(ALPHA * x_ref[...] + mix, g0_ref[...], b0_ref[...])
    xb = x.astype(BF16)
    row = _iota2((tt, FF_BLOCK), 0)

    def conv(col):
        sl = slice(col, col + FF_BLOCK)
        h = jnp.dot(xb, wup_ref[:, sl], preferred_element_type=F32)
        p1 = prev_ref[ROW_PAD - 1:ROW_PAD, sl]
        p2 = prev_ref[ROW_PAD - 2:ROW_PAD - 1, sl]
        h1 = jnp.where(row == 0, p1, pltpu.roll(h, 1, 0))
        h2 = jnp.where(row == 0, p2, jnp.where(row == 1, p1, pltpu.roll(h, 2, 0)))
        prev_ref[:, sl] = h[tt - ROW_PAD:tt, :]
        return h2 * cw_ref[0:1, sl] + h1 * cw_ref[1:2, sl] + h * cw_ref[2:3, sl] + cb_ref[:, sl]

    for j in range(nb):
        gate = conv(j * FF_BLOCK)
        val = conv(ff + j * FF_BLOCK)
        act_ref[:, j * FF_BLOCK:(j + 1) * FF_BLOCK] = (_silu(gate) * val).astype(BF16)
    nbuf_ref[...] = prev_ref[...]
    f = jnp.dot(act_ref[...], wdn_ref[...], preferred_element_type=F32)
    y = _layer_norm(ALPHA * x + f, g_ref[...], b_ref[...])
    gate = _sigmoid(jnp.dot(y.astype(BF16), wg_ref[...], preferred_element_type=F32) + bg_ref[...])
    emb = jnp.dot(pe_ref[...].astype(BF16), wp_ref[...], preferred_element_type=F32)
    o_ref[...] = y + gate * emb


def _ffn(ma, mb, x, pe, buf8, wo, g0, b0, wup, cw, cb, wdn, g, bb, wg, bg, wp, tt):
    b, l, d = x.shape
    half = ma.shape[2]
    ff2 = 2 * D_FF
    const = lambda shape: pl.BlockSpec(shape, lambda bi, i: (0,) * len(shape),
                                       pipeline_mode=pl.Buffered(1))
    return pl.pallas_call(
        functools.partial(_ffn_kernel, tt=tt),
        grid=(b, l // tt),
        in_specs=[pl.BlockSpec((None, tt, half), lambda bi, i: (bi, i, 0)),
                  pl.BlockSpec((None, tt, half), lambda bi, i: (bi, i, 0)),
                  pl.BlockSpec((None, tt, d), lambda bi, i: (bi, i, 0)),
                  pl.BlockSpec((None, tt, PLE_DIM), lambda bi, i: (bi, i, 0)),
                  pl.BlockSpec((None, ROW_PAD, ff2), lambda bi, i: (bi, 0, 0)),
                  const((2 * half, d)), const((1, d)), const((1, d)),
                  const((d, ff2)), const((3, ff2)), const((1, ff2)), const((D_FF, d)),
                  const((1, d)), const((1, d)), const((d, d)), const((1, d)), const((PLE_DIM, d))],
        out_specs=[pl.BlockSpec((None, tt, d), lambda bi, i: (bi, i, 0)),
                   pl.BlockSpec((None, ROW_PAD, ff2), lambda bi, i: (bi, 0, 0))],
        out_shape=[jax.ShapeDtypeStruct((b, l, d), F32),
                   jax.ShapeDtypeStruct((b, ROW_PAD, ff2), F32)],
        scratch_shapes=[pltpu.VMEM((ROW_PAD, ff2), F32), pltpu.VMEM((tt, D_FF), BF16)],
        compiler_params=_params(("parallel", "arbitrary")),
        name="ffn",
    )(ma, mb, x, pe, buf8, wo, g0, b0, wup, cw, cb, wdn, g, bb, wg, bg, wp)


def _rwkv_kernel(p_ref, sh_ref, s0_ref, mu_ref, w0_ref, w2_ref, a0_ref, a2_ref, g2_ref, kk_ref,
                 ka_ref, rk_ref, lg_ref, lb_ref, seg_ref,
                 out_ref, nsh_ref, sl_ref, xp_ref, s_ref, *, cps):
    c = CHUNK
    w = C_WIDTH
    n = C_HEAD_DIM
    gw = MXU_TILE
    ng = w // gw
    tt = cps * c

    @pl.when(pl.program_id(1) == 0)
    def _():
        xp_ref[0:ROW_PAD, :] = sh_ref[...]
        s_ref[...] = s0_ref[...]

    proj = p_ref[...]
    xp_ref[ROW_PAD:ROW_PAD + tt, :] = proj
    prev = xp_ref[ROW_PAD - 1:ROW_PAD - 1 + tt, :]
    last = proj[tt - ROW_PAD:tt, :]
    xp_ref[0:ROW_PAD, :] = last
    nsh_ref[...] = last
    xs = proj + (prev - proj) * mu_ref[...]

    seg = seg_ref[...]

    def head_sum(x):
        return jnp.concatenate([_mm_exact_rhs(x[:, i * gw:(i + 1) * gw], seg) for i in range(ng)], axis=1)

    r = xs[:, 0:w]
    k = xs[:, w:2 * w]
    v = xs[:, 2 * w:3 * w]
    lora_in = xs[:, 3 * w:3 * w + 128]
    ww = -_softplus(-(w0_ref[...] + _mm(jnp.tanh(lora_in), w2_ref[...]))) - 0.5
    a = _sigmoid(a0_ref[...] + _mm(lora_in, a2_ref[...]))
    g = _mm(_sigmoid(xs[:, 3 * w + 128:3 * w + 256]), g2_ref[...])
    kx = k * kk_ref[...]
    k = k * (1.0 + (a - 1.0) * ka_ref[...])
    sums = head_sum(jnp.concatenate([kx * kx, r * k * rk_ref[...]], axis=0))
    kk = kx * lax.rsqrt(sums[0:tt] + 1e-6)
    bonus = sums[tt:2 * tt] * v
    lw = -jnp.exp(ww)
    sa = -kk
    sb = kk * a

    ri = _iota2((c, c), 0)
    ci = _iota2((c, c), 1)
    tri = (ri >= ci).astype(BF16)
    row_i = _iota2((c, gw), 0)
    lane_j = _iota2((c, gw), 1) & (c - 1)
    low_i = row_i >= lane_j
    low_s = row_i > lane_j
    sel = (row_i == lane_j).astype(BF16)
    bd = _bd_mask((gw, gw), n, n)

    units = []
    for cc in range(cps):
        rs = slice(cc * c, (cc + 1) * c)
        lw_c = lw[rs]
        cum = _mm_exact_lhs(tri, lw_c)
        cum_last = cum[c - 1:c, :]
        e_out = jnp.exp(-cum)
        e_end = jnp.exp(cum_last - cum)
        a_t = sa[rs] * jnp.exp(cum - lw_c)
        r_t = r[rs] * jnp.exp(cum)
        b_t = sb[rs] * e_out
        k_t = k[rs] * e_out
        b_o = sb[rs] * e_end
        k_o = k[rs] * e_end
        e_last = jnp.exp(cum_last)
        for i in range(ng):
            gs = slice(i * gw, (i + 1) * gw)
            units.append(dict(cc=cc, i=i, lr=jnp.concatenate([a_t[:, gs], r_t[:, gs]], axis=0),
                              b_t=b_t[:, gs], k_t=k_t[:, gs], v=v[rs, gs],
                              bk=jnp.concatenate([b_o[:, gs], k_o[:, gs]], axis=0), e_last=e_last[:, gs]))
    for un in units:
        un["bt"] = jnp.where(bd, _mm(un["b_t"], sel, trans_a=True), 0.0)
        un["kt"] = jnp.where(bd, _mm(un["k_t"], sel, trans_a=True), 0.0)
    for un in units:
        g1 = _mm(un["lr"], un["bt"])
        g2 = _mm(un["lr"], un["kt"])
        un["m_ab"] = jnp.where(low_s, g1[0:c], 0.0)
        un["m_ak"] = jnp.where(low_s, g2[0:c], 0.0)
        un["m_r"] = jnp.concatenate([jnp.where(low_i, g1[c:2 * c], 0.0),
                                     jnp.where(low_i, g2[c:2 * c], 0.0)], axis=1)
        un["bdv"] = _bd_rows(un["v"], bd)
    for un, tmat in zip(units, _neumann_inv_cat([un["m_ab"] for un in units], bd)):
        un["tmat"] = tmat
        un["akv"] = _mm(un["m_ak"], un["bdv"])

    state = [s_ref[i] for i in range(ng)]
    outs = [[None] * ng for _ in range(cps)]
    for un in units:
        i = un["i"]
        ls = _mm(un["lr"], state[i], trans_b=True)
        u = _mm(un["tmat"], _bd_rows(ls[0:c] + un["akv"], bd))
        outs[un["cc"]][i] = ls[c:2 * c] + _mm(
            un["m_r"], jnp.concatenate([_bd_rows(u, bd), un["bdv"]], axis=0))
        uv = jnp.concatenate([u, un["v"]], axis=0)
        state[i] = state[i] * un["e_last"] + jnp.where(bd, _mm(uv, un["bk"], trans_a=True), 0.0)
    for i in range(ng):
        s_ref[i] = state[i]
        sl_ref[i] = state[i]

    o = jnp.concatenate([jnp.concatenate(row, axis=1) for row in outs], axis=0)
    inv_n = 1.0 / n
    oc = o - head_sum(o) * inv_n
    var = head_sum(oc * oc) * inv_n
    on = oc * lax.rsqrt(var + C_LNX_EPS) * lg_ref[...] + lb_ref[...]
    out_ref[...] = (on + bonus) * g


def _rwkv(proj, sh8, s0, mu, w0, w2p, a0, a2p, g2, kk, ka, rk, lg, lb, seg, cps):
    b, l, cp = proj.shape
    tt = cps * CHUNK
    w = C_WIDTH
    const = lambda shape: pl.BlockSpec(shape, lambda bi, i: (0,) * len(shape))
    st = (C_HEADS // C_GROUP, MXU_TILE, MXU_TILE)
    return pl.pallas_call(
        functools.partial(_rwkv_kernel, cps=cps),
        grid=(b, l // tt),
        in_specs=[pl.BlockSpec((None, tt, cp), lambda bi, i: (bi, i, 0)),
                  pl.BlockSpec((None, ROW_PAD, cp), lambda bi, i: (bi, 0, 0)),
                  pl.BlockSpec((None,) + st, lambda bi, i: (bi, 0, 0, 0)),
                  const((1, cp)), const((1, w)), const((128, w)), const((1, w)), const((128, w)),
                  const((128, w)), const((1, w)), const((1, w)), const((1, w)), const((1, w)),
                  const((1, w)), const((MXU_TILE, MXU_TILE))],
        out_specs=[pl.BlockSpec((None, tt, w), lambda bi, i: (bi, i, 0)),
                   pl.BlockSpec((None, ROW_PAD, cp), lambda bi, i: (bi, 0, 0)),
                   pl.BlockSpec((None,) + st, lambda bi, i: (bi, 0, 0, 0))],
        out_shape=[jax.ShapeDtypeStruct((b, l, w), F32),
                   jax.ShapeDtypeStruct((b, ROW_PAD, cp), F32),
                   jax.ShapeDtypeStruct((b,) + st, F32)],
        scratch_shapes=[pltpu.VMEM((ROW_PAD + tt, cp), F32), pltpu.VMEM(st, F32)],
        compiler_params=_params(("parallel", "arbitrary")),
        name="rwkv7",
    )(proj, sh8, s0, mu, w0, w2p, a0, a2p, g2, kk, ka, rk, lg, lb, seg)


def _mla_prep_kernel(d_ref, ct_ref, st_ref, kc_ref, qg_ref, wq_ref, wqs_ref, kg_ref,
                     q_ref, c_ref, kr_ref):
    lq = D_Q_LORA
    lkv = D_KV_LORA
    qn = _rms_norm(d_ref[:, 0:lq], qg_ref[...]).astype(BF16)
    ct = ct_ref[...]
    st = st_ref[...]
    for h in range(D_HEADS):
        sl = slice(h * D_QK_PAD, (h + 1) * D_QK_PAD)
        q = jnp.dot(qn, wq_ref[:, sl], preferred_element_type=F32) * ct
        q = q + jnp.dot(qn, wqs_ref[:, sl], preferred_element_type=F32) * st
        q_ref[:, sl] = (q * MLA_SCALE).astype(BF16)
    c_ref[...] = _rms_norm(d_ref[:, lq:lq + lkv], kg_ref[...])
    o = lq + lkv
    kr_ref[...] = d_ref[:, o:o + D_QK_PAD] * kc_ref[...] + d_ref[:, o + D_QK_PAD:o + 2 * D_QK_PAD] * st


def _mla_prep(d, ctab, stab, kctab, qg, wq, wqs, kg, tm):
    b, l, dw = d.shape
    hq = D_HEADS * D_QK_PAD
    const = lambda shape: pl.BlockSpec(shape, lambda bi, i: (0,) * len(shape))
    tab = pl.BlockSpec((tm, D_QK_PAD), lambda bi, i: (i, 0))
    return pl.pallas_call(
        _mla_prep_kernel,
        grid=(b, l // tm),
        in_specs=[pl.BlockSpec((None, tm, dw), lambda bi, i: (bi, i, 0)), tab, tab, tab,
                  const((1, D_Q_LORA)), const((D_Q_LORA, hq)), const((D_Q_LORA, hq)),
                  const((1, D_KV_LORA))],
        out_specs=[pl.BlockSpec((None, tm, hq), lambda bi, i: (bi, i, 0)),
                   pl.BlockSpec((None, tm, D_KV_LORA), lambda bi, i: (bi, i, 0)),
                   pl.BlockSpec((None, tm, D_QK_PAD), lambda bi, i: (bi, i, 0))],
        out_shape=[jax.ShapeDtypeStruct((b, l, hq), BF16),
                   jax.ShapeDtypeStruct((b, l, D_KV_LORA), F32),
                   jax.ShapeDtypeStruct((b, l, D_QK_PAD), F32)],
        compiler_params=_params(("parallel", "parallel")),
        name="mla_prep",
    )(d, ctab, stab, kctab, qg, wq, wqs, kg)


def _kv_kernel(c_ref, kr_ref, wk_ref, wv_ref, k_ref, v_ref):
    cb = c_ref[...].astype(BF16)
    kr = kr_ref[...]
    ones = jnp.ones((cb.shape[0], D_V_PAD - D_V), BF16)
    for h in range(D_HEADS):
        sl = slice(h * D_QK_PAD, (h + 1) * D_QK_PAD)
        k_ref[:, sl] = (jnp.dot(cb, wk_ref[:, sl], preferred_element_type=F32) + kr).astype(BF16)
        vh = jnp.dot(cb, wv_ref[:, h * D_V:(h + 1) * D_V], preferred_element_type=F32).astype(BF16)
        v_ref[:, h * D_V_PAD:h * D_V_PAD + D_V] = vh
        v_ref[:, h * D_V_PAD + D_V:(h + 1) * D_V_PAD] = ones


def _kv(c, krp, wk, wv, tm):
    t = c.shape[0]
    hk = D_HEADS * D_QK_PAD
    hv = D_HEADS * D_V_PAD
    const = lambda shape: pl.BlockSpec(shape, lambda i: (0,) * len(shape))
    return pl.pallas_call(
        _kv_kernel,
        grid=(t // tm,),
        in_specs=[pl.BlockSpec((tm, D_KV_LORA), lambda i: (i, 0)),
                  pl.BlockSpec((tm, D_QK_PAD), lambda i: (i, 0)),
                  const((D_KV_LORA, hk)), const((D_KV_LORA, D_HEADS * D_V))],
        out_specs=[pl.BlockSpec((tm, hk), lambda i: (i, 0)), pl.BlockSpec((tm, hv), lambda i: (i, 0))],
        out_shape=[jax.ShapeDtypeStruct((t, hk), BF16), jax.ShapeDtypeStruct((t, hv), BF16)],
        compiler_params=_params(("parallel",)),
        name="mla_kv",
    )(c, krp, wk, wv)


def _attn_kernel(q_ref, k_ref, v_ref, o_ref, m_ref, acc_ref, *, tq, tk, nk, q_off):
    hps = ATTN_HEADS
    q_first = q_off + pl.program_id(2) * tq
    first_chunk_end = q_first // CHUNK * CHUNK + CHUNK
    last_chunk_end = (q_first + tq - 1) // CHUNK * CHUNK + CHUNK
    n_full = jnp.minimum(nk, first_chunk_end // tk)
    n_live = jnp.minimum(nk, (last_chunk_end + tk - 1) // tk)
    m_ref[...] = jnp.full(m_ref.shape, -jnp.inf, F32)
    acc_ref[...] = jnp.zeros(acc_ref.shape, F32)
    qs = [q_ref[:, h * D_QK_PAD:(h + 1) * D_QK_PAD] for h in range(hps)]

    def step(j, masked):
        start = pl.multiple_of(j * tk, tk)
        ss = [lax.dot_general(qs[h], k_ref[pl.ds(start, tk), h * D_QK_PAD:(h + 1) * D_QK_PAD],
                              _dims(False, True), preferred_element_type=F32) for h in range(hps)]
        if masked:
            q_chunk = jnp.right_shift(q_first + _iota2((tq, tk), 0), CHUNK_SHIFT)
            k_chunk = jnp.right_shift(start + _iota2((tq, tk), 1), CHUNK_SHIFT)
            ss = [jnp.where(k_chunk <= q_chunk, s, -jnp.inf) for s in ss]
        m_old = [m_ref[h] for h in range(hps)]
        m_new = [jnp.maximum(m_old[h], jnp.max(ss[h], axis=-1, keepdims=True)) for h in range(hps)]
        ps = [jnp.exp(ss[h] - m_new[h]).astype(BF16) for h in range(hps)]
        for h in range(hps):
            acc_ref[h] = jnp.exp(m_old[h] - m_new[h]) * acc_ref[h] + jnp.dot(
                ps[h], v_ref[pl.ds(start, tk), h * D_V_PAD:(h + 1) * D_V_PAD], preferred_element_type=F32)
            m_ref[h] = m_new[h]

    def full_body(j, carry):
        step(j, False)
        return carry

    def edge_body(j, carry):
        step(j, True)
        return carry

    lax.fori_loop(0, n_full, full_body, 0)
    lax.fori_loop(n_full, n_live, edge_body, 0)
    for h in range(hps):
        acc = acc_ref[h]
        o_ref[:, h * D_V:(h + 1) * D_V] = acc[:, 0:D_V] / acc[:, D_V:D_V + 1]


def _attention(q, k, v, tq, tk, q_off):
    b, l, _ = q.shape
    lk = k.shape[1]
    hps = ATTN_HEADS
    return pl.pallas_call(
        functools.partial(_attn_kernel, tq=tq, tk=tk, nk=lk // tk, q_off=q_off),
        grid=(b, D_HEADS // hps, l // tq),
        in_specs=[pl.BlockSpec((None, tq, hps * D_QK_PAD), lambda bi, h, i: (bi, i, h)),
                  pl.BlockSpec((None, lk, hps * D_QK_PAD), lambda bi, h, i: (bi, 0, h)),
                  pl.BlockSpec((None, lk, hps * D_V_PAD), lambda bi, h, i: (bi, 0, h))],
        out_specs=pl.BlockSpec((None, tq, hps * D_V), lambda bi, h, i: (bi, i, h)),
        out_shape=jax.ShapeDtypeStruct((b, l, D_HEADS * D_V), F32),
        scratch_shapes=[pltpu.VMEM((hps, tq, 1), F32), pltpu.VMEM((hps, tq, D_V_PAD), F32)],
        compiler_params=_params(("parallel", "parallel", "arbitrary")),
        name="mla_attention",
    )(q, k, v)


def _pad_rows_front(buf):
    return jnp.pad(buf, ((0, 0), (ROW_PAD - buf.shape[1], 0), (0, 0)))


def _block_diag(w):
    nb, bi, bj = w.shape
    eye = jnp.eye(nb, dtype=w.dtype)
    return (eye[:, None, :, None] * w[:, :, None, :]).reshape(nb * bi, nb * bj)


def _state_to_block_diag(s, per):
    b, h, n, m = s.shape
    eye = jnp.eye(per, dtype=s.dtype)
    s6 = s.reshape(b, h // per, per, n, 1, m) * eye[None, None, :, None, :, None]
    return s6.reshape(b, h // per, per * n, per * m)


def _state_from_block_diag(sbd, per):
    b, g, pn, pm = sbd.shape
    n, m = pn // per, pm // per
    s6 = sbd.reshape(b, g, per, n, per, m)
    return jnp.stack([s6[:, :, p, :, p, :] for p in range(per)], axis=2).reshape(b, g * per, n, m)


def _rope_swap(w):
    half = w.shape[-1] // 2
    return jnp.concatenate([-w[..., half:], w[..., :half]], axis=-1)


def _prepare(w_in0, a_w_r, a_w_i, a_b_r, a_b_i, w_in1, c_w2, c_a2, d_w_qb, d_w_kvb):
    p = {}
    o = 0
    p["w_a"] = w_in0[:, o:o + 2 * A_WIDTH].astype(BF16); o += 2 * A_WIDTH
    p["w_qkv"] = w_in0[:, o:o + 3 * B_WIDTH].astype(BF16); o += 3 * B_WIDTH
    p["w_z"] = w_in0[:, o:o + B_WIDTH].astype(BF16); o += B_WIDTH
    p["w_gs"] = jnp.pad(w_in0[:, o:o + 2 * B_HEADS], ((0, 0), (0, 128 - 2 * B_HEADS))).astype(BF16)
    p["a_wg"] = jnp.concatenate([_block_diag(a_w_r), _block_diag(a_w_i)], axis=1).astype(BF16)
    p["a_bg"] = jnp.concatenate([a_b_r, a_b_i])[None, :]
    p["w_c"] = w_in1[:, 0:C_PROJ].astype(BF16)
    o = C_PROJ
    w_qa = w_in1[:, o:o + D_Q_LORA]; o += D_Q_LORA
    w_craw = w_in1[:, o:o + D_KV_LORA]; o += D_KV_LORA
    w_kr = w_in1[:, o:o + D_ROPE]
    place = lambda w: jnp.pad(w, ((0, 0), (D_NOPE, D_QK_PAD - D_NOPE - D_ROPE)))
    p["w_d"] = jnp.concatenate([w_qa, w_craw, place(w_kr), place(_rope_swap(w_kr))], axis=1).astype(BF16)
    p["c_w2p"] = jnp.pad(c_w2, ((0, 64), (0, 0)))
    p["c_a2p"] = jnp.pad(c_a2, ((64, 0), (0, 0)))
    wq = d_w_qb.reshape(D_Q_LORA, D_HEADS, D_NOPE + D_ROPE)
    zpad = jnp.zeros((D_Q_LORA, D_HEADS, D_QK_PAD - D_NOPE - D_ROPE), F32)
    p["wq"] = jnp.concatenate([wq, zpad], axis=-1).reshape(D_Q_LORA, D_HEADS * D_QK_PAD).astype(BF16)
    wq_sw = jnp.concatenate([jnp.zeros((D_Q_LORA, D_HEADS, D_NOPE), F32), _rope_swap(wq[..., D_NOPE:]), zpad], axis=-1)
    p["wqs"] = wq_sw.reshape(D_Q_LORA, D_HEADS * D_QK_PAD).astype(BF16)
    wkv = d_w_kvb.reshape(D_KV_LORA, D_HEADS, D_NOPE + D_V)
    wk = jnp.pad(wkv[..., :D_NOPE], ((0, 0), (0, 0), (0, D_QK_PAD - D_NOPE)))
    p["wk"] = wk.reshape(D_KV_LORA, D_HEADS * D_QK_PAD).astype(BF16)
    p["wv"] = wkv[..., D_NOPE:].reshape(D_KV_LORA, D_HEADS * D_V).astype(BF16)
    lane = jnp.arange(MXU_TILE) // C_HEAD_DIM
    p["seg"] = (lane[:, None] == lane[None, :]).astype(BF16)
    return p


def _rope_tables(pos):
    half = D_ROPE // 2
    inv = ROPE_THETA ** (-jnp.arange(half, dtype=F32) / half)
    ang = pos.astype(F32)[:, None] * inv[None, :]
    cos = jnp.cos(ang)
    sin = jnp.sin(ang)
    n = pos.shape[0]
    tail = jnp.zeros((n, D_QK_PAD - D_NOPE - D_ROPE), F32)
    cos2 = jnp.concatenate([cos, cos], axis=1)
    sin2 = jnp.concatenate([sin, sin], axis=1)
    q_cos = jnp.concatenate([jnp.ones((n, D_NOPE), F32), cos2, tail], axis=1)
    q_sin = jnp.concatenate([jnp.zeros((n, D_NOPE), F32), sin2, tail], axis=1)
    k_cos = jnp.concatenate([jnp.zeros((n, D_NOPE), F32), cos2, tail], axis=1)
    return q_cos, q_sin, k_cos


def _trunk(x, pe, pos, a_conv, a_h, b_conv, b_s, c_shift, c_s, d_ckv, d_krope, f_conv, wts, prm):
    (a_conv_w, a_conv_b, a_lambda, b_conv_w, b_a_log, b_dt_bias, b_norm_g, w_out0, c_mu, c_w0,
     c_a0, c_g2, c_k_k, c_k_a, c_r_k, c_lnx_g, c_lnx_b, d_qa_g, d_kva_g, w_out1, ln_mix_g, ln_mix_b,
     ffn_w_up, ffn_conv_w, ffn_conv_b, ffn_w_down, ln_ffn_g, ln_ffn_b, ple_w_gate, ple_b_gate,
     ple_w_proj) = wts
    b, l, d = x.shape
    t = b * l
    tm = min(512, t)
    tt = min(512, l)
    cps = 2 if l % (2 * CHUNK) == 0 else 1
    row = lambda vec: vec.reshape(1, -1)
    f_new = []

    def ffn_block(i, mix_a, mix_b, xin, w_out):
        y, nbuf = _ffn(mix_a, mix_b, xin.reshape(b, l, d), pe[i], _pad_rows_front(f_conv[i]),
                       w_out.astype(BF16), row(ln_mix_g[i]), row(ln_mix_b[i]),
                       ffn_w_up[i].astype(BF16), ffn_conv_w[i], row(ffn_conv_b[i]),
                       ffn_w_down[i].astype(BF16), row(ln_ffn_g[i]), row(ln_ffn_b[i]),
                       ple_w_gate[i].astype(BF16), row(ple_b_gate[i]), ple_w_proj[i].astype(BF16), tt)
        f_new.append(nbuf[:, ROW_PAD - 2:, :])
        return y.reshape(t, d)

    x2 = x.reshape(t, d)
    a2, qkv, z, gs = _proj(x2, [prm["w_a"], prm["w_qkv"], prm["w_z"], prm["w_gs"]], tm)
    a_out, a_buf, a_hl = _rglru(a2.reshape(b, l, -1), _pad_rows_front(a_conv), a_h[:, None, :],
                                a_conv_w, row(a_conv_b), prm["a_wg"], prm["a_bg"], row(a_lambda),
                                min(256, l))
    pad8 = lambda vec: jnp.pad(vec, (B_HEADS, 128 - 2 * B_HEADS))[None, :]
    b_out, b_buf, b_sl = _gdn(qkv.reshape(b, l, -1), z.reshape(b, l, -1), gs.reshape(b, l, -1),
                              _pad_rows_front(b_conv), _state_to_block_diag(b_s, B_PAIR), b_conv_w,
                              pad8(b_a_log), pad8(b_dt_bias), row(b_norm_g), cps)
    x2 = ffn_block(0, a_out, b_out, x2, w_out0)

    c_in, d_in = _proj(x2, [prm["w_c"], prm["w_d"]], tm)
    c_out, c_sh, c_sl = _rwkv(c_in.reshape(b, l, -1), _pad_rows_front(c_shift),
                              _state_to_block_diag(c_s, C_GROUP), row(c_mu), row(c_w0),
                              prm["c_w2p"], row(c_a0), prm["c_a2p"], c_g2, row(c_k_k), row(c_k_a),
                              row(c_r_k.reshape(-1)), row(c_lnx_g), row(c_lnx_b), prm["seg"], cps)
    q_cos, q_sin, k_cos = _rope_tables(pos)
    q, c_new, kr_pad = _mla_prep(d_in.reshape(b, l, -1), q_cos, q_sin, k_cos, row(d_qa_g), prm["wq"],
                                 prm["wqs"], row(d_kva_g), min(512, l))
    past = d_ckv.shape[1]
    if past:
        c_all = jnp.concatenate([d_ckv, c_new], axis=1)
        kr_old = jnp.pad(d_krope, ((0, 0), (0, 0), (D_NOPE, D_QK_PAD - D_NOPE - D_ROPE)))
        kr_all = jnp.concatenate([kr_old, kr_pad], axis=1)
    else:
        c_all, kr_all = c_new, kr_pad
    lk = past + l
    tkv = b * lk
    k, v = _kv(c_all.reshape(tkv, -1), kr_all.reshape(tkv, -1), prm["wk"], prm["wv"],
               512 if tkv % 512 == 0 else lk)
    tq = min(ATTN_TQ, l)
    tk = ATTN_TK if lk % ATTN_TK == 0 else lk
    d_out = _attention(q, k.reshape(b, lk, -1), v.reshape(b, lk, -1), tq, tk, past)
    x2 = ffn_block(1, c_out, d_out, x2, w_out1)

    return (x2.reshape(b, l, d), a_buf[:, ROW_PAD - 3:, :], a_hl[:, 0, :], b_buf[:, ROW_PAD - 3:, :],
            _state_from_block_diag(b_sl, B_PAIR), c_sh[:, ROW_PAD - 1:, :],
            _state_from_block_diag(c_sl, C_GROUP), c_new, kr_pad[:, :, D_NOPE:D_NOPE + D_ROPE],
            jnp.stack(f_new))


def kernel(x_prompt, x_sample, state_a_conv, state_a_h, state_b_conv, state_b_s, state_c_shift, state_c_s, cache_d_ckv, cache_d_krope, state_ffn_conv, p_prompt, p_sample, w_in0, a_conv_w, a_conv_b, a_w_r, a_b_r, a_w_i, a_b_i, a_lambda, b_conv_w, b_a_log, b_dt_bias, b_norm_g, w_out0, w_in1, c_mu, c_w0, c_w2, c_a0, c_a2, c_g2, c_k_k, c_k_a, c_r_k, c_lnx_g, c_lnx_b, d_qa_g, d_w_qb, d_kva_g, d_w_kvb, w_out1, ln_mix_g, ln_mix_b, ffn_w_up, ffn_conv_w, ffn_conv_b, ffn_w_down, ln_ffn_g, ln_ffn_b, ple_w_gate, ple_b_gate, ple_w_proj):
    prm = _prepare(w_in0, a_w_r, a_w_i, a_b_r, a_b_i, w_in1, c_w2, c_a2, d_w_qb, d_w_kvb)
    wts = (a_conv_w, a_conv_b, a_lambda, b_conv_w, b_a_log, b_dt_bias, b_norm_g, w_out0, c_mu, c_w0,
           c_a0, c_g2, c_k_k, c_k_a, c_r_k, c_lnx_g, c_lnx_b, d_qa_g, d_kva_g, w_out1, ln_mix_g, ln_mix_b,
           ffn_w_up, ffn_conv_w, ffn_conv_b, ffn_w_down, ln_ffn_g, ln_ffn_b, ple_w_gate, ple_b_gate,
           ple_w_proj)
    dt = x_prompt.dtype
    bp, lp = x_prompt.shape[0], x_prompt.shape[1]
    zeros = lambda *shape: jnp.zeros(shape, dt)
    prompt = _trunk(x_prompt, p_prompt, jnp.arange(lp),
                    zeros(bp, 3, A_WIDTH), zeros(bp, A_WIDTH), zeros(bp, 3, 3 * B_WIDTH),
                    zeros(bp, B_HEADS, B_HEAD_DIM, B_HEAD_DIM), zeros(bp, 1, C_PROJ),
                    zeros(bp, C_HEADS, C_HEAD_DIM, C_HEAD_DIM), zeros(bp, 0, D_KV_LORA),
                    zeros(bp, 0, D_ROPE), zeros(DEPTH, bp, 2, 2 * D_FF), wts, prm)
    ls = x_sample.shape[1]
    past = cache_d_ckv.shape[1]
    sample = _trunk(x_sample, p_sample, past + jnp.arange(ls), state_a_conv, state_a_h, state_b_conv,
                    state_b_s, state_c_shift, state_c_s, cache_d_ckv, cache_d_krope, state_ffn_conv,
                    wts, prm)
    out = [prompt[0], sample[0]]
    for ps, ss in zip(prompt[1:], sample[1:]):
        out += [ps, ss]
    return tuple(out)
```

```python
import functools

import jax
import jax.numpy as jnp
from jax import lax
from jax.experimental import pallas as pl
from jax.experimental.pallas import tpu as pltpu

F32 = jnp.float32
BF16 = jnp.bfloat16

D_MODEL = 1024
DEPTH = 2
CHUNK = 64
CHUNK_SHIFT = CHUNK.bit_length() - 1
PLE_DIM = 256
ALPHA = (2 * DEPTH) ** 0.25
NORM_EPS = 1e-5
A_WIDTH = 512
A_BLOCKS = 8
A_C = 8.0
B_HEADS = 4
B_HEAD_DIM = 128
B_WIDTH = 512
C_HEAD_DIM = 64
C_WIDTH = 512
C_HEADS = 8
C_LNX_EPS = 64e-5
C_PROJ = 3 * C_WIDTH + 64 + 64 + 128
D_HEADS = 4
D_Q_LORA = 384
D_KV_LORA = 256
D_NOPE = 128
D_ROPE = 64
D_V = 128
D_QK_PAD = 256
D_V_PAD = 256
MLA_SCALE = (D_NOPE + D_ROPE) ** -0.5
ROPE_THETA = 10000.0
D_FF = 2816
FF_BLOCK = 256
TRI_BASE = 8
MXU_TILE = 256
ROW_PAD = 8
VMEM_LIMIT = 56 * 1024 * 1024
ATTN_HEADS = 2
ATTN_TQ = 256
ATTN_TK = 2048
B_PAIR = MXU_TILE // B_HEAD_DIM
C_GROUP = MXU_TILE // C_HEAD_DIM


def _params(sem):
    return pltpu.CompilerParams(dimension_semantics=sem, vmem_limit_bytes=VMEM_LIMIT)


def _sigmoid(x):
    return 1.0 / (1.0 + jnp.exp(-x))


def _softplus(x):
    return jnp.maximum(x, 0.0) + jnp.log(1.0 + jnp.exp(-jnp.abs(x)))


def _silu(x):
    return x * _sigmoid(x)


def _gelu_tanh(x):
    return 0.5 * x * (1.0 + jnp.tanh(0.7978845608028654 * (x + 0.044715 * x * x * x)))


def _dims(trans_a, trans_b):
    return (((0 if trans_a else 1,), (1 if trans_b else 0,)), ((), ()))


def _mm(a, b, trans_a=False, trans_b=False):
    return lax.dot_general(a.astype(BF16), b.astype(BF16), _dims(trans_a, trans_b),
                           preferred_element_type=F32)


def _split3(a):
    a1 = a.astype(BF16)
    r1 = a - a1.astype(F32)
    a2 = r1.astype(BF16)
    return a1, a2, (r1 - a2.astype(F32)).astype(BF16)


def _mm_exact_rhs(a, mask):
    a1, a2, a3 = _split3(a)
    mb = mask.astype(BF16)
    out = jnp.dot(a3, mb, preferred_element_type=F32)
    out = out + jnp.dot(a2, mb, preferred_element_type=F32)
    return out + jnp.dot(a1, mb, preferred_element_type=F32)


def _mm_exact_lhs(mask, b):
    b1, b2, b3 = _split3(b)
    mb = mask.astype(BF16)
    out = jnp.dot(mb, b3, preferred_element_type=F32)
    out = out + jnp.dot(mb, b2, preferred_element_type=F32)
    return out + jnp.dot(mb, b1, preferred_element_type=F32)


def _iota2(shape, axis):
    return lax.broadcasted_iota(jnp.int32, shape, axis)


def _blk(idx, size):
    return jnp.right_shift(idx, size.bit_length() - 1)


def _bd_mask(shape, row_block, col_block):
    return _blk(_iota2(shape, 0), row_block) == _blk(_iota2(shape, 1), col_block)


def _bd_rows(x, mask):
    reps = mask.shape[0] // x.shape[0]
    return jnp.where(mask, jnp.concatenate([x] * reps, axis=0), 0.0)


def _tri_inv_cat(xs, bd):
    c, wd = xs[0].shape
    row = _iota2((c, wd), 0)
    col = _iota2((c, wd), 1) & (c - 1)
    eye = (row == col).astype(F32)

    def prod(a, b):
        return jnp.dot(a.astype(BF16), _bd_rows(b, bd).astype(BF16), preferred_element_type=F32)

    base = _blk(row, TRI_BASE) == _blk(col, TRI_BASE)
    ps = [jnp.where(base, x, 0.0) for x in xs]
    ts = [eye + p for p in ps]
    k = 2
    while k < TRI_BASE:
        ps = [prod(p, p) for p in ps]
        ts = [t + prod(t, p) for t, p in zip(ts, ps)]
        k *= 2
    size = TRI_BASE
    while size < c:
        pair = (_blk(row, 2 * size) == _blk(col, 2 * size)) & (_blk(row, size) != _blk(col, size))
        offs = [prod(t, jnp.where(pair, x, 0.0)) for t, x in zip(ts, xs)]
        ts = [t + prod(o, t) for t, o in zip(ts, offs)]
        size *= 2
    return ts


def _layer_norm(x, g, b):
    mu = jnp.mean(x, axis=-1, keepdims=True)
    xc = x - mu
    var = jnp.mean(xc * xc, axis=-1, keepdims=True)
    return xc * lax.rsqrt(var + NORM_EPS) * g + b


def _rms_norm(x, g, eps=1e-6):
    return x * lax.rsqrt(jnp.mean(x * x, axis=-1, keepdims=True) + eps) * g


def _proj_kernel(x_ref, *refs):
    n = len(refs) // 2
    xb = x_ref[...].astype(BF16)
    for w_ref, o_ref in zip(refs[:n], refs[n:]):
        o_ref[...] = jnp.dot(xb, w_ref[...], preferred_element_type=F32)


def _proj(x, ws, tm):
    t, k = x.shape
    return pl.pallas_call(
        _proj_kernel,
        grid=(t // tm,),
        in_specs=[pl.BlockSpec((tm, k), lambda i: (i, 0))]
        + [pl.BlockSpec(w.shape, lambda i: (0, 0)) for w in ws],
        out_specs=[pl.BlockSpec((tm, w.shape[1]), lambda i: (i, 0)) for w in ws],
        out_shape=[jax.ShapeDtypeStruct((t, w.shape[1]), F32) for w in ws],
        compiler_params=_params(("parallel",)),
        name="proj",
    )(x, *ws)


def _scan_rows(a, b):
    n = a.shape[0]
    row = _iota2(a.shape, 0)
    s = 1
    while s < n:
        a_sh = pltpu.roll(a, s, 0)
        b_sh = pltpu.roll(b, s, 0)
        m = row >= s
        b = jnp.where(m, a * b_sh + b, b)
        a = jnp.where(m, a * a_sh, a)
        s *= 2
    return a, b


def _rglru_kernel(a2_ref, buf_ref, h0_ref, cw_ref, cb_ref, wg_ref, bg_ref, lam_ref,
                  out_ref, nbuf_ref, hl_ref, xp_ref, hc_ref, *, tt):
    w = A_WIDTH

    @pl.when(pl.program_id(1) == 0)
    def _():
        xp_ref[0:ROW_PAD, :] = buf_ref[...]
        hc_ref[...] = h0_ref[...]

    gate_in = a2_ref[:, 0:w]
    x_in = a2_ref[:, w:2 * w]
    xp_ref[ROW_PAD:ROW_PAD + tt, :] = x_in
    xc = x_in * cw_ref[3:4, :] + cb_ref[...]
    for j in range(3):
        xc = xc + xp_ref[ROW_PAD - 3 + j:ROW_PAD - 3 + j + tt, :] * cw_ref[j:j + 1, :]
    gates = jnp.dot(xc.astype(BF16), wg_ref[...], preferred_element_type=F32) + bg_ref[...]
    r = _sigmoid(gates[:, 0:w])
    ig = _sigmoid(gates[:, w:2 * w])
    log_a = (-A_C) * r * _softplus(-lam_ref[...])
    a = jnp.exp(log_a)
    u = jnp.sqrt(1.0 - a * a) * (ig * xc)
    a_cum, h_loc = _scan_rows(a, u)
    h = a_cum * hc_ref[...] + h_loc
    out_ref[...] = h * _gelu_tanh(gate_in)
    hc_ref[...] = h[tt - 1:tt, :]
    hl_ref[...] = h[tt - 1:tt, :]
    last = x_in[tt - ROW_PAD:tt, :]
    xp_ref[0:ROW_PAD, :] = last
    nbuf_ref[...] = last


def _rglru(a2, buf8, h0, cw, cb, wg, bg, lam, tt):
    b, l, _ = a2.shape
    w = A_WIDTH
    const = lambda shape: pl.BlockSpec(shape, lambda bi, i: (0,) * len(shape))
    return pl.pallas_call(
        functools.partial(_rglru_kernel, tt=tt),
        grid=(b, l // tt),
        in_specs=[pl.BlockSpec((None, tt, 2 * w), lambda bi, i: (bi, i, 0)),
                  pl.BlockSpec((None, ROW_PAD, w), lambda bi, i: (bi, 0, 0)),
                  pl.BlockSpec((None, 1, w), lambda bi, i: (bi, 0, 0)),
                  const((4, w)), const((1, w)), const((w, 2 * w)), const((1, 2 * w)), const((1, w))],
        out_specs=[pl.BlockSpec((None, tt, w), lambda bi, i: (bi, i, 0)),
                   pl.BlockSpec((None, ROW_PAD, w), lambda bi, i: (bi, 0, 0)),
                   pl.BlockSpec((None, 1, w), lambda bi, i: (bi, 0, 0))],
        out_shape=[jax.ShapeDtypeStruct((b, l, w), F32),
                   jax.ShapeDtypeStruct((b, ROW_PAD, w), F32),
                   jax.ShapeDtypeStruct((b, 1, w), F32)],
        scratch_shapes=[pltpu.VMEM((ROW_PAD + tt, w), F32), pltpu.VMEM((1, w), F32)],
        compiler_params=_params(("parallel", "arbitrary")),
        name="rglru",
    )(a2, buf8, h0, cw, cb, wg, bg, lam)


def _gdn_kernel(qkv_ref, z_ref, gs_ref, buf_ref, s0_ref, cw_ref, alog_ref, dtb_ref, ng_ref,
                out_ref, nbuf_ref, sl_ref, xp_ref, s_ref, *, cps):
    c = CHUNK
    dk = B_HEAD_DIM
    nh = B_HEADS
    hw = nh * dk
    gw = MXU_TILE
    tt = cps * c

    @pl.when(pl.program_id(1) == 0)
    def _():
        xp_ref[0:ROW_PAD, :] = buf_ref[...]
        s_ref[...] = s0_ref[...]

    x_in = qkv_ref[...]
    xp_ref[ROW_PAD:ROW_PAD + tt, :] = x_in
    xc = x_in * cw_ref[3:4, :]
    for j in range(3):
        xc = xc + xp_ref[ROW_PAD - 3 + j:ROW_PAD - 3 + j + tt, :] * cw_ref[j:j + 1, :]
    xc = _silu(xc)
    last = x_in[tt - ROW_PAD:tt, :]
    xp_ref[0:ROW_PAD, :] = last
    nbuf_ref[...] = last

    def unit(x, scale):
        return x * (lax.rsqrt(jnp.sum(x * x, axis=-1, keepdims=True) + 1e-6) * scale)

    q4 = jnp.concatenate([unit(xc[:, h * dk:(h + 1) * dk], dk ** -0.5) for h in range(nh)], axis=1)
    k4 = jnp.concatenate([unit(xc[:, hw + h * dk:hw + (h + 1) * dk], 1.0) for h in range(nh)], axis=1)
    v4 = xc[:, 2 * hw:3 * hw]
    gs = gs_ref[...]
    beta_all = _sigmoid(gs)
    g_all = -jnp.exp(alog_ref[...]) * _softplus(gs + dtb_ref[...])

    def per_head(x, first):
        return jnp.concatenate([jnp.broadcast_to(x[:, first + h:first + h + 1], (x.shape[0], dk))
                                for h in range(nh)], axis=1)

    ri = _iota2((c, c), 0)
    ci = _iota2((c, c), 1)
    tri = (ri >= ci).astype(BF16)
    upper = (ri > ci).astype(F32)
    row_i = _iota2((c, gw), 0)
    lane_j = _iota2((c, gw), 1) & (c - 1)
    lane_head = _blk(_iota2((c, gw), 1), c)
    low_i = row_i >= lane_j
    low_s = row_i > lane_j
    bd_tok = _bd_mask((gw, gw), c, c)
    bd_kt = _bd_mask((gw, hw), c, dk)
    bd_vn = _bd_mask((B_PAIR * c, gw), c, dk)
    bd_st = _bd_mask((gw, gw), dk, dk)

    pre = []
    for cc in range(cps):
        rs = slice(cc * c, (cc + 1) * c)
        g_c = g_all[rs]
        gmat = _mm_exact_lhs(tri, jnp.concatenate(
            [g_c[:, nh + h:nh + h + 1] * upper for h in range(nh)] + [g_c], axis=1))
        decay = jnp.where(low_i, jnp.exp(gmat[:, 0:gw]), 0.0)
        gc_all = gmat[:, gw:gw + 128]
        gc4 = per_head(gc_all, nh)
        gl4 = per_head(gc_all[c - 1:c, :], nh)
        egc4 = jnp.exp(gc4)
        beta4 = per_head(beta_all[rs], 0)
        kc = k4[rs]
        qc = q4[rs]
        kb = kc * beta4
        gkq = _mm(jnp.concatenate([kb, qc], axis=0), _bd_rows(kc, bd_kt), trans_b=True)
        pre.append(dict(lmat=jnp.where(low_s, gkq[0:c] * decay, 0.0), qk=gkq[c:2 * c] * decay,
                        vb=v4[rs] * beta4, kbe=kb * egc4, q_in=qc * egc4,
                        k_out_t=jnp.transpose(kc * jnp.exp(gl4 - gc4)), egl=jnp.exp(gl4)))
    tmats = _tri_inv_cat([-p["lmat"] for p in pre], bd_tok)
    for p, tmat in zip(pre, tmats):
        us, ws = [], []
        for h in range(nh):
            hs = slice(h * dk, (h + 1) * dk)
            rhs = jnp.concatenate([p["vb"][:, hs], p["kbe"][:, hs]], axis=1)
            uw = _mm(jnp.where(lane_head == h, tmat, 0.0), jnp.concatenate([rhs] * nh, axis=0))
            us.append(uw[:, 0:dk])
            ws.append(uw[:, dk:2 * dk])
        p["u"] = jnp.concatenate(us, axis=1)
        p["w"] = jnp.concatenate(ws, axis=1)

    state = [s_ref[i] for i in range(nh // B_PAIR)]
    for cc, p in enumerate(pre):
        rs = slice(cc * c, (cc + 1) * c)
        outs = []
        for i in range(nh // B_PAIR):
            ps = slice(i * gw, (i + 1) * gw)
            ws = _mm(jnp.concatenate([p["w"][:, ps], p["q_in"][:, ps]], axis=0), state[i])
            v_new = p["u"][:, ps] - ws[0:c]
            qk = p["qk"][:, i * B_PAIR * c:(i + 1) * B_PAIR * c]
            outs.append(ws[c:2 * c] + _mm(qk, _bd_rows(v_new, bd_vn)))
            state[i] = state[i] * p["egl"][:, ps] + jnp.where(bd_st, _mm(p["k_out_t"][ps, :], v_new), 0.0)
        o = jnp.concatenate(outs, axis=1)
        for h in range(nh):
            hs = slice(h * dk, (h + 1) * dk)
            out_ref[rs, hs] = _rms_norm(o[:, hs], ng_ref[...]) * _silu(z_ref[rs, hs])
    for i in range(nh // B_PAIR):
        s_ref[i] = state[i]
        sl_ref[i] = state[i]


def _gdn(qkv, z, gs, buf8, s0, cw, alog, dtb, ng, cps):
    b, l, _ = qkv.shape
    tt = cps * CHUNK
    w3 = 3 * B_WIDTH
    const = lambda shape: pl.BlockSpec(shape, lambda bi, i: (0,) * len(shape))
    st = (B_HEADS // B_PAIR, MXU_TILE, MXU_TILE)
    return pl.pallas_call(
        functools.partial(_gdn_kernel, cps=cps),
        grid=(b, l // tt),
        in_specs=[pl.BlockSpec((None, tt, w3), lambda bi, i: (bi, i, 0)),
                  pl.BlockSpec((None, tt, B_WIDTH), lambda bi, i: (bi, i, 0)),
                  pl.BlockSpec((None, tt, 128), lambda bi, i: (bi, i, 0)),
                  pl.BlockSpec((None, ROW_PAD, w3), lambda bi, i: (bi, 0, 0)),
                  pl.BlockSpec((None,) + st, lambda bi, i: (bi, 0, 0, 0)),
                  const((4, w3)), const((1, 128)), const((1, 128)), const((1, B_HEAD_DIM))],
        out_specs=[pl.BlockSpec((None, tt, B_WIDTH), lambda bi, i: (bi, i, 0)),
                   pl.BlockSpec((None, ROW_PAD, w3), lambda bi, i: (bi, 0, 0)),
                   pl.BlockSpec((None,) + st, lambda bi, i: (bi, 0, 0, 0))],
        out_shape=[jax.ShapeDtypeStruct((b, l, B_WIDTH), F32),
                   jax.ShapeDtypeStruct((b, ROW_PAD, w3), F32),
                   jax.ShapeDtypeStruct((b,) + st, F32)],
        scratch_shapes=[pltpu.VMEM((ROW_PAD + tt, w3), F32), pltpu.VMEM(st, F32)],
        compiler_params=_params(("parallel", "arbitrary")),
        name="gdn",
    )(qkv, z, gs, buf8, s0, cw, alog, dtb, ng)


def _ffn_kernel(ma_ref, mb_ref, x_ref, pe_ref, buf_ref, wo_ref, g0_ref, b0_ref, wup_ref, cw_ref, cb_ref,
                wdn_ref, g_ref, b_ref, wg_ref, bg_ref, wp_ref, o_ref, nbuf_ref, prev_ref, act_ref, *, tt):
    ff = D_FF
    nb = ff // FF_BLOCK
    half = ma_ref.shape[1]

    @pl.when(pl.program_id(1) == 0)
    def _():
        prev_ref[...] = buf_ref[...]

    mix = jnp.dot(ma_ref[...].astype(BF16), wo_ref[0:half, :], preferred_element_type=F32)
    mix = mix + jnp.dot(mb_ref[...].astype(BF16), wo_ref[half:2 * half, :], preferred_element_type=F32)
    x = _layer_norm(ALPHA * x_ref[...] + mix, g0_ref[...], b0_ref[...])
    xb = x.astype(BF16)
    row = _iota2((tt, FF_BLOCK), 0)

    def conv(col):
        sl = slice(col, col + FF_BLOCK)
        h = jnp.dot(xb, wup_ref[:, sl], preferred_element_type=F32)
        p1 = prev_ref[ROW_PAD - 1:ROW_PAD, sl]
        p2 = prev_ref[ROW_PAD - 2:ROW_PAD - 1, sl]
        h1 = jnp.where(row == 0, p1, pltpu.roll(h, 1, 0))
        h2 = jnp.where(row == 0, p2, jnp.where(row == 1, p1, pltpu.roll(h, 2, 0)))
        prev_ref[:, sl] = h[tt - ROW_PAD:tt, :]
        return h2 * cw_ref[0:1, sl] + h1 * cw_ref[1:2, sl] + h * cw_ref[2:3, sl] + cb_ref[:, sl]

    for j in range(nb):
        gate = conv(j * FF_BLOCK)
        val = conv(ff + j * FF_BLOCK)
        act_ref[:, j * FF_BLOCK:(j + 1) * FF_BLOCK] = (_silu(gate) * val).astype(BF16)
    nbuf_ref[...] = prev_ref[...]
    f = jnp.dot(act_ref[...], wdn_ref[...], preferred_element_type=F32)
    y = _layer_norm(ALPHA * x + f, g_ref[...], b_ref[...])
    gate = _sigmoid(jnp.dot(y.astype(BF16), wg_ref[...], preferred_element_type=F32) + bg_ref[...])
    emb = jnp.dot(pe_ref[...].astype(BF16), wp_ref[...], preferred_element_type=F32)
    o_ref[...] = y + gate * emb


def _ffn(ma, mb, x, pe, layer, buf8, wo, g0, b0, wup, cw, cb, wdn, g, bb, wg, bg, wp, tt):
    b, l, d = x.shape
    half = ma.shape[2]
    ff2 = 2 * D_FF
    const = lambda shape: pl.BlockSpec(shape, lambda bi, i: (0,) * len(shape),
                                       pipeline_mode=pl.Buffered(1))
    return pl.pallas_call(
        functools.partial(_ffn_kernel, tt=tt),
        grid=(b, l // tt),
        in_specs=[pl.BlockSpec((None, tt, half), lambda bi, i: (bi, i, 0)),
                  pl.BlockSpec((None, tt, half), lambda bi, i: (bi, i, 0)),
                  pl.BlockSpec((None, tt, d), lambda bi, i: (bi, i, 0)),
                  pl.BlockSpec((None, None, tt, PLE_DIM), lambda bi, i: (layer, bi, i, 0)),
                  pl.BlockSpec((None, ROW_PAD, ff2), lambda bi, i: (bi, 0, 0)),
                  const((2 * half, d)), const((1, d)), const((1, d)),
                  const((d, ff2)), const((3, ff2)), const((1, ff2)), const((D_FF, d)),
                  const((1, d)), const((1, d)), const((d, d)), const((1, d)), const((PLE_DIM, d))],
        out_specs=[pl.BlockSpec((None, tt, d), lambda bi, i: (bi, i, 0)),
                   pl.BlockSpec((None, ROW_PAD, ff2), lambda bi, i: (bi, 0, 0))],
        out_shape=[jax.ShapeDtypeStruct((b, l, d), F32),
                   jax.ShapeDtypeStruct((b, ROW_PAD, ff2), F32)],
        scratch_shapes=[pltpu.VMEM((ROW_PAD, ff2), F32), pltpu.VMEM((tt, D_FF), BF16)],
        compiler_params=_params(("parallel", "arbitrary")),
        name="ffn",
    )(ma, mb, x, pe, buf8, wo, g0, b0, wup, cw, cb, wdn, g, bb, wg, bg, wp)


def _rwkv_kernel(p_ref, sh_ref, s0_ref, mu_ref, w0_ref, w2_ref, a0_ref, a2_ref, g2_ref, kk_ref,
                 ka_ref, rk_ref, lg_ref, lb_ref, seg_ref,
                 out_ref, nsh_ref, sl_ref, xp_ref, s_ref, *, cps):
    c = CHUNK
    w = C_WIDTH
    n = C_HEAD_DIM
    gw = MXU_TILE
    ng = w // gw
    tt = cps * c

    @pl.when(pl.program_id(1) == 0)
    def _():
        xp_ref[0:ROW_PAD, :] = sh_ref[...]
        s_ref[...] = s0_ref[...]

    proj = p_ref[...]
    xp_ref[ROW_PAD:ROW_PAD + tt, :] = proj
    prev = xp_ref[ROW_PAD - 1:ROW_PAD - 1 + tt, :]
    last = proj[tt - ROW_PAD:tt, :]
    xp_ref[0:ROW_PAD, :] = last
    nsh_ref[...] = last
    xs = proj + (prev - proj) * mu_ref[...]

    seg = seg_ref[...]

    def head_sum(x):
        return jnp.concatenate([_mm_exact_rhs(x[:, i * gw:(i + 1) * gw], seg) for i in range(ng)], axis=1)

    r = xs[:, 0:w]
    k = xs[:, w:2 * w]
    v = xs[:, 2 * w:3 * w]
    lora_in = xs[:, 3 * w:3 * w + 128]
    ww = -_softplus(-(w0_ref[...] + _mm(jnp.tanh(lora_in), w2_ref[...]))) - 0.5
    a = _sigmoid(a0_ref[...] + _mm(lora_in, a2_ref[...]))
    g = _mm(_sigmoid(xs[:, 3 * w + 128:3 * w + 256]), g2_ref[...])
    kx = k * kk_ref[...]
    k = k * (1.0 + (a - 1.0) * ka_ref[...])
    sums = head_sum(jnp.concatenate([kx * kx, r * k * rk_ref[...]], axis=0))
    kk = kx * lax.rsqrt(sums[0:tt] + 1e-6)
    bonus = sums[tt:2 * tt] * v
    lw = -jnp.exp(ww)
    sa = -kk
    sb = kk * a

    ri = _iota2((c, c), 0)
    ci = _iota2((c, c), 1)
    tri = (ri >= ci).astype(BF16)
    row_i = _iota2((c, gw), 0)
    lane_j = _iota2((c, gw), 1) & (c - 1)
    low_i = row_i >= lane_j
    low_s = row_i > lane_j
    bd = _bd_mask((gw, gw), n, n)

    units = []
    for cc in range(cps):
        rs = slice(cc * c, (cc + 1) * c)
        lw_c = lw[rs]
        cum = _mm_exact_lhs(tri, lw_c)
        cum_last = cum[c - 1:c, :]
        e_out = jnp.exp(-cum)
        e_end = jnp.exp(cum_last - cum)
        a_t = sa[rs] * jnp.exp(cum - lw_c)
        r_t = r[rs] * jnp.exp(cum)
        b_t = sb[rs] * e_out
        k_t = k[rs] * e_out
        b_o = sb[rs] * e_end
        k_o = k[rs] * e_end
        e_last = jnp.exp(cum_last)
        for i in range(ng):
            gs = slice(i * gw, (i + 1) * gw)
            bk = jnp.concatenate([b_o[:, gs], k_o[:, gs]], axis=0)
            e_col = jnp.transpose(jnp.broadcast_to(e_last[:, gs], (2 * c, gw)))
            units.append(dict(cc=cc, i=i, lr=jnp.concatenate([a_t[:, gs], r_t[:, gs]], axis=0),
                              b_t=b_t[:, gs], k_t=k_t[:, gs], v=v[rs, gs], bk_t=jnp.transpose(bk),
                              e_last=jnp.concatenate([e_col] * (gw // (2 * c)), axis=1)))
    for un in units:
        g1 = _mm(un["lr"], _bd_rows(un["b_t"], bd), trans_b=True)
        g2 = _mm(un["lr"], _bd_rows(un["k_t"], bd), trans_b=True)
        un["m_ab"] = jnp.where(low_s, g1[0:c], 0.0)
        un["m_ak"] = jnp.where(low_s, g2[0:c], 0.0)
        un["m_r"] = jnp.concatenate([jnp.where(low_i, g1[c:2 * c], 0.0),
                                     jnp.where(low_i, g2[c:2 * c], 0.0)], axis=1)
        un["bdv"] = _bd_rows(un["v"], bd)
    for un, tmat in zip(units, _tri_inv_cat([un["m_ab"] for un in units], bd)):
        un["tmat"] = tmat
        un["akv"] = _mm(un["m_ak"], un["bdv"])

    state = [s_ref[i] for i in range(ng)]
    outs = [[None] * ng for _ in range(cps)]
    for un in units:
        i = un["i"]
        ls = _mm(un["lr"], state[i])
        u = _mm(un["tmat"], _bd_rows(ls[0:c] + un["akv"], bd))
        outs[un["cc"]][i] = ls[c:2 * c] + _mm(
            un["m_r"], jnp.concatenate([_bd_rows(u, bd), un["bdv"]], axis=0))
        uv = jnp.concatenate([u, un["v"]], axis=0)
        state[i] = state[i] * un["e_last"] + jnp.where(bd, _mm(un["bk_t"], uv), 0.0)
    for i in range(ng):
        s_ref[i] = state[i]
        sl_ref[i] = state[i]

    o = jnp.concatenate([jnp.concatenate(row, axis=1) for row in outs], axis=0)
    inv_n = 1.0 / n
    oc = o - head_sum(o) * inv_n
    var = head_sum(oc * oc) * inv_n
    on = oc * lax.rsqrt(var + C_LNX_EPS) * lg_ref[...] + lb_ref[...]
    out_ref[...] = (on + bonus) * g


def _rwkv(proj, sh8, s0, mu, w0, w2p, a0, a2p, g2, kk, ka, rk, lg, lb, seg, cps):
    b, l, cp = proj.shape
    tt = cps * CHUNK
    w = C_WIDTH
    const = lambda shape: pl.BlockSpec(shape, lambda bi, i: (0,) * len(shape))
    st = (C_HEADS // C_GROUP, MXU_TILE, MXU_TILE)
    return pl.pallas_call(
        functools.partial(_rwkv_kernel, cps=cps),
        grid=(b, l // tt),
        in_specs=[pl.BlockSpec((None, tt, cp), lambda bi, i: (bi, i, 0)),
                  pl.BlockSpec((None, ROW_PAD, cp), lambda bi, i: (bi, 0, 0)),
                  pl.BlockSpec((None,) + st, lambda bi, i: (bi, 0, 0, 0)),
                  const((1, cp)), const((1, w)), const((128, w)), const((1, w)), const((128, w)),
                  const((128, w)), const((1, w)), const((1, w)), const((1, w)), const((1, w)),
                  const((1, w)), const((MXU_TILE, MXU_TILE))],
        out_specs=[pl.BlockSpec((None, tt, w), lambda bi, i: (bi, i, 0)),
                   pl.BlockSpec((None, ROW_PAD, cp), lambda bi, i: (bi, 0, 0)),
                   pl.BlockSpec((None,) + st, lambda bi, i: (bi, 0, 0, 0))],
        out_shape=[jax.ShapeDtypeStruct((b, l, w), F32),
                   jax.ShapeDtypeStruct((b, ROW_PAD, cp), F32),
                   jax.ShapeDtypeStruct((b,) + st, F32)],
        scratch_shapes=[pltpu.VMEM((ROW_PAD + tt, cp), F32), pltpu.VMEM(st, F32)],
        compiler_params=_params(("parallel", "arbitrary")),
        name="rwkv7",
    )(proj, sh8, s0, mu, w0, w2p, a0, a2p, g2, kk, ka, rk, lg, lb, seg)


def _mla_prep_kernel(d_ref, ct_ref, st_ref, kc_ref, qg_ref, wq_ref, wqs_ref, kg_ref,
                     q_ref, c_ref, kr_ref):
    lq = D_Q_LORA
    lkv = D_KV_LORA
    qn = _rms_norm(d_ref[:, 0:lq], qg_ref[...]).astype(BF16)
    ct = ct_ref[...]
    st = st_ref[...]
    for h in range(D_HEADS):
        sl = slice(h * D_QK_PAD, (h + 1) * D_QK_PAD)
        q = jnp.dot(qn, wq_ref[:, sl], preferred_element_type=F32) * ct
        q = q + jnp.dot(qn, wqs_ref[:, sl], preferred_element_type=F32) * st
        q_ref[:, sl] = (q * MLA_SCALE).astype(BF16)
    c_ref[...] = _rms_norm(d_ref[:, lq:lq + lkv], kg_ref[...])
    o = lq + lkv
    kr_ref[...] = d_ref[:, o:o + D_QK_PAD] * kc_ref[...] + d_ref[:, o + D_QK_PAD:o + 2 * D_QK_PAD] * st


def _mla_prep(d, ctab, stab, kctab, qg, wq, wqs, kg, tm):
    b, l, dw = d.shape
    hq = D_HEADS * D_QK_PAD
    const = lambda shape: pl.BlockSpec(shape, lambda bi, i: (0,) * len(shape))
    tab = pl.BlockSpec((tm, D_QK_PAD), lambda bi, i: (i, 0))
    return pl.pallas_call(
        _mla_prep_kernel,
        grid=(b, l // tm),
        in_specs=[pl.BlockSpec((None, tm, dw), lambda bi, i: (bi, i, 0)), tab, tab, tab,
                  const((1, D_Q_LORA)), const((D_Q_LORA, hq)), const((D_Q_LORA, hq)),
                  const((1, D_KV_LORA))],
        out_specs=[pl.BlockSpec((None, tm, hq), lambda bi, i: (bi, i, 0)),
                   pl.BlockSpec((None, tm, D_KV_LORA), lambda bi, i: (bi, i, 0)),
                   pl.BlockSpec((None, tm, D_QK_PAD), lambda bi, i: (bi, i, 0))],
        out_shape=[jax.ShapeDtypeStruct((b, l, hq), BF16),
                   jax.ShapeDtypeStruct((b, l, D_KV_LORA), F32),
                   jax.ShapeDtypeStruct((b, l, D_QK_PAD), F32)],
        compiler_params=_params(("parallel", "parallel")),
        name="mla_prep",
    )(d, ctab, stab, kctab, qg, wq, wqs, kg)


def _kv_kernel(c_ref, kr_ref, wk_ref, wv_ref, k_ref, v_ref):
    cb = c_ref[...].astype(BF16)
    kr = kr_ref[...]
    ones = jnp.ones((cb.shape[0], D_V_PAD - D_V), BF16)
    for h in range(D_HEADS):
        sl = slice(h * D_QK_PAD, (h + 1) * D_QK_PAD)
        k_ref[:, sl] = (jnp.dot(cb, wk_ref[:, sl], preferred_element_type=F32) + kr).astype(BF16)
        vh = jnp.dot(cb, wv_ref[:, h * D_V:(h + 1) * D_V], preferred_element_type=F32).astype(BF16)
        v_ref[:, h * D_V_PAD:h * D_V_PAD + D_V] = vh
        v_ref[:, h * D_V_PAD + D_V:(h + 1) * D_V_PAD] = ones


def _kv(c, krp, wk, wv, tm):
    t = c.shape[0]
    hk = D_HEADS * D_QK_PAD
    hv = D_HEADS * D_V_PAD
    const = lambda shape: pl.BlockSpec(shape, lambda i: (0,) * len(shape))
    return pl.pallas_call(
        _kv_kernel,
        grid=(t // tm,),
        in_specs=[pl.BlockSpec((tm, D_KV_LORA), lambda i: (i, 0)),
                  pl.BlockSpec((tm, D_QK_PAD), lambda i: (i, 0)),
                  const((D_KV_LORA, hk)), const((D_KV_LORA, D_HEADS * D_V))],
        out_specs=[pl.BlockSpec((tm, hk), lambda i: (i, 0)), pl.BlockSpec((tm, hv), lambda i: (i, 0))],
        out_shape=[jax.ShapeDtypeStruct((t, hk), BF16), jax.ShapeDtypeStruct((t, hv), BF16)],
        compiler_params=_params(("parallel",)),
        name="mla_kv",
    )(c, krp, wk, wv)


def _attn_kernel(q_ref, k_ref, v_ref, o_ref, m_ref, acc_ref, *, tq, tk, nk, q_off):
    hps = ATTN_HEADS
    q_first = q_off + pl.program_id(2) * tq
    first_chunk_end = q_first // CHUNK * CHUNK + CHUNK
    last_chunk_end = (q_first + tq - 1) // CHUNK * CHUNK + CHUNK
    n_full = jnp.minimum(nk, first_chunk_end // tk)
    n_live = jnp.minimum(nk, (last_chunk_end + tk - 1) // tk)
    m_ref[...] = jnp.full(m_ref.shape, -jnp.inf, F32)
    acc_ref[...] = jnp.zeros(acc_ref.shape, F32)
    qs = [q_ref[:, h * D_QK_PAD:(h + 1) * D_QK_PAD] for h in range(hps)]

    def step(j, masked):
        start = pl.multiple_of(j * tk, tk)
        ss = [lax.dot_general(qs[h], k_ref[pl.ds(start, tk), h * D_QK_PAD:(h + 1) * D_QK_PAD],
                              _dims(False, True), preferred_element_type=F32) for h in range(hps)]
        if masked:
            q_chunk = jnp.right_shift(q_first + _iota2((tq, tk), 0), CHUNK_SHIFT)
            k_chunk = jnp.right_shift(start + _iota2((tq, tk), 1), CHUNK_SHIFT)
            ss = [jnp.where(k_chunk <= q_chunk, s, -jnp.inf) for s in ss]
        m_old = [m_ref[h] for h in range(hps)]
        m_new = [jnp.maximum(m_old[h], jnp.max(ss[h], axis=-1, keepdims=True)) for h in range(hps)]
        ps = [jnp.exp(ss[h] - m_new[h]).astype(BF16) for h in range(hps)]
        for h in range(hps):
            acc_ref[h] = jnp.exp(m_old[h] - m_new[h]) * acc_ref[h] + jnp.dot(
                ps[h], v_ref[pl.ds(start, tk), h * D_V_PAD:(h + 1) * D_V_PAD], preferred_element_type=F32)
            m_ref[h] = m_new[h]

    def full_body(j, carry):
        step(j, False)
        return carry

    def edge_body(j, carry):
        step(j, True)
        return carry

    lax.fori_loop(0, n_full, full_body, 0)
    lax.fori_loop(n_full, n_live, edge_body, 0)
    for h in range(hps):
        acc = acc_ref[h]
        o_ref[:, h * D_V:(h + 1) * D_V] = acc[:, 0:D_V] / acc[:, D_V:D_V + 1]


def _attention(q, k, v, tq, tk, q_off):
    b, l, _ = q.shape
    lk = k.shape[1]
    hps = ATTN_HEADS
    return pl.pallas_call(
        functools.partial(_attn_kernel, tq=tq, tk=tk, nk=lk // tk, q_off=q_off),
        grid=(b, D_HEADS // hps, l // tq),
        in_specs=[pl.BlockSpec((None, tq, hps * D_QK_PAD), lambda bi, h, i: (bi, i, h)),
                  pl.BlockSpec((None, lk, hps * D_QK_PAD), lambda bi, h, i: (bi, 0, h)),
                  pl.BlockSpec((None, lk, hps * D_V_PAD), lambda bi, h, i: (bi, 0, h))],
        out_specs=pl.BlockSpec((None, tq, hps * D_V), lambda bi, h, i: (bi, i, h)),
        out_shape=jax.ShapeDtypeStruct((b, l, D_HEADS * D_V), F32),
        scratch_shapes=[pltpu.VMEM((hps, tq, 1), F32), pltpu.VMEM((hps, tq, D_V_PAD), F32)],
        compiler_params=_params(("parallel", "parallel", "arbitrary")),
        name="mla_attention",
    )(q, k, v)


def _pad_rows_front(buf):
    return jnp.pad(buf, ((0, 0), (ROW_PAD - buf.shape[1], 0), (0, 0)))


def _block_diag(w):
    nb, bi, bj = w.shape
    eye = jnp.eye(nb, dtype=w.dtype)
    return (eye[:, None, :, None] * w[:, :, None, :]).reshape(nb * bi, nb * bj)


def _state_to_block_diag(s, per):
    b, h, n, m = s.shape
    eye = jnp.eye(per, dtype=s.dtype)
    s6 = s.reshape(b, h // per, per, n, 1, m) * eye[None, None, :, None, :, None]
    return s6.reshape(b, h // per, per * n, per * m)


def _state_from_block_diag(sbd, per):
    b, g, pn, pm = sbd.shape
    n, m = pn // per, pm // per
    s6 = sbd.reshape(b, g, per, n, per, m)
    return jnp.stack([s6[:, :, p, :, p, :] for p in range(per)], axis=2).reshape(b, g * per, n, m)


def _rope_swap(w):
    half = w.shape[-1] // 2
    return jnp.concatenate([-w[..., half:], w[..., :half]], axis=-1)


def _prepare(w_in0, a_w_r, a_w_i, a_b_r, a_b_i, w_in1, c_w2, c_a2, d_w_qb, d_w_kvb):
    p = {}
    o = 0
    p["w_a"] = w_in0[:, o:o + 2 * A_WIDTH].astype(BF16); o += 2 * A_WIDTH
    p["w_qkv"] = w_in0[:, o:o + 3 * B_WIDTH].astype(BF16); o += 3 * B_WIDTH
    p["w_z"] = w_in0[:, o:o + B_WIDTH].astype(BF16); o += B_WIDTH
    p["w_gs"] = jnp.pad(w_in0[:, o:o + 2 * B_HEADS], ((0, 0), (0, 128 - 2 * B_HEADS))).astype(BF16)
    p["a_wg"] = jnp.concatenate([_block_diag(a_w_r), _block_diag(a_w_i)], axis=1).astype(BF16)
    p["a_bg"] = jnp.concatenate([a_b_r, a_b_i])[None, :]
    p["w_c"] = w_in1[:, 0:C_PROJ].astype(BF16)
    o = C_PROJ
    w_qa = w_in1[:, o:o + D_Q_LORA]; o += D_Q_LORA
    w_craw = w_in1[:, o:o + D_KV_LORA]; o += D_KV_LORA
    w_kr = w_in1[:, o:o + D_ROPE]
    place = lambda w: jnp.pad(w, ((0, 0), (D_NOPE, D_QK_PAD - D_NOPE - D_ROPE)))
    p["w_d"] = jnp.concatenate([w_qa, w_craw, place(w_kr), place(_rope_swap(w_kr))], axis=1).astype(BF16)
    p["c_w2p"] = jnp.pad(c_w2, ((0, 64), (0, 0)))
    p["c_a2p"] = jnp.pad(c_a2, ((64, 0), (0, 0)))
    wq = d_w_qb.reshape(D_Q_LORA, D_HEADS, D_NOPE + D_ROPE)
    zpad = jnp.zeros((D_Q_LORA, D_HEADS, D_QK_PAD - D_NOPE - D_ROPE), F32)
    p["wq"] = jnp.concatenate([wq, zpad], axis=-1).reshape(D_Q_LORA, D_HEADS * D_QK_PAD).astype(BF16)
    wq_sw = jnp.concatenate([jnp.zeros((D_Q_LORA, D_HEADS, D_NOPE), F32), _rope_swap(wq[..., D_NOPE:]), zpad], axis=-1)
    p["wqs"] = wq_sw.reshape(D_Q_LORA, D_HEADS * D_QK_PAD).astype(BF16)
    wkv = d_w_kvb.reshape(D_KV_LORA, D_HEADS, D_NOPE + D_V)
    wk = jnp.pad(wkv[..., :D_NOPE], ((0, 0), (0, 0), (0, D_QK_PAD - D_NOPE)))
    p["wk"] = wk.reshape(D_KV_LORA, D_HEADS * D_QK_PAD).astype(BF16)
    p["wv"] = wkv[..., D_NOPE:].reshape(D_KV_LORA, D_HEADS * D_V).astype(BF16)
    lane = jnp.arange(MXU_TILE) // C_HEAD_DIM
    p["seg"] = (lane[:, None] == lane[None, :]).astype(BF16)
    return p


def _rope_tables(pos):
    half = D_ROPE // 2
    inv = ROPE_THETA ** (-jnp.arange(half, dtype=F32) / half)
    ang = pos.astype(F32)[:, None] * inv[None, :]
    cos = jnp.cos(ang)
    sin = jnp.sin(ang)
    n = pos.shape[0]
    tail = jnp.zeros((n, D_QK_PAD - D_NOPE - D_ROPE), F32)
    cos2 = jnp.concatenate([cos, cos], axis=1)
    sin2 = jnp.concatenate([sin, sin], axis=1)
    q_cos = jnp.concatenate([jnp.ones((n, D_NOPE), F32), cos2, tail], axis=1)
    q_sin = jnp.concatenate([jnp.zeros((n, D_NOPE), F32), sin2, tail], axis=1)
    k_cos = jnp.concatenate([jnp.zeros((n, D_NOPE), F32), cos2, tail], axis=1)
    return q_cos, q_sin, k_cos


def _trunk(x, pe, pos, a_conv, a_h, b_conv, b_s, c_shift, c_s, d_ckv, d_krope, f_conv, wts, prm):
    (a_conv_w, a_conv_b, a_lambda, b_conv_w, b_a_log, b_dt_bias, b_norm_g, w_out0, c_mu, c_w0,
     c_a0, c_g2, c_k_k, c_k_a, c_r_k, c_lnx_g, c_lnx_b, d_qa_g, d_kva_g, w_out1, ln_mix_g, ln_mix_b,
     ffn_w_up, ffn_conv_w, ffn_conv_b, ffn_w_down, ln_ffn_g, ln_ffn_b, ple_w_gate, ple_b_gate,
     ple_w_proj) = wts
    b, l, d = x.shape
    t = b * l
    tm = min(512, t)
    tt = min(512, l)
    cps = 2 if l % (2 * CHUNK) == 0 else 1
    row = lambda vec: vec.reshape(1, -1)
    f_new = []

    def ffn_block(i, mix_a, mix_b, xin, w_out):
        y, nbuf = _ffn(mix_a, mix_b, xin.reshape(b, l, d), pe, i, _pad_rows_front(f_conv[i]),
                       w_out.astype(BF16), row(ln_mix_g[i]), row(ln_mix_b[i]),
                       ffn_w_up[i].astype(BF16), ffn_conv_w[i], row(ffn_conv_b[i]),
                       ffn_w_down[i].astype(BF16), row(ln_ffn_g[i]), row(ln_ffn_b[i]),
                       ple_w_gate[i].astype(BF16), row(ple_b_gate[i]), ple_w_proj[i].astype(BF16), tt)
        f_new.append(nbuf[:, ROW_PAD - 2:, :])
        return y.reshape(t, d)

    x2 = x.reshape(t, d)
    a2, qkv, z, gs = _proj(x2, [prm["w_a"], prm["w_qkv"], prm["w_z"], prm["w_gs"]], tm)
    a_out, a_buf, a_hl = _rglru(a2.reshape(b, l, -1), _pad_rows_front(a_conv), a_h[:, None, :],
                                a_conv_w, row(a_conv_b), prm["a_wg"], prm["a_bg"], row(a_lambda),
                                min(256, l))
    pad8 = lambda vec: jnp.pad(vec, (B_HEADS, 128 - 2 * B_HEADS))[None, :]
    b_out, b_buf, b_sl = _gdn(qkv.reshape(b, l, -1), z.reshape(b, l, -1), gs.reshape(b, l, -1),
                              _pad_rows_front(b_conv), _state_to_block_diag(b_s, B_PAIR), b_conv_w,
                              pad8(b_a_log), pad8(b_dt_bias), row(b_norm_g), cps)
    x2 = ffn_block(0, a_out, b_out, x2, w_out0)

    c_in, d_in = _proj(x2, [prm["w_c"], prm["w_d"]], tm)
    c_out, c_sh, c_sl = _rwkv(c_in.reshape(b, l, -1), _pad_rows_front(c_shift),
                              _state_to_block_diag(jnp.swapaxes(c_s, 2, 3), C_GROUP), row(c_mu), row(c_w0),
                              prm["c_w2p"], row(c_a0), prm["c_a2p"], c_g2, row(c_k_k), row(c_k_a),
                              row(c_r_k.reshape(-1)), row(c_lnx_g), row(c_lnx_b), prm["seg"], cps)
    q_cos, q_sin, k_cos = _rope_tables(pos)
    q, c_new, kr_pad = _mla_prep(d_in.reshape(b, l, -1), q_cos, q_sin, k_cos, row(d_qa_g), prm["wq"],
                                 prm["wqs"], row(d_kva_g), min(512, l))
    past = d_ckv.shape[1]
    if past:
        c_all = jnp.concatenate([d_ckv, c_new], axis=1)
        kr_old = jnp.pad(d_krope, ((0, 0), (0, 0), (D_NOPE, D_QK_PAD - D_NOPE - D_ROPE)))
        kr_all = jnp.concatenate([kr_old, kr_pad], axis=1)
    else:
        c_all, kr_all = c_new, kr_pad
    lk = past + l
    tkv = b * lk
    k, v = _kv(c_all.reshape(tkv, -1), kr_all.reshape(tkv, -1), prm["wk"], prm["wv"],
               512 if tkv % 512 == 0 else lk)
    tq = min(ATTN_TQ, l)
    tk = ATTN_TK if lk % ATTN_TK == 0 else lk
    d_out = _attention(q, k.reshape(b, lk, -1), v.reshape(b, lk, -1), tq, tk, past)
    x2 = ffn_block(1, c_out, d_out, x2, w_out1)

    return (x2.reshape(b, l, d), a_buf[:, ROW_PAD - 3:, :], a_hl[:, 0, :], b_buf[:, ROW_PAD - 3:, :],
            _state_from_block_diag(b_sl, B_PAIR), c_sh[:, ROW_PAD - 1:, :],
            jnp.swapaxes(_state_from_block_diag(c_sl, C_GROUP), 2, 3), c_new,
            kr_pad[:, :, D_NOPE:D_NOPE + D_ROPE],
            jnp.stack(f_new))


def kernel(x_prompt, x_sample, state_a_conv, state_a_h, state_b_conv, state_b_s, state_c_shift, state_c_s, cache_d_ckv, cache_d_krope, state_ffn_conv, p_prompt, p_sample, w_in0, a_conv_w, a_conv_b, a_w_r, a_b_r, a_w_i, a_b_i, a_lambda, b_conv_w, b_a_log, b_dt_bias, b_norm_g, w_out0, w_in1, c_mu, c_w0, c_w2, c_a0, c_a2, c_g2, c_k_k, c_k_a, c_r_k, c_lnx_g, c_lnx_b, d_qa_g, d_w_qb, d_kva_g, d_w_kvb, w_out1, ln_mix_g, ln_mix_b, ffn_w_up, ffn_conv_w, ffn_conv_b, ffn_w_down, ln_ffn_g, ln_ffn_b, ple_w_gate, ple_b_gate, ple_w_proj):
    prm = _prepare(w_in0, a_w_r, a_w_i, a_b_r, a_b_i, w_in1, c_w2, c_a2, d_w_qb, d_w_kvb)
    wts = (a_conv_w, a_conv_b, a_lambda, b_conv_w, b_a_log, b_dt_bias, b_norm_g, w_out0, c_mu, c_w0,
           c_a0, c_g2, c_k_k, c_k_a, c_r_k, c_lnx_g, c_lnx_b, d_qa_g, d_kva_g, w_out1, ln_mix_g, ln_mix_b,
           ffn_w_up, ffn_conv_w, ffn_conv_b, ffn_w_down, ln_ffn_g, ln_ffn_b, ple_w_gate, ple_b_gate,
           ple_w_proj)
    dt = x_prompt.dtype
    bp, lp = x_prompt.shape[0], x_prompt.shape[1]
    zeros = lambda *shape: jnp.zeros(shape, dt)
    prompt = _trunk(x_prompt, p_prompt, jnp.arange(lp),
                    zeros(bp, 3, A_WIDTH), zeros(bp, A_WIDTH), zeros(bp, 3, 3 * B_WIDTH),
                    zeros(bp, B_HEADS, B_HEAD_DIM, B_HEAD_DIM), zeros(bp, 1, C_PROJ),
                    zeros(bp, C_HEADS, C_HEAD_DIM, C_HEAD_DIM), zeros(bp, 0, D_KV_LORA),
                    zeros(bp, 0, D_ROPE), zeros(DEPTH, bp, 2, 2 * D_FF), wts, prm)
    ls = x_sample.shape[1]
    past = cache_d_ckv.shape[1]
    sample = _trunk(x_sample, p_sample, past + jnp.arange(ls), state_a_conv, state_a_h, state_b_conv,
                    state_b_s, state_c_shift, state_c_s, cache_d_ckv, cache_d_krope, state_ffn_conv,
                    wts, prm)
    out = [prompt[0], sample[0]]
    for ps, ss in zip(prompt[1:], sample[1:]):
        out += [ps, ss]
    return tuple(out)
```

```python
import functools

import jax
import jax.numpy as jnp
from jax import lax
from jax.experimental import pallas as pl
from jax.experimental.pallas import tpu as pltpu

F32 = jnp.float32
BF16 = jnp.bfloat16

D_MODEL = 1024
DEPTH = 2
CHUNK = 64
CHUNK_SHIFT = CHUNK.bit_length() - 1
PLE_DIM = 256
ALPHA = (2 * DEPTH) ** 0.25
NORM_EPS = 1e-5
A_WIDTH = 512
A_BLOCKS = 8
A_C = 8.0
SQRT_FLOOR = 1e-30
B_HEADS = 4
B_HEAD_DIM = 128
B_WIDTH = 512
C_HEAD_DIM = 64
C_WIDTH = 512
C_HEADS = 8
C_LNX_EPS = 64e-5
C_PROJ = 3 * C_WIDTH + 64 + 64 + 128
D_HEADS = 4
D_Q_LORA = 384
D_KV_LORA = 256
D_NOPE = 128
D_ROPE = 64
D_V = 128
D_QK_PAD = 256
D_V_PAD = 256
MLA_SCALE = (D_NOPE + D_ROPE) ** -0.5
ROPE_THETA = 10000.0
D_FF = 2816
FF_BLOCK = 256
TRI_BASE = 8
MXU_TILE = 256
ROW_PAD = 8
VMEM_LIMIT = 56 * 1024 * 1024
ATTN_HEADS = 2
ATTN_TQ = 512
GDN_CPS = 8
RWKV_CPS = 4
ATTN_TK = 2048
B_PAIR = MXU_TILE // B_HEAD_DIM
C_GROUP = MXU_TILE // C_HEAD_DIM


def _params(sem):
    return pltpu.CompilerParams(dimension_semantics=sem, vmem_limit_bytes=VMEM_LIMIT)


def _sigmoid(x):
    return 1.0 / (1.0 + jnp.exp(-x))


def _softplus(x):
    return jnp.maximum(x, 0.0) + jnp.log(1.0 + jnp.exp(-jnp.abs(x)))


def _silu(x):
    return x * _sigmoid(x)


def _gelu_tanh(x):
    return 0.5 * x * (1.0 + jnp.tanh(0.7978845608028654 * (x + 0.044715 * x * x * x)))


def _dims(trans_a, trans_b):
    return (((0 if trans_a else 1,), (1 if trans_b else 0,)), ((), ()))


def _mm(a, b, trans_a=False, trans_b=False):
    return lax.dot_general(a.astype(BF16), b.astype(BF16), _dims(trans_a, trans_b),
                           preferred_element_type=F32)


def _split3(a):
    a1 = a.astype(BF16)
    r1 = a - a1.astype(F32)
    a2 = r1.astype(BF16)
    return a1, a2, (r1 - a2.astype(F32)).astype(BF16)


def _mm_exact_rhs(a, mask):
    a1, a2, a3 = _split3(a)
    mb = mask.astype(BF16)
    out = jnp.dot(a3, mb, preferred_element_type=F32)
    out = out + jnp.dot(a2, mb, preferred_element_type=F32)
    return out + jnp.dot(a1, mb, preferred_element_type=F32)


def _mm_exact_lhs(mask, b):
    b1, b2, b3 = _split3(b)
    mb = mask.astype(BF16)
    out = jnp.dot(mb, b3, preferred_element_type=F32)
    out = out + jnp.dot(mb, b2, preferred_element_type=F32)
    return out + jnp.dot(mb, b1, preferred_element_type=F32)


def _iota2(shape, axis):
    return lax.broadcasted_iota(jnp.int32, shape, axis)


def _blk(idx, size):
    return jnp.right_shift(idx, size.bit_length() - 1)


def _bd_mask(shape, row_block, col_block):
    return _blk(_iota2(shape, 0), row_block) == _blk(_iota2(shape, 1), col_block)


def _bd_rows(x, mask):
    reps = mask.shape[0] // x.shape[0]
    return jnp.where(mask, jnp.concatenate([x] * reps, axis=0), 0.0)


def _tri_inv_cat(xs, bd):
    c, wd = xs[0].shape
    row = _iota2((c, wd), 0)
    col = _iota2((c, wd), 1) & (c - 1)
    eye = (row == col).astype(F32)

    def prod(a, b):
        return jnp.dot(a.astype(BF16), _bd_rows(b, bd).astype(BF16), preferred_element_type=F32)

    base = _blk(row, TRI_BASE) == _blk(col, TRI_BASE)
    ps = [jnp.where(base, x, 0.0) for x in xs]
    ts = [eye + p for p in ps]
    k = 2
    while k < TRI_BASE:
        ps = [prod(p, p) for p in ps]
        ts = [t + prod(t, p) for t, p in zip(ts, ps)]
        k *= 2
    size = TRI_BASE
    while size < c:
        pair = (_blk(row, 2 * size) == _blk(col, 2 * size)) & (_blk(row, size) != _blk(col, size))
        offs = [prod(t, jnp.where(pair, x, 0.0)) for t, x in zip(ts, xs)]
        ts = [t + prod(o, t) for t, o in zip(ts, offs)]
        size *= 2
    return ts


def _layer_norm(x, g, b):
    mu = jnp.mean(x, axis=-1, keepdims=True)
    xc = x - mu
    var = jnp.mean(xc * xc, axis=-1, keepdims=True)
    return xc * lax.rsqrt(var + NORM_EPS) * g + b


def _rms_norm(x, g, eps=1e-6):
    return x * lax.rsqrt(jnp.mean(x * x, axis=-1, keepdims=True) + eps) * g


def _proj_kernel(x_ref, *refs):
    n = len(refs) // 2
    xb = x_ref[...].astype(BF16)
    for w_ref, o_ref in zip(refs[:n], refs[n:]):
        o_ref[...] = jnp.dot(xb, w_ref[...], preferred_element_type=F32)


def _proj(x, ws, tm):
    t, k = x.shape
    return pl.pallas_call(
        _proj_kernel,
        grid=(t // tm,),
        in_specs=[pl.BlockSpec((tm, k), lambda i: (i, 0))]
        + [pl.BlockSpec(w.shape, lambda i: (0, 0)) for w in ws],
        out_specs=[pl.BlockSpec((tm, w.shape[1]), lambda i: (i, 0)) for w in ws],
        out_shape=[jax.ShapeDtypeStruct((t, w.shape[1]), F32) for w in ws],
        compiler_params=_params(("parallel",)),
        name="proj",
    )(x, *ws)


def _scan_rows(a, b):
    n = a.shape[0]
    row = _iota2(a.shape, 0)
    s = 1
    while s < n:
        a_sh = pltpu.roll(a, s, 0)
        b_sh = pltpu.roll(b, s, 0)
        m = row >= s
        b = jnp.where(m, a * b_sh + b, b)
        a = jnp.where(m, a * a_sh, a)
        s *= 2
    return a, b


def _rglru_kernel(a2_ref, buf_ref, h0_ref, cw_ref, cb_ref, wg_ref, bg_ref, lam_ref,
                  out_ref, nbuf_ref, hl_ref, xp_ref, hc_ref, *, tt):
    w = A_WIDTH

    @pl.when(pl.program_id(1) == 0)
    def _():
        xp_ref[0:ROW_PAD, :] = buf_ref[...]
        hc_ref[...] = h0_ref[...]

    gate_in = a2_ref[:, 0:w]
    x_in = a2_ref[:, w:2 * w]
    xp_ref[ROW_PAD:ROW_PAD + tt, :] = x_in
    xc = x_in * cw_ref[3:4, :] + cb_ref[...]
    for j in range(3):
        xc = xc + xp_ref[ROW_PAD - 3 + j:ROW_PAD - 3 + j + tt, :] * cw_ref[j:j + 1, :]
    gates = jnp.dot(xc.astype(BF16), wg_ref[...], preferred_element_type=F32) + bg_ref[...]
    r = _sigmoid(gates[:, 0:w])
    ig = _sigmoid(gates[:, w:2 * w])
    log_a = (-A_C) * r * _softplus(-lam_ref[...])
    a = jnp.exp(log_a)
    y = 1.0 - a * a
    u = y * lax.rsqrt(jnp.maximum(y, SQRT_FLOOR)) * (ig * xc)
    a_cum, h_loc = _scan_rows(a, u)
    h = a_cum * hc_ref[...] + h_loc
    out_ref[...] = h * _gelu_tanh(gate_in)
    hc_ref[...] = h[tt - 1:tt, :]
    hl_ref[...] = h[tt - 1:tt, :]
    last = x_in[tt - ROW_PAD:tt, :]
    xp_ref[0:ROW_PAD, :] = last
    nbuf_ref[...] = last


def _rglru(a2, buf8, h0, cw, cb, wg, bg, lam, tt):
    b, l, _ = a2.shape
    w = A_WIDTH
    const = lambda shape: pl.BlockSpec(shape, lambda bi, i: (0,) * len(shape))
    return pl.pallas_call(
        functools.partial(_rglru_kernel, tt=tt),
        grid=(b, l // tt),
        in_specs=[pl.BlockSpec((None, tt, 2 * w), lambda bi, i: (bi, i, 0)),
                  pl.BlockSpec((None, ROW_PAD, w), lambda bi, i: (bi, 0, 0)),
                  pl.BlockSpec((None, 1, w), lambda bi, i: (bi, 0, 0)),
                  const((4, w)), const((1, w)), const((w, 2 * w)), const((1, 2 * w)), const((1, w))],
        out_specs=[pl.BlockSpec((None, tt, w), lambda bi, i: (bi, i, 0)),
                   pl.BlockSpec((None, ROW_PAD, w), lambda bi, i: (bi, 0, 0)),
                   pl.BlockSpec((None, 1, w), lambda bi, i: (bi, 0, 0))],
        out_shape=[jax.ShapeDtypeStruct((b, l, w), F32),
                   jax.ShapeDtypeStruct((b, ROW_PAD, w), F32),
                   jax.ShapeDtypeStruct((b, 1, w), F32)],
        scratch_shapes=[pltpu.VMEM((ROW_PAD + tt, w), F32), pltpu.VMEM((1, w), F32)],
        compiler_params=_params(("parallel", "arbitrary")),
        name="rglru",
    )(a2, buf8, h0, cw, cb, wg, bg, lam)


def _gdn_kernel(qkv_ref, z_ref, gs_ref, buf_ref, s0_ref, cw_ref, alog_ref, dtb_ref, ng_ref,
                out_ref, nbuf_ref, sl_ref, xp_ref, s_ref, *, cps):
    c = CHUNK
    dk = B_HEAD_DIM
    nh = B_HEADS
    hw = nh * dk
    gw = MXU_TILE
    tt = cps * c

    @pl.when(pl.program_id(1) == 0)
    def _():
        xp_ref[0:ROW_PAD, :] = buf_ref[...]
        s_ref[...] = s0_ref[...]

    x_in = qkv_ref[...]
    xp_ref[ROW_PAD:ROW_PAD + tt, :] = x_in
    xc = x_in * cw_ref[3:4, :]
    for j in range(3):
        xc = xc + xp_ref[ROW_PAD - 3 + j:ROW_PAD - 3 + j + tt, :] * cw_ref[j:j + 1, :]
    xc = _silu(xc)
    last = x_in[tt - ROW_PAD:tt, :]
    xp_ref[0:ROW_PAD, :] = last
    nbuf_ref[...] = last

    def unit(x, scale):
        return x * (lax.rsqrt(jnp.sum(x * x, axis=-1, keepdims=True) + 1e-6) * scale)

    q4 = jnp.concatenate([unit(xc[:, h * dk:(h + 1) * dk], dk ** -0.5) for h in range(nh)], axis=1)
    k4 = jnp.concatenate([unit(xc[:, hw + h * dk:hw + (h + 1) * dk], 1.0) for h in range(nh)], axis=1)
    v4 = xc[:, 2 * hw:3 * hw]
    gs = gs_ref[...]
    beta_all = _sigmoid(gs)
    g_all = -jnp.exp(alog_ref[...]) * _softplus(gs + dtb_ref[...])

    def per_head(x, first):
        return jnp.concatenate([jnp.broadcast_to(x[:, first + h:first + h + 1], (x.shape[0], dk))
                                for h in range(nh)], axis=1)

    ri = _iota2((c, c), 0)
    ci = _iota2((c, c), 1)
    tri = (ri >= ci).astype(BF16)
    upper = (ri > ci).astype(F32)
    row_i = _iota2((c, gw), 0)
    lane_j = _iota2((c, gw), 1) & (c - 1)
    lane_head = _blk(_iota2((c, gw), 1), c)
    low_i = row_i >= lane_j
    low_s = row_i > lane_j
    bd_tok = _bd_mask((gw, gw), c, c)
    bd_kt = _bd_mask((gw, hw), c, dk)
    bd_vn = _bd_mask((B_PAIR * c, gw), c, dk)
    bd_st = _bd_mask((gw, gw), dk, dk)

    pre = []
    for cc in range(cps):
        rs = slice(cc * c, (cc + 1) * c)
        g_c = g_all[rs]
        gmat = _mm_exact_lhs(tri, jnp.concatenate(
            [g_c[:, nh + h:nh + h + 1] * upper for h in range(nh)] + [g_c], axis=1))
        decay = jnp.where(low_i, jnp.exp(gmat[:, 0:gw]), 0.0)
        gc_all = gmat[:, gw:gw + 128]
        gc4 = per_head(gc_all, nh)
        gl4 = per_head(gc_all[c - 1:c, :], nh)
        egc4 = jnp.exp(gc4)
        beta4 = per_head(beta_all[rs], 0)
        kc = k4[rs]
        qc = q4[rs]
        kb = kc * beta4
        gkq = _mm(jnp.concatenate([kb, qc], axis=0), _bd_rows(kc, bd_kt), trans_b=True)
        pre.append(dict(lmat=jnp.where(low_s, gkq[0:c] * decay, 0.0), qk=gkq[c:2 * c] * decay,
                        vb=v4[rs] * beta4, kbe=kb * egc4, q_in=qc * egc4,
                        k_out_t=jnp.transpose(kc * jnp.exp(gl4 - gc4)), egl=jnp.exp(gl4)))
    tmats = _tri_inv_cat([-p["lmat"] for p in pre], bd_tok)
    for p, tmat in zip(pre, tmats):
        us, ws = [], []
        for h in range(nh):
            hs = slice(h * dk, (h + 1) * dk)
            rhs = jnp.concatenate([p["vb"][:, hs], p["kbe"][:, hs]], axis=1)
            uw = _mm(jnp.where(lane_head == h, tmat, 0.0), jnp.concatenate([rhs] * nh, axis=0))
            us.append(uw[:, 0:dk])
            ws.append(uw[:, dk:2 * dk])
        p["u"] = jnp.concatenate(us, axis=1)
        p["w"] = jnp.concatenate(ws, axis=1)

    state = [s_ref[i] for i in range(nh // B_PAIR)]
    for cc, p in enumerate(pre):
        rs = slice(cc * c, (cc + 1) * c)
        outs = []
        for i in range(nh // B_PAIR):
            ps = slice(i * gw, (i + 1) * gw)
            ws = _mm(jnp.concatenate([p["w"][:, ps], p["q_in"][:, ps]], axis=0), state[i])
            v_new = p["u"][:, ps] - ws[0:c]
            qk = p["qk"][:, i * B_PAIR * c:(i + 1) * B_PAIR * c]
            outs.append(ws[c:2 * c] + _mm(qk, _bd_rows(v_new, bd_vn)))
            state[i] = state[i] * p["egl"][:, ps] + jnp.where(bd_st, _mm(p["k_out_t"][ps, :], v_new), 0.0)
        o = jnp.concatenate(outs, axis=1)
        for h in range(nh):
            hs = slice(h * dk, (h + 1) * dk)
            out_ref[rs, hs] = _rms_norm(o[:, hs], ng_ref[...]) * _silu(z_ref[rs, hs])
    for i in range(nh // B_PAIR):
        s_ref[i] = state[i]
        sl_ref[i] = state[i]


def _gdn(qkv, z, gs, buf8, s0, cw, alog, dtb, ng, cps):
    b, l, _ = qkv.shape
    tt = cps * CHUNK
    w3 = 3 * B_WIDTH
    const = lambda shape: pl.BlockSpec(shape, lambda bi, i: (0,) * len(shape))
    st = (B_HEADS // B_PAIR, MXU_TILE, MXU_TILE)
    return pl.pallas_call(
        functools.partial(_gdn_kernel, cps=cps),
        grid=(b, l // tt),
        in_specs=[pl.BlockSpec((None, tt, w3), lambda bi, i: (bi, i, 0)),
                  pl.BlockSpec((None, tt, B_WIDTH), lambda bi, i: (bi, i, 0)),
                  pl.BlockSpec((None, tt, 128), lambda bi, i: (bi, i, 0)),
                  pl.BlockSpec((None, ROW_PAD, w3), lambda bi, i: (bi, 0, 0)),
                  pl.BlockSpec((None,) + st, lambda bi, i: (bi, 0, 0, 0)),
                  const((4, w3)), const((1, 128)), const((1, 128)), const((1, B_HEAD_DIM))],
        out_specs=[pl.BlockSpec((None, tt, B_WIDTH), lambda bi, i: (bi, i, 0)),
                   pl.BlockSpec((None, ROW_PAD, w3), lambda bi, i: (bi, 0, 0)),
                   pl.BlockSpec((None,) + st, lambda bi, i: (bi, 0, 0, 0))],
        out_shape=[jax.ShapeDtypeStruct((b, l, B_WIDTH), F32),
                   jax.ShapeDtypeStruct((b, ROW_PAD, w3), F32),
                   jax.ShapeDtypeStruct((b,) + st, F32)],
        scratch_shapes=[pltpu.VMEM((ROW_PAD + tt, w3), F32), pltpu.VMEM(st, F32)],
        compiler_params=_params(("parallel", "arbitrary")),
        name="gdn",
    )(qkv, z, gs, buf8, s0, cw, alog, dtb, ng)


def _ffn_kernel(ma_ref, mb_ref, x_ref, pe_ref, buf_ref, wo_ref, g0_ref, b0_ref, wup_ref, cw_ref, cb_ref,
                wdn_ref, g_ref, b_ref, wg_ref, bg_ref, wp_ref, o_ref, nbuf_ref, prev_ref, act_ref, *, tt):
    ff = D_FF
    nb = ff // FF_BLOCK
    half = ma_ref.shape[1]

    @pl.when(pl.program_id(1) == 0)
    def _():
        prev_ref[...] = buf_ref[...]

    mix = jnp.dot(ma_ref[...].astype(BF16), wo_ref[0:half, :], preferred_element_type=F32)
    mix = mix + jnp.dot(mb_ref[...].astype(BF16), wo_ref[half:2 * half, :], preferred_element_type=F32)
    x = _layer_norm(ALPHA * x_ref[...] + mix, g0_ref[...], b0_ref[...])
    xb = x.astype(BF16)
    row = _iota2((tt, FF_BLOCK), 0)

    def conv(col):
        sl = slice(col, col + FF_BLOCK)
        h = jnp.dot(xb, wup_ref[:, sl], preferred_element_type=F32)
        p1 = prev_ref[ROW_PAD - 1:ROW_PAD, sl]
        p2 = prev_ref[ROW_PAD - 2:ROW_PAD - 1, sl]
        h1 = jnp.where(row == 0, p1, pltpu.roll(h, 1, 0))
        h2 = jnp.where(row == 0, p2, jnp.where(row == 1, p1, pltpu.roll(h, 2, 0)))
        prev_ref[:, sl] = h[tt - ROW_PAD:tt, :]
        return h2 * cw_ref[0:1, sl] + h1 * cw_ref[1:2, sl] + h * cw_ref[2:3, sl] + cb_ref[:, sl]

    for j in range(nb):
        gate = conv(j * FF_BLOCK)
        val = conv(ff + j * FF_BLOCK)
        act_ref[:, j * FF_BLOCK:(j + 1) * FF_BLOCK] = (_silu(gate) * val).astype(BF16)
    nbuf_ref[...] = prev_ref[...]
    f = jnp.dot(act_ref[...], wdn_ref[...], preferred_element_type=F32)
    y = _layer_norm(ALPHA * x + f, g_ref[...], b_ref[...])
    gate = _sigmoid(jnp.dot(y.astype(BF16), wg_ref[...], preferred_element_type=F32) + bg_ref[...])
    emb = jnp.dot(pe_ref[...].astype(BF16), wp_ref[...], preferred_element_type=F32)
    o_ref[...] = y + gate * emb


def _ffn(ma, mb, x, pe, layer, buf8, wo, g0, b0, wup, cw, cb, wdn, g, bb, wg, bg, wp, tt):
    b, l, d = x.shape
    half = ma.shape[2]
    ff2 = 2 * D_FF
    const = lambda shape: pl.BlockSpec(shape, lambda bi, i: (0,) * len(shape),
                                       pipeline_mode=pl.Buffered(1))
    return pl.pallas_call(
        functools.partial(_ffn_kernel, tt=tt),
        grid=(b, l // tt),
        in_specs=[pl.BlockSpec((None, tt, half), lambda bi, i: (bi, i, 0)),
                  pl.BlockSpec((None, tt, half), lambda bi, i: (bi, i, 0)),
                  pl.BlockSpec((None, tt, d), lambda bi, i: (bi, i, 0)),
                  pl.BlockSpec((None, None, tt, PLE_DIM), lambda bi, i: (layer, bi, i, 0)),
                  pl.BlockSpec((None, ROW_PAD, ff2), lambda bi, i: (bi, 0, 0)),
                  const((2 * half, d)), const((1, d)), const((1, d)),
                  const((d, ff2)), const((3, ff2)), const((1, ff2)), const((D_FF, d)),
                  const((1, d)), const((1, d)), const((d, d)), const((1, d)), const((PLE_DIM, d))],
        out_specs=[pl.BlockSpec((None, tt, d), lambda bi, i: (bi, i, 0)),
                   pl.BlockSpec((None, ROW_PAD, ff2), lambda bi, i: (bi, 0, 0))],
        out_shape=[jax.ShapeDtypeStruct((b, l, d), F32),
                   jax.ShapeDtypeStruct((b, ROW_PAD, ff2), F32)],
        scratch_shapes=[pltpu.VMEM((ROW_PAD, ff2), F32), pltpu.VMEM((tt, D_FF), BF16)],
        compiler_params=_params(("parallel", "arbitrary")),
        name="ffn",
    )(ma, mb, x, pe, buf8, wo, g0, b0, wup, cw, cb, wdn, g, bb, wg, bg, wp)


def _rwkv_kernel(p_ref, sh_ref, s0_ref, mu_ref, w0_ref, w2_ref, a0_ref, a2_ref, g2_ref, kk_ref,
                 ka_ref, rk_ref, lg_ref, lb_ref, seg_ref,
                 out_ref, nsh_ref, sl_ref, xp_ref, s_ref, *, cps):
    c = CHUNK
    w = C_WIDTH
    n = C_HEAD_DIM
    gw = MXU_TILE
    ng = w // gw
    tt = cps * c

    @pl.when(pl.program_id(1) == 0)
    def _():
        xp_ref[0:ROW_PAD, :] = sh_ref[...]
        s_ref[...] = s0_ref[...]

    proj = p_ref[...]
    xp_ref[ROW_PAD:ROW_PAD + tt, :] = proj
    prev = xp_ref[ROW_PAD - 1:ROW_PAD - 1 + tt, :]
    last = proj[tt - ROW_PAD:tt, :]
    xp_ref[0:ROW_PAD, :] = last
    nsh_ref[...] = last
    xs = proj + (prev - proj) * mu_ref[...]

    seg = seg_ref[...]

    def head_sum(x):
        return jnp.concatenate([_mm_exact_rhs(x[:, i * gw:(i + 1) * gw], seg) for i in range(ng)], axis=1)

    r = xs[:, 0:w]
    k = xs[:, w:2 * w]
    v = xs[:, 2 * w:3 * w]
    lora_in = xs[:, 3 * w:3 * w + 128]
    ww = -_softplus(-(w0_ref[...] + _mm(jnp.tanh(lora_in), w2_ref[...]))) - 0.5
    a = _sigmoid(a0_ref[...] + _mm(lora_in, a2_ref[...]))
    g = _mm(_sigmoid(xs[:, 3 * w + 128:3 * w + 256]), g2_ref[...])
    kx = k * kk_ref[...]
    k = k * (1.0 + (a - 1.0) * ka_ref[...])
    sums = head_sum(jnp.concatenate([kx * kx, r * k * rk_ref[...]], axis=0))
    kk = kx * lax.rsqrt(sums[0:tt] + 1e-6)
    bonus = sums[tt:2 * tt] * v
    lw = -jnp.exp(ww)
    sa = -kk
    sb = kk * a

    ri = _iota2((c, c), 0)
    ci = _iota2((c, c), 1)
    tri = (ri >= ci).astype(BF16)
    row_i = _iota2((c, gw), 0)
    lane_j = _iota2((c, gw), 1) & (c - 1)
    low_i = row_i >= lane_j
    low_s = row_i > lane_j
    bd = _bd_mask((gw, gw), n, n)

    units = []
    for cc in range(cps):
        rs = slice(cc * c, (cc + 1) * c)
        lw_c = lw[rs]
        cum = _mm_exact_lhs(tri, lw_c)
        cum_last = cum[c - 1:c, :]
        e_out = jnp.exp(-cum)
        e_end = jnp.exp(cum_last - cum)
        a_t = sa[rs] * jnp.exp(cum - lw_c)
        r_t = r[rs] * jnp.exp(cum)
        b_t = sb[rs] * e_out
        k_t = k[rs] * e_out
        b_o = sb[rs] * e_end
        k_o = k[rs] * e_end
        e_last = jnp.exp(cum_last)
        for i in range(ng):
            gs = slice(i * gw, (i + 1) * gw)
            bk = jnp.concatenate([b_o[:, gs], k_o[:, gs]], axis=0)
            e_col = jnp.transpose(jnp.broadcast_to(e_last[:, gs], (2 * c, gw)))
            units.append(dict(cc=cc, i=i, lr=jnp.concatenate([a_t[:, gs], r_t[:, gs]], axis=0),
                              b_t=b_t[:, gs], k_t=k_t[:, gs], v=v[rs, gs], bk_t=jnp.transpose(bk),
                              e_last=jnp.concatenate([e_col] * (gw // (2 * c)), axis=1)))
    for un in units:
        g1 = _mm(un["lr"], _bd_rows(un["b_t"], bd), trans_b=True)
        g2 = _mm(un["lr"], _bd_rows(un["k_t"], bd), trans_b=True)
        un["m_ab"] = jnp.where(low_s, g1[0:c], 0.0)
        un["m_ak"] = jnp.where(low_s, g2[0:c], 0.0)
        un["m_r"] = jnp.concatenate([jnp.where(low_i, g1[c:2 * c], 0.0),
                                     jnp.where(low_i, g2[c:2 * c], 0.0)], axis=1)
        un["bdv"] = _bd_rows(un["v"], bd)
    for un, tmat in zip(units, _tri_inv_cat([un["m_ab"] for un in units], bd)):
        un["tmat"] = tmat
        un["akv"] = _mm(un["m_ak"], un["bdv"])

    state = [s_ref[i] for i in range(ng)]
    outs = [[None] * ng for _ in range(cps)]
    for un in units:
        i = un["i"]
        ls = _mm(un["lr"], state[i])
        u = _mm(un["tmat"], _bd_rows(ls[0:c] + un["akv"], bd))
        outs[un["cc"]][i] = ls[c:2 * c] + _mm(
            un["m_r"], jnp.concatenate([_bd_rows(u, bd), un["bdv"]], axis=0))
        uv = jnp.concatenate([u, un["v"]], axis=0)
        state[i] = state[i] * un["e_last"] + jnp.where(bd, _mm(un["bk_t"], uv), 0.0)
    for i in range(ng):
        s_ref[i] = state[i]
        sl_ref[i] = state[i]

    o = jnp.concatenate([jnp.concatenate(row, axis=1) for row in outs], axis=0)
    inv_n = 1.0 / n
    oc = o - head_sum(o) * inv_n
    var = head_sum(oc * oc) * inv_n
    on = oc * lax.rsqrt(var + C_LNX_EPS) * lg_ref[...] + lb_ref[...]
    out_ref[...] = (on + bonus) * g


def _rwkv(proj, sh8, s0, mu, w0, w2p, a0, a2p, g2, kk, ka, rk, lg, lb, seg, cps):
    b, l, cp = proj.shape
    tt = cps * CHUNK
    w = C_WIDTH
    const = lambda shape: pl.BlockSpec(shape, lambda bi, i: (0,) * len(shape))
    st = (C_HEADS // C_GROUP, MXU_TILE, MXU_TILE)
    return pl.pallas_call(
        functools.partial(_rwkv_kernel, cps=cps),
        grid=(b, l // tt),
        in_specs=[pl.BlockSpec((None, tt, cp), lambda bi, i: (bi, i, 0)),
                  pl.BlockSpec((None, ROW_PAD, cp), lambda bi, i: (bi, 0, 0)),
                  pl.BlockSpec((None,) + st, lambda bi, i: (bi, 0, 0, 0)),
                  const((1, cp)), const((1, w)), const((128, w)), const((1, w)), const((128, w)),
                  const((128, w)), const((1, w)), const((1, w)), const((1, w)), const((1, w)),
                  const((1, w)), const((MXU_TILE, MXU_TILE))],
        out_specs=[pl.BlockSpec((None, tt, w), lambda bi, i: (bi, i, 0)),
                   pl.BlockSpec((None, ROW_PAD, cp), lambda bi, i: (bi, 0, 0)),
                   pl.BlockSpec((None,) + st, lambda bi, i: (bi, 0, 0, 0))],
        out_shape=[jax.ShapeDtypeStruct((b, l, w), F32),
                   jax.ShapeDtypeStruct((b, ROW_PAD, cp), F32),
                   jax.ShapeDtypeStruct((b,) + st, F32)],
        scratch_shapes=[pltpu.VMEM((ROW_PAD + tt, cp), F32), pltpu.VMEM(st, F32)],
        compiler_params=_params(("parallel", "arbitrary")),
        name="rwkv7",
    )(proj, sh8, s0, mu, w0, w2p, a0, a2p, g2, kk, ka, rk, lg, lb, seg)


def _mla_prep_kernel(d_ref, ct_ref, st_ref, kc_ref, qg_ref, wq_ref, wqs_ref, kg_ref,
                     q_ref, c_ref, kr_ref):
    lq = D_Q_LORA
    lkv = D_KV_LORA
    qn = _rms_norm(d_ref[:, 0:lq], qg_ref[...]).astype(BF16)
    ct = ct_ref[...]
    st = st_ref[...]
    for h in range(D_HEADS):
        sl = slice(h * D_QK_PAD, (h + 1) * D_QK_PAD)
        q = jnp.dot(qn, wq_ref[:, sl], preferred_element_type=F32) * ct
        q = q + jnp.dot(qn, wqs_ref[:, sl], preferred_element_type=F32) * st
        q_ref[:, sl] = (q * MLA_SCALE).astype(BF16)
    c_ref[...] = _rms_norm(d_ref[:, lq:lq + lkv], kg_ref[...])
    o = lq + lkv
    kr_ref[...] = d_ref[:, o:o + D_QK_PAD] * kc_ref[...] + d_ref[:, o + D_QK_PAD:o + 2 * D_QK_PAD] * st


def _mla_prep(d, ctab, stab, kctab, qg, wq, wqs, kg, tm):
    b, l, dw = d.shape
    hq = D_HEADS * D_QK_PAD
    const = lambda shape: pl.BlockSpec(shape, lambda bi, i: (0,) * len(shape))
    tab = pl.BlockSpec((tm, D_QK_PAD), lambda bi, i: (i, 0))
    return pl.pallas_call(
        _mla_prep_kernel,
        grid=(b, l // tm),
        in_specs=[pl.BlockSpec((None, tm, dw), lambda bi, i: (bi, i, 0)), tab, tab, tab,
                  const((1, D_Q_LORA)), const((D_Q_LORA, hq)), const((D_Q_LORA, hq)),
                  const((1, D_KV_LORA))],
        out_specs=[pl.BlockSpec((None, tm, hq), lambda bi, i: (bi, i, 0)),
                   pl.BlockSpec((None, tm, D_KV_LORA), lambda bi, i: (bi, i, 0)),
                   pl.BlockSpec((None, tm, D_QK_PAD), lambda bi, i: (bi, i, 0))],
        out_shape=[jax.ShapeDtypeStruct((b, l, hq), BF16),
                   jax.ShapeDtypeStruct((b, l, D_KV_LORA), F32),
                   jax.ShapeDtypeStruct((b, l, D_QK_PAD), F32)],
        compiler_params=_params(("parallel", "parallel")),
        name="mla_prep",
    )(d, ctab, stab, kctab, qg, wq, wqs, kg)


def _kv_kernel(c_ref, kr_ref, wk_ref, wv_ref, k_ref, v_ref):
    cb = c_ref[...].astype(BF16)
    kr = kr_ref[...]
    ones = jnp.ones((cb.shape[0], D_V_PAD - D_V), BF16)
    for h in range(D_HEADS):
        sl = slice(h * D_QK_PAD, (h + 1) * D_QK_PAD)
        k_ref[:, sl] = (jnp.dot(cb, wk_ref[:, sl], preferred_element_type=F32) + kr).astype(BF16)
        vh = jnp.dot(cb, wv_ref[:, h * D_V:(h + 1) * D_V], preferred_element_type=F32).astype(BF16)
        v_ref[:, h * D_V_PAD:h * D_V_PAD + D_V] = vh
        v_ref[:, h * D_V_PAD + D_V:(h + 1) * D_V_PAD] = ones


def _kv(c, krp, wk, wv, tm):
    t = c.shape[0]
    hk = D_HEADS * D_QK_PAD
    hv = D_HEADS * D_V_PAD
    const = lambda shape: pl.BlockSpec(shape, lambda i: (0,) * len(shape))
    return pl.pallas_call(
        _kv_kernel,
        grid=(t // tm,),
        in_specs=[pl.BlockSpec((tm, D_KV_LORA), lambda i: (i, 0)),
                  pl.BlockSpec((tm, D_QK_PAD), lambda i: (i, 0)),
                  const((D_KV_LORA, hk)), const((D_KV_LORA, D_HEADS * D_V))],
        out_specs=[pl.BlockSpec((tm, hk), lambda i: (i, 0)), pl.BlockSpec((tm, hv), lambda i: (i, 0))],
        out_shape=[jax.ShapeDtypeStruct((t, hk), BF16), jax.ShapeDtypeStruct((t, hv), BF16)],
        compiler_params=_params(("parallel",)),
        name="mla_kv",
    )(c, krp, wk, wv)


def _attn_kernel(q_ref, k_ref, v_ref, o_ref, m_ref, acc_ref, *, tq, tk, nk, q_off):
    hps = ATTN_HEADS
    q_first = q_off + pl.program_id(2) * tq
    first_chunk_end = q_first // CHUNK * CHUNK + CHUNK
    last_chunk_end = (q_first + tq - 1) // CHUNK * CHUNK + CHUNK
    n_full = jnp.minimum(nk, first_chunk_end // tk)
    n_live = jnp.minimum(nk, (last_chunk_end + tk - 1) // tk)
    m_ref[...] = jnp.full(m_ref.shape, -jnp.inf, F32)
    acc_ref[...] = jnp.zeros(acc_ref.shape, F32)
    qs = [q_ref[:, h * D_QK_PAD:(h + 1) * D_QK_PAD] for h in range(hps)]

    def step(j, masked):
        start = pl.multiple_of(j * tk, tk)
        ss = [lax.dot_general(qs[h], k_ref[pl.ds(start, tk), h * D_QK_PAD:(h + 1) * D_QK_PAD],
                              _dims(False, True), preferred_element_type=F32) for h in range(hps)]
        if masked:
            q_chunk = jnp.right_shift(q_first + _iota2((tq, tk), 0), CHUNK_SHIFT)
            k_chunk = jnp.right_shift(start + _iota2((tq, tk), 1), CHUNK_SHIFT)
            ss = [jnp.where(k_chunk <= q_chunk, s, -jnp.inf) for s in ss]
        m_old = [m_ref[h] for h in range(hps)]
        m_new = [jnp.maximum(m_old[h], jnp.max(ss[h], axis=-1, keepdims=True)) for h in range(hps)]
        ps = [jnp.exp(ss[h] - m_new[h]).astype(BF16) for h in range(hps)]
        for h in range(hps):
            acc_ref[h] = jnp.exp(m_old[h] - m_new[h]) * acc_ref[h] + jnp.dot(
                ps[h], v_ref[pl.ds(start, tk), h * D_V_PAD:(h + 1) * D_V_PAD], preferred_element_type=F32)
            m_ref[h] = m_new[h]

    def full_body(j, carry):
        step(j, False)
        return carry

    def edge_body(j, carry):
        step(j, True)
        return carry

    lax.fori_loop(0, n_full, full_body, 0)
    lax.fori_loop(n_full, n_live, edge_body, 0)
    for h in range(hps):
        acc = acc_ref[h]
        o_ref[:, h * D_V:(h + 1) * D_V] = acc[:, 0:D_V] / acc[:, D_V:D_V + 1]


def _attention(q, k, v, tq, tk, q_off):
    b, l, _ = q.shape
    lk = k.shape[1]
    hps = ATTN_HEADS
    return pl.pallas_call(
        functools.partial(_attn_kernel, tq=tq, tk=tk, nk=lk // tk, q_off=q_off),
        grid=(b, D_HEADS // hps, l // tq),
        in_specs=[pl.BlockSpec((None, tq, hps * D_QK_PAD), lambda bi, h, i: (bi, i, h)),
                  pl.BlockSpec((None, lk, hps * D_QK_PAD), lambda bi, h, i: (bi, 0, h)),
                  pl.BlockSpec((None, lk, hps * D_V_PAD), lambda bi, h, i: (bi, 0, h))],
        out_specs=pl.BlockSpec((None, tq, hps * D_V), lambda bi, h, i: (bi, i, h)),
        out_shape=jax.ShapeDtypeStruct((b, l, D_HEADS * D_V), F32),
        scratch_shapes=[pltpu.VMEM((hps, tq, 1), F32), pltpu.VMEM((hps, tq, D_V_PAD), F32)],
        compiler_params=_params(("parallel", "parallel", "arbitrary")),
        name="mla_attention",
    )(q, k, v)


def _pad_rows_front(buf):
    return jnp.pad(buf, ((0, 0), (ROW_PAD - buf.shape[1], 0), (0, 0)))


def _block_diag(w):
    nb, bi, bj = w.shape
    eye = jnp.eye(nb, dtype=w.dtype)
    return (eye[:, None, :, None] * w[:, :, None, :]).reshape(nb * bi, nb * bj)


def _state_to_block_diag(s, per):
    b, h, n, m = s.shape
    eye = jnp.eye(per, dtype=s.dtype)
    s6 = s.reshape(b, h // per, per, n, 1, m) * eye[None, None, :, None, :, None]
    return s6.reshape(b, h // per, per * n, per * m)


def _state_from_block_diag(sbd, per):
    b, g, pn, pm = sbd.shape
    n, m = pn // per, pm // per
    s6 = sbd.reshape(b, g, per, n, per, m)
    return jnp.stack([s6[:, :, p, :, p, :] for p in range(per)], axis=2).reshape(b, g * per, n, m)


def _rope_swap(w):
    half = w.shape[-1] // 2
    return jnp.concatenate([-w[..., half:], w[..., :half]], axis=-1)


def _prepare(w_in0, a_w_r, a_w_i, a_b_r, a_b_i, w_in1, c_w2, c_a2, d_w_qb, d_w_kvb):
    p = {}
    o = 0
    p["w_a"] = w_in0[:, o:o + 2 * A_WIDTH].astype(BF16); o += 2 * A_WIDTH
    p["w_qkv"] = w_in0[:, o:o + 3 * B_WIDTH].astype(BF16); o += 3 * B_WIDTH
    p["w_z"] = w_in0[:, o:o + B_WIDTH].astype(BF16); o += B_WIDTH
    p["w_gs"] = jnp.pad(w_in0[:, o:o + 2 * B_HEADS], ((0, 0), (0, 128 - 2 * B_HEADS))).astype(BF16)
    p["a_wg"] = jnp.concatenate([_block_diag(a_w_r), _block_diag(a_w_i)], axis=1).astype(BF16)
    p["a_bg"] = jnp.concatenate([a_b_r, a_b_i])[None, :]
    p["w_c"] = w_in1[:, 0:C_PROJ].astype(BF16)
    o = C_PROJ
    w_qa = w_in1[:, o:o + D_Q_LORA]; o += D_Q_LORA
    w_craw = w_in1[:, o:o + D_KV_LORA]; o += D_KV_LORA
    w_kr = w_in1[:, o:o + D_ROPE]
    place = lambda w: jnp.pad(w, ((0, 0), (D_NOPE, D_QK_PAD - D_NOPE - D_ROPE)))
    p["w_d"] = jnp.concatenate([w_qa, w_craw, place(w_kr), place(_rope_swap(w_kr))], axis=1).astype(BF16)
    p["c_w2p"] = jnp.pad(c_w2, ((0, 64), (0, 0)))
    p["c_a2p"] = jnp.pad(c_a2, ((64, 0), (0, 0)))
    wq = d_w_qb.reshape(D_Q_LORA, D_HEADS, D_NOPE + D_ROPE)
    zpad = jnp.zeros((D_Q_LORA, D_HEADS, D_QK_PAD - D_NOPE - D_ROPE), F32)
    p["wq"] = jnp.concatenate([wq, zpad], axis=-1).reshape(D_Q_LORA, D_HEADS * D_QK_PAD).astype(BF16)
    wq_sw = jnp.concatenate([jnp.zeros((D_Q_LORA, D_HEADS, D_NOPE), F32), _rope_swap(wq[..., D_NOPE:]), zpad], axis=-1)
    p["wqs"] = wq_sw.reshape(D_Q_LORA, D_HEADS * D_QK_PAD).astype(BF16)
    wkv = d_w_kvb.reshape(D_KV_LORA, D_HEADS, D_NOPE + D_V)
    wk = jnp.pad(wkv[..., :D_NOPE], ((0, 0), (0, 0), (0, D_QK_PAD - D_NOPE)))
    p["wk"] = wk.reshape(D_KV_LORA, D_HEADS * D_QK_PAD).astype(BF16)
    p["wv"] = wkv[..., D_NOPE:].reshape(D_KV_LORA, D_HEADS * D_V).astype(BF16)
    lane = jnp.arange(MXU_TILE) // C_HEAD_DIM
    p["seg"] = (lane[:, None] == lane[None, :]).astype(BF16)
    return p


def _rope_tables(pos):
    half = D_ROPE // 2
    inv = ROPE_THETA ** (-jnp.arange(half, dtype=F32) / half)
    ang = pos.astype(F32)[:, None] * inv[None, :]
    cos = jnp.cos(ang)
    sin = jnp.sin(ang)
    n = pos.shape[0]
    tail = jnp.zeros((n, D_QK_PAD - D_NOPE - D_ROPE), F32)
    cos2 = jnp.concatenate([cos, cos], axis=1)
    sin2 = jnp.concatenate([sin, sin], axis=1)
    q_cos = jnp.concatenate([jnp.ones((n, D_NOPE), F32), cos2, tail], axis=1)
    q_sin = jnp.concatenate([jnp.zeros((n, D_NOPE), F32), sin2, tail], axis=1)
    k_cos = jnp.concatenate([jnp.zeros((n, D_NOPE), F32), cos2, tail], axis=1)
    return q_cos, q_sin, k_cos


def _trunk(x, pe, pos, a_conv, a_h, b_conv, b_s, c_shift, c_s, d_ckv, d_krope, f_conv, wts, prm):
    (a_conv_w, a_conv_b, a_lambda, b_conv_w, b_a_log, b_dt_bias, b_norm_g, w_out0, c_mu, c_w0,
     c_a0, c_g2, c_k_k, c_k_a, c_r_k, c_lnx_g, c_lnx_b, d_qa_g, d_kva_g, w_out1, ln_mix_g, ln_mix_b,
     ffn_w_up, ffn_conv_w, ffn_conv_b, ffn_w_down, ln_ffn_g, ln_ffn_b, ple_w_gate, ple_b_gate,
     ple_w_proj) = wts
    b, l, d = x.shape
    t = b * l
    tm = min(512, t)
    tt = min(512, l)
    def chunks_per_step(want):
        cps = want
        while l % (cps * CHUNK):
            cps //= 2
        return cps
    row = lambda vec: vec.reshape(1, -1)
    f_new = []

    def ffn_block(i, mix_a, mix_b, xin, w_out):
        y, nbuf = _ffn(mix_a, mix_b, xin.reshape(b, l, d), pe, i, _pad_rows_front(f_conv[i]),
                       w_out.astype(BF16), row(ln_mix_g[i]), row(ln_mix_b[i]),
                       ffn_w_up[i].astype(BF16), ffn_conv_w[i], row(ffn_conv_b[i]),
                       ffn_w_down[i].astype(BF16), row(ln_ffn_g[i]), row(ln_ffn_b[i]),
                       ple_w_gate[i].astype(BF16), row(ple_b_gate[i]), ple_w_proj[i].astype(BF16), tt)
        f_new.append(nbuf[:, ROW_PAD - 2:, :])
        return y.reshape(t, d)

    x2 = x.reshape(t, d)
    a2, qkv, z, gs = _proj(x2, [prm["w_a"], prm["w_qkv"], prm["w_z"], prm["w_gs"]], tm)
    a_out, a_buf, a_hl = _rglru(a2.reshape(b, l, -1), _pad_rows_front(a_conv), a_h[:, None, :],
                                a_conv_w, row(a_conv_b), prm["a_wg"], prm["a_bg"], row(a_lambda),
                                min(256, l))
    pad8 = lambda vec: jnp.pad(vec, (B_HEADS, 128 - 2 * B_HEADS))[None, :]
    b_out, b_buf, b_sl = _gdn(qkv.reshape(b, l, -1), z.reshape(b, l, -1), gs.reshape(b, l, -1),
                              _pad_rows_front(b_conv), _state_to_block_diag(b_s, B_PAIR), b_conv_w,
                              pad8(b_a_log), pad8(b_dt_bias), row(b_norm_g), chunks_per_step(GDN_CPS))
    x2 = ffn_block(0, a_out, b_out, x2, w_out0)

    c_in, d_in = _proj(x2, [prm["w_c"], prm["w_d"]], tm)
    c_out, c_sh, c_sl = _rwkv(c_in.reshape(b, l, -1), _pad_rows_front(c_shift),
                              _state_to_block_diag(jnp.swapaxes(c_s, 2, 3), C_GROUP), row(c_mu), row(c_w0),
                              prm["c_w2p"], row(c_a0), prm["c_a2p"], c_g2, row(c_k_k), row(c_k_a),
                              row(c_r_k.reshape(-1)), row(c_lnx_g), row(c_lnx_b), prm["seg"],
                              chunks_per_step(RWKV_CPS))
    q_cos, q_sin, k_cos = _rope_tables(pos)
    q, c_new, kr_pad = _mla_prep(d_in.reshape(b, l, -1), q_cos, q_sin, k_cos, row(d_qa_g), prm["wq"],
                                 prm["wqs"], row(d_kva_g), min(512, l))
    past = d_ckv.shape[1]
    if past:
        c_all = jnp.concatenate([d_ckv, c_new], axis=1)
        kr_old = jnp.pad(d_krope, ((0, 0), (0, 0), (D_NOPE, D_QK_PAD - D_NOPE - D_ROPE)))
        kr_all = jnp.concatenate([kr_old, kr_pad], axis=1)
    else:
        c_all, kr_all = c_new, kr_pad
    lk = past + l
    tkv = b * lk
    k, v = _kv(c_all.reshape(tkv, -1), kr_all.reshape(tkv, -1), prm["wk"], prm["wv"],
               512 if tkv % 512 == 0 else lk)
    tq = min(ATTN_TQ, l)
    tk = ATTN_TK if lk % ATTN_TK == 0 else lk
    d_out = _attention(q, k.reshape(b, lk, -1), v.reshape(b, lk, -1), tq, tk, past)
    x2 = ffn_block(1, c_out, d_out, x2, w_out1)

    return (x2.reshape(b, l, d), a_buf[:, ROW_PAD - 3:, :], a_hl[:, 0, :], b_buf[:, ROW_PAD - 3:, :],
            _state_from_block_diag(b_sl, B_PAIR), c_sh[:, ROW_PAD - 1:, :],
            jnp.swapaxes(_state_from_block_diag(c_sl, C_GROUP), 2, 3), c_new,
            kr_pad[:, :, D_NOPE:D_NOPE + D_ROPE],
            jnp.stack(f_new))


def kernel(x_prompt, x_sample, state_a_conv, state_a_h, state_b_conv, state_b_s, state_c_shift, state_c_s, cache_d_ckv, cache_d_krope, state_ffn_conv, p_prompt, p_sample, w_in0, a_conv_w, a_conv_b, a_w_r, a_b_r, a_w_i, a_b_i, a_lambda, b_conv_w, b_a_log, b_dt_bias, b_norm_g, w_out0, w_in1, c_mu, c_w0, c_w2, c_a0, c_a2, c_g2, c_k_k, c_k_a, c_r_k, c_lnx_g, c_lnx_b, d_qa_g, d_w_qb, d_kva_g, d_w_kvb, w_out1, ln_mix_g, ln_mix_b, ffn_w_up, ffn_conv_w, ffn_conv_b, ffn_w_down, ln_ffn_g, ln_ffn_b, ple_w_gate, ple_b_gate, ple_w_proj):
    prm = _prepare(w_in0, a_w_r, a_w_i, a_b_r, a_b_i, w_in1, c_w2, c_a2, d_w_qb, d_w_kvb)
    wts = (a_conv_w, a_conv_b, a_lambda, b_conv_w, b_a_log, b_dt_bias, b_norm_g, w_out0, c_mu, c_w0,
           c_a0, c_g2, c_k_k, c_k_a, c_r_k, c_lnx_g, c_lnx_b, d_qa_g, d_kva_g, w_out1, ln_mix_g, ln_mix_b,
           ffn_w_up, ffn_conv_w, ffn_conv_b, ffn_w_down, ln_ffn_g, ln_ffn_b, ple_w_gate, ple_b_gate,
           ple_w_proj)
    dt = x_prompt.dtype
    bp, lp = x_prompt.shape[0], x_prompt.shape[1]
    zeros = lambda *shape: jnp.zeros(shape, dt)
    prompt = _trunk(x_prompt, p_prompt, jnp.arange(lp),
                    zeros(bp, 3, A_WIDTH), zeros(bp, A_WIDTH), zeros(bp, 3, 3 * B_WIDTH),
                    zeros(bp, B_HEADS, B_HEAD_DIM, B_HEAD_DIM), zeros(bp, 1, C_PROJ),
                    zeros(bp, C_HEADS, C_HEAD_DIM, C_HEAD_DIM), zeros(bp, 0, D_KV_LORA),
                    zeros(bp, 0, D_ROPE), zeros(DEPTH, bp, 2, 2 * D_FF), wts, prm)
    ls = x_sample.shape[1]
    past = cache_d_ckv.shape[1]
    sample = _trunk(x_sample, p_sample, past + jnp.arange(ls), state_a_conv, state_a_h, state_b_conv,
                    state_b_s, state_c_shift, state_c_s, cache_d_ckv, cache_d_krope, state_ffn_conv,
                    wts, prm)
    out = [prompt[0], sample[0]]
    for ps, ss in zip(prompt[1:], sample[1:]):
        out += [ps, ss]
    return tuple(out)
```

```python
import functools

import jax
import jax.numpy as jnp
from jax import lax
from jax.experimental import pallas as pl
from jax.experimental.pallas import tpu as pltpu

F32 = jnp.float32
BF16 = jnp.bfloat16

D_MODEL = 1024
DEPTH = 2
CHUNK = 64
CHUNK_SHIFT = CHUNK.bit_length() - 1
PLE_DIM = 256
ALPHA = (2 * DEPTH) ** 0.25
NORM_EPS = 1e-5
A_WIDTH = 512
A_BLOCKS = 8
A_C = 8.0
SQRT_FLOOR = 1e-30
B_HEADS = 4
B_HEAD_DIM = 128
B_WIDTH = 512
C_HEAD_DIM = 64
C_WIDTH = 512
C_HEADS = 8
C_LNX_EPS = 64e-5
C_PROJ = 3 * C_WIDTH + 64 + 64 + 128
D_HEADS = 4
D_Q_LORA = 384
D_KV_LORA = 256
D_NOPE = 128
D_ROPE = 64
D_V = 128
D_QK_PAD = 256
D_V_PAD = 256
MLA_SCALE = (D_NOPE + D_ROPE) ** -0.5
ROPE_THETA = 10000.0
D_FF = 2816
FF_BLOCK = 256
TRI_BASE = 8
MXU_TILE = 256
ROW_PAD = 8
VMEM_LIMIT = 56 * 1024 * 1024
ATTN_HEADS = 2
ATTN_TQ = 512
GDN_CPS = 8
RWKV_CPS = 4
ATTN_EDGE_SPLIT = 4
ATTN_TK = 2048
B_PAIR = MXU_TILE // B_HEAD_DIM
C_GROUP = MXU_TILE // C_HEAD_DIM


def _params(sem):
    return pltpu.CompilerParams(dimension_semantics=sem, vmem_limit_bytes=VMEM_LIMIT)


def _sigmoid(x):
    return 1.0 / (1.0 + jnp.exp(-x))


def _softplus(x):
    return jnp.maximum(x, 0.0) + jnp.log(1.0 + jnp.exp(-jnp.abs(x)))


def _silu(x):
    return x * _sigmoid(x)


def _gelu_tanh(x):
    return 0.5 * x * (1.0 + jnp.tanh(0.7978845608028654 * (x + 0.044715 * x * x * x)))


def _dims(trans_a, trans_b):
    return (((0 if trans_a else 1,), (1 if trans_b else 0,)), ((), ()))


def _mm(a, b, trans_a=False, trans_b=False):
    return lax.dot_general(a.astype(BF16), b.astype(BF16), _dims(trans_a, trans_b),
                           preferred_element_type=F32)


def _split3(a):
    a1 = a.astype(BF16)
    r1 = a - a1.astype(F32)
    a2 = r1.astype(BF16)
    return a1, a2, (r1 - a2.astype(F32)).astype(BF16)


def _mm_exact_rhs(a, mask):
    a1, a2, a3 = _split3(a)
    mb = mask.astype(BF16)
    out = jnp.dot(a3, mb, preferred_element_type=F32)
    out = out + jnp.dot(a2, mb, preferred_element_type=F32)
    return out + jnp.dot(a1, mb, preferred_element_type=F32)


def _mm_exact_lhs(mask, b):
    b1, b2, b3 = _split3(b)
    mb = mask.astype(BF16)
    out = jnp.dot(mb, b3, preferred_element_type=F32)
    out = out + jnp.dot(mb, b2, preferred_element_type=F32)
    return out + jnp.dot(mb, b1, preferred_element_type=F32)


def _iota2(shape, axis):
    return lax.broadcasted_iota(jnp.int32, shape, axis)


def _blk(idx, size):
    return jnp.right_shift(idx, size.bit_length() - 1)


def _bd_mask(shape, row_block, col_block):
    return _blk(_iota2(shape, 0), row_block) == _blk(_iota2(shape, 1), col_block)


def _bd_rows(x, mask):
    reps = mask.shape[0] // x.shape[0]
    return jnp.where(mask, jnp.concatenate([x] * reps, axis=0), 0.0)


def _tri_inv_cat(xs, bd):
    c, wd = xs[0].shape
    row = _iota2((c, wd), 0)
    col = _iota2((c, wd), 1) & (c - 1)
    eye = (row == col).astype(F32)

    def prod(a, b):
        return jnp.dot(a.astype(BF16), _bd_rows(b, bd).astype(BF16), preferred_element_type=F32)

    base = _blk(row, TRI_BASE) == _blk(col, TRI_BASE)
    ps = [jnp.where(base, x, 0.0) for x in xs]
    ts = [eye + p for p in ps]
    k = 2
    while k < TRI_BASE:
        ps = [prod(p, p) for p in ps]
        ts = [t + prod(t, p) for t, p in zip(ts, ps)]
        k *= 2
    size = TRI_BASE
    while size < c:
        pair = (_blk(row, 2 * size) == _blk(col, 2 * size)) & (_blk(row, size) != _blk(col, size))
        offs = [prod(t, jnp.where(pair, x, 0.0)) for t, x in zip(ts, xs)]
        ts = [t + prod(o, t) for t, o in zip(ts, offs)]
        size *= 2
    return ts


def _layer_norm(x, g, b):
    mu = jnp.mean(x, axis=-1, keepdims=True)
    xc = x - mu
    var = jnp.mean(xc * xc, axis=-1, keepdims=True)
    return xc * lax.rsqrt(var + NORM_EPS) * g + b


def _rms_norm(x, g, eps=1e-6):
    return x * lax.rsqrt(jnp.mean(x * x, axis=-1, keepdims=True) + eps) * g


def _proj_kernel(x_ref, *refs):
    n = len(refs) // 2
    xb = x_ref[...].astype(BF16)
    for w_ref, o_ref in zip(refs[:n], refs[n:]):
        o_ref[...] = jnp.dot(xb, w_ref[...], preferred_element_type=F32)


def _proj(x, ws, tm):
    t, k = x.shape
    return pl.pallas_call(
        _proj_kernel,
        grid=(t // tm,),
        in_specs=[pl.BlockSpec((tm, k), lambda i: (i, 0))]
        + [pl.BlockSpec(w.shape, lambda i: (0, 0)) for w in ws],
        out_specs=[pl.BlockSpec((tm, w.shape[1]), lambda i: (i, 0)) for w in ws],
        out_shape=[jax.ShapeDtypeStruct((t, w.shape[1]), F32) for w in ws],
        compiler_params=_params(("parallel",)),
        name="proj",
    )(x, *ws)


def _scan_rows(a, b):
    n = a.shape[0]
    row = _iota2(a.shape, 0)
    s = 1
    while s < n:
        a_sh = pltpu.roll(a, s, 0)
        b_sh = pltpu.roll(b, s, 0)
        m = row >= s
        b = jnp.where(m, a * b_sh + b, b)
        a = jnp.where(m, a * a_sh, a)
        s *= 2
    return a, b


def _rglru_kernel(a2_ref, buf_ref, h0_ref, cw_ref, cb_ref, wg_ref, bg_ref, lam_ref,
                  out_ref, nbuf_ref, hl_ref, xp_ref, hc_ref, *, tt):
    w = A_WIDTH

    @pl.when(pl.program_id(1) == 0)
    def _():
        xp_ref[0:ROW_PAD, :] = buf_ref[...]
        hc_ref[...] = h0_ref[...]

    gate_in = a2_ref[:, 0:w]
    x_in = a2_ref[:, w:2 * w]
    xp_ref[ROW_PAD:ROW_PAD + tt, :] = x_in
    xc = x_in * cw_ref[3:4, :] + cb_ref[...]
    for j in range(3):
        xc = xc + xp_ref[ROW_PAD - 3 + j:ROW_PAD - 3 + j + tt, :] * cw_ref[j:j + 1, :]
    gates = jnp.dot(xc.astype(BF16), wg_ref[...], preferred_element_type=F32) + bg_ref[...]
    r = _sigmoid(gates[:, 0:w])
    ig = _sigmoid(gates[:, w:2 * w])
    log_a = (-A_C) * r * _softplus(-lam_ref[...])
    a = jnp.exp(log_a)
    y = 1.0 - a * a
    u = y * lax.rsqrt(jnp.maximum(y, SQRT_FLOOR)) * (ig * xc)
    a_cum, h_loc = _scan_rows(a, u)
    h = a_cum * hc_ref[...] + h_loc
    out_ref[...] = h * _gelu_tanh(gate_in)
    hc_ref[...] = h[tt - 1:tt, :]
    hl_ref[...] = h[tt - 1:tt, :]
    last = x_in[tt - ROW_PAD:tt, :]
    xp_ref[0:ROW_PAD, :] = last
    nbuf_ref[...] = last


def _rglru(a2, buf8, h0, cw, cb, wg, bg, lam, tt):
    b, l, _ = a2.shape
    w = A_WIDTH
    const = lambda shape: pl.BlockSpec(shape, lambda bi, i: (0,) * len(shape))
    return pl.pallas_call(
        functools.partial(_rglru_kernel, tt=tt),
        grid=(b, l // tt),
        in_specs=[pl.BlockSpec((None, tt, 2 * w), lambda bi, i: (bi, i, 0)),
                  pl.BlockSpec((None, ROW_PAD, w), lambda bi, i: (bi, 0, 0)),
                  pl.BlockSpec((None, 1, w), lambda bi, i: (bi, 0, 0)),
                  const((4, w)), const((1, w)), const((w, 2 * w)), const((1, 2 * w)), const((1, w))],
        out_specs=[pl.BlockSpec((None, tt, w), lambda bi, i: (bi, i, 0)),
                   pl.BlockSpec((None, ROW_PAD, w), lambda bi, i: (bi, 0, 0)),
                   pl.BlockSpec((None, 1, w), lambda bi, i: (bi, 0, 0))],
        out_shape=[jax.ShapeDtypeStruct((b, l, w), F32),
                   jax.ShapeDtypeStruct((b, ROW_PAD, w), F32),
                   jax.ShapeDtypeStruct((b, 1, w), F32)],
        scratch_shapes=[pltpu.VMEM((ROW_PAD + tt, w), F32), pltpu.VMEM((1, w), F32)],
        compiler_params=_params(("parallel", "arbitrary")),
        name="rglru",
    )(a2, buf8, h0, cw, cb, wg, bg, lam)


def _gdn_kernel(qkv_ref, z_ref, gs_ref, buf_ref, s0_ref, cw_ref, alog_ref, dtb_ref, ng_ref,
                out_ref, nbuf_ref, sl_ref, xp_ref, s_ref, *, cps):
    c = CHUNK
    dk = B_HEAD_DIM
    nh = B_HEADS
    hw = nh * dk
    gw = MXU_TILE
    tt = cps * c

    @pl.when(pl.program_id(1) == 0)
    def _():
        xp_ref[0:ROW_PAD, :] = buf_ref[...]
        s_ref[...] = s0_ref[...]

    x_in = qkv_ref[...]
    xp_ref[ROW_PAD:ROW_PAD + tt, :] = x_in
    xc = x_in * cw_ref[3:4, :]
    for j in range(3):
        xc = xc + xp_ref[ROW_PAD - 3 + j:ROW_PAD - 3 + j + tt, :] * cw_ref[j:j + 1, :]
    xc = _silu(xc)
    last = x_in[tt - ROW_PAD:tt, :]
    xp_ref[0:ROW_PAD, :] = last
    nbuf_ref[...] = last

    def unit(x, scale):
        return x * (lax.rsqrt(jnp.sum(x * x, axis=-1, keepdims=True) + 1e-6) * scale)

    q4 = jnp.concatenate([unit(xc[:, h * dk:(h + 1) * dk], dk ** -0.5) for h in range(nh)], axis=1)
    k4 = jnp.concatenate([unit(xc[:, hw + h * dk:hw + (h + 1) * dk], 1.0) for h in range(nh)], axis=1)
    v4 = xc[:, 2 * hw:3 * hw]
    gs = gs_ref[...]
    beta_all = _sigmoid(gs)
    g_all = -jnp.exp(alog_ref[...]) * _softplus(gs + dtb_ref[...])

    def per_head(x, first):
        return jnp.concatenate([jnp.broadcast_to(x[:, first + h:first + h + 1], (x.shape[0], dk))
                                for h in range(nh)], axis=1)

    ri = _iota2((c, c), 0)
    ci = _iota2((c, c), 1)
    tri = (ri >= ci).astype(BF16)
    upper = (ri > ci).astype(F32)
    row_i = _iota2((c, gw), 0)
    lane_j = _iota2((c, gw), 1) & (c - 1)
    lane_head = _blk(_iota2((c, gw), 1), c)
    low_i = row_i >= lane_j
    low_s = row_i > lane_j
    bd_tok = _bd_mask((gw, gw), c, c)
    bd_kt = _bd_mask((gw, hw), c, dk)
    bd_vn = _bd_mask((B_PAIR * c, gw), c, dk)
    bd_st = _bd_mask((gw, gw), dk, dk)

    pre = []
    for cc in range(cps):
        rs = slice(cc * c, (cc + 1) * c)
        g_c = g_all[rs]
        gmat = _mm_exact_lhs(tri, jnp.concatenate(
            [g_c[:, nh + h:nh + h + 1] * upper for h in range(nh)] + [g_c], axis=1))
        decay = jnp.where(low_i, jnp.exp(gmat[:, 0:gw]), 0.0)
        gc_all = gmat[:, gw:gw + 128]
        gc4 = per_head(gc_all, nh)
        gl4 = per_head(gc_all[c - 1:c, :], nh)
        egc4 = jnp.exp(gc4)
        beta4 = per_head(beta_all[rs], 0)
        kc = k4[rs]
        qc = q4[rs]
        kb = kc * beta4
        gkq = _mm(jnp.concatenate([kb, qc], axis=0), _bd_rows(kc, bd_kt), trans_b=True)
        pre.append(dict(lmat=jnp.where(low_s, gkq[0:c] * decay, 0.0), qk=gkq[c:2 * c] * decay,
                        vb=v4[rs] * beta4, kbe=kb * egc4, q_in=qc * egc4,
                        k_out_t=jnp.transpose(kc * jnp.exp(gl4 - gc4)), egl=jnp.exp(gl4)))
    tmats = _tri_inv_cat([-p["lmat"] for p in pre], bd_tok)
    for p, tmat in zip(pre, tmats):
        us, ws = [], []
        for h in range(nh):
            hs = slice(h * dk, (h + 1) * dk)
            rhs = jnp.concatenate([p["vb"][:, hs], p["kbe"][:, hs]], axis=1)
            uw = _mm(jnp.where(lane_head == h, tmat, 0.0), jnp.concatenate([rhs] * nh, axis=0))
            us.append(uw[:, 0:dk])
            ws.append(uw[:, dk:2 * dk])
        p["u"] = jnp.concatenate(us, axis=1)
        p["w"] = jnp.concatenate(ws, axis=1)

    state = [s_ref[i] for i in range(nh // B_PAIR)]
    for cc, p in enumerate(pre):
        rs = slice(cc * c, (cc + 1) * c)
        outs = []
        for i in range(nh // B_PAIR):
            ps = slice(i * gw, (i + 1) * gw)
            ws = _mm(jnp.concatenate([p["w"][:, ps], p["q_in"][:, ps]], axis=0), state[i])
            v_new = p["u"][:, ps] - ws[0:c]
            qk = p["qk"][:, i * B_PAIR * c:(i + 1) * B_PAIR * c]
            outs.append(ws[c:2 * c] + _mm(qk, _bd_rows(v_new, bd_vn)))
            state[i] = state[i] * p["egl"][:, ps] + jnp.where(bd_st, _mm(p["k_out_t"][ps, :], v_new), 0.0)
        o = jnp.concatenate(outs, axis=1)
        for h in range(nh):
            hs = slice(h * dk, (h + 1) * dk)
            out_ref[rs, hs] = _rms_norm(o[:, hs], ng_ref[...]) * _silu(z_ref[rs, hs])
    for i in range(nh // B_PAIR):
        s_ref[i] = state[i]
        sl_ref[i] = state[i]


def _gdn(qkv, z, gs, buf8, s0, cw, alog, dtb, ng, cps):
    b, l, _ = qkv.shape
    tt = cps * CHUNK
    w3 = 3 * B_WIDTH
    const = lambda shape: pl.BlockSpec(shape, lambda bi, i: (0,) * len(shape))
    st = (B_HEADS // B_PAIR, MXU_TILE, MXU_TILE)
    return pl.pallas_call(
        functools.partial(_gdn_kernel, cps=cps),
        grid=(b, l // tt),
        in_specs=[pl.BlockSpec((None, tt, w3), lambda bi, i: (bi, i, 0)),
                  pl.BlockSpec((None, tt, B_WIDTH), lambda bi, i: (bi, i, 0)),
                  pl.BlockSpec((None, tt, 128), lambda bi, i: (bi, i, 0)),
                  pl.BlockSpec((None, ROW_PAD, w3), lambda bi, i: (bi, 0, 0)),
                  pl.BlockSpec((None,) + st, lambda bi, i: (bi, 0, 0, 0)),
                  const((4, w3)), const((1, 128)), const((1, 128)), const((1, B_HEAD_DIM))],
        out_specs=[pl.BlockSpec((None, tt, B_WIDTH), lambda bi, i: (bi, i, 0)),
                   pl.BlockSpec((None, ROW_PAD, w3), lambda bi, i: (bi, 0, 0)),
                   pl.BlockSpec((None,) + st, lambda bi, i: (bi, 0, 0, 0))],
        out_shape=[jax.ShapeDtypeStruct((b, l, B_WIDTH), F32),
                   jax.ShapeDtypeStruct((b, ROW_PAD, w3), F32),
                   jax.ShapeDtypeStruct((b,) + st, F32)],
        scratch_shapes=[pltpu.VMEM((ROW_PAD + tt, w3), F32), pltpu.VMEM(st, F32)],
        compiler_params=_params(("parallel", "arbitrary")),
        name="gdn",
    )(qkv, z, gs, buf8, s0, cw, alog, dtb, ng)


def _ffn_kernel(ma_ref, mb_ref, x_ref, pe_ref, buf_ref, wo_ref, g0_ref, b0_ref, wup_ref, cw_ref, cb_ref,
                wdn_ref, g_ref, b_ref, wg_ref, bg_ref, wp_ref, o_ref, nbuf_ref, prev_ref, act_ref, *, tt):
    ff = D_FF
    nb = ff // FF_BLOCK
    half = ma_ref.shape[1]

    @pl.when(pl.program_id(1) == 0)
    def _():
        prev_ref[...] = buf_ref[...]

    mix = jnp.dot(ma_ref[...].astype(BF16), wo_ref[0:half, :], preferred_element_type=F32)
    mix = mix + jnp.dot(mb_ref[...].astype(BF16), wo_ref[half:2 * half, :], preferred_element_type=F32)
    x = _layer_norm(ALPHA * x_ref[...] + mix, g0_ref[...], b0_ref[...])
    xb = x.astype(BF16)
    row = _iota2((tt, FF_BLOCK), 0)

    def conv(col):
        sl = slice(col, col + FF_BLOCK)
        h = jnp.dot(xb, wup_ref[:, sl], preferred_element_type=F32)
        p1 = prev_ref[ROW_PAD - 1:ROW_PAD, sl]
        p2 = prev_ref[ROW_PAD - 2:ROW_PAD - 1, sl]
        h1 = jnp.where(row == 0, p1, pltpu.roll(h, 1, 0))
        h2 = jnp.where(row == 0, p2, jnp.where(row == 1, p1, pltpu.roll(h, 2, 0)))
        prev_ref[:, sl] = h[tt - ROW_PAD:tt, :]
        return h2 * cw_ref[0:1, sl] + h1 * cw_ref[1:2, sl] + h * cw_ref[2:3, sl] + cb_ref[:, sl]

    for j in range(nb):
        gate = conv(j * FF_BLOCK)
        val = conv(ff + j * FF_BLOCK)
        act_ref[:, j * FF_BLOCK:(j + 1) * FF_BLOCK] = (_silu(gate) * val).astype(BF16)
    nbuf_ref[...] = prev_ref[...]
    f = jnp.dot(act_ref[...], wdn_ref[...], preferred_element_type=F32)
    y = _layer_norm(ALPHA * x + f, g_ref[...], b_ref[...])
    gate = _sigmoid(jnp.dot(y.astype(BF16), wg_ref[...], preferred_element_type=F32) + bg_ref[...])
    emb = jnp.dot(pe_ref[...].astype(BF16), wp_ref[...], preferred_element_type=F32)
    o_ref[...] = y + gate * emb


def _ffn(ma, mb, x, pe, layer, buf8, wo, g0, b0, wup, cw, cb, wdn, g, bb, wg, bg, wp, tt):
    b, l, d = x.shape
    half = ma.shape[2]
    ff2 = 2 * D_FF
    const = lambda shape: pl.BlockSpec(shape, lambda bi, i: (0,) * len(shape),
                                       pipeline_mode=pl.Buffered(1))
    return pl.pallas_call(
        functools.partial(_ffn_kernel, tt=tt),
        grid=(b, l // tt),
        in_specs=[pl.BlockSpec((None, tt, half), lambda bi, i: (bi, i, 0)),
                  pl.BlockSpec((None, tt, half), lambda bi, i: (bi, i, 0)),
                  pl.BlockSpec((None, tt, d), lambda bi, i: (bi, i, 0)),
                  pl.BlockSpec((None, None, tt, PLE_DIM), lambda bi, i: (layer, bi, i, 0)),
                  pl.BlockSpec((None, ROW_PAD, ff2), lambda bi, i: (bi, 0, 0)),
                  const((2 * half, d)), const((1, d)), const((1, d)),
                  const((d, ff2)), const((3, ff2)), const((1, ff2)), const((D_FF, d)),
                  const((1, d)), const((1, d)), const((d, d)), const((1, d)), const((PLE_DIM, d))],
        out_specs=[pl.BlockSpec((None, tt, d), lambda bi, i: (bi, i, 0)),
                   pl.BlockSpec((None, ROW_PAD, ff2), lambda bi, i: (bi, 0, 0))],
        out_shape=[jax.ShapeDtypeStruct((b, l, d), F32),
                   jax.ShapeDtypeStruct((b, ROW_PAD, ff2), F32)],
        scratch_shapes=[pltpu.VMEM((ROW_PAD, ff2), F32), pltpu.VMEM((tt, D_FF), BF16)],
        compiler_params=_params(("parallel", "arbitrary")),
        name="ffn",
    )(ma, mb, x, pe, buf8, wo, g0, b0, wup, cw, cb, wdn, g, bb, wg, bg, wp)


def _rwkv_kernel(p_ref, sh_ref, s0_ref, mu_ref, w0_ref, w2_ref, a0_ref, a2_ref, g2_ref, kk_ref,
                 ka_ref, rk_ref, lg_ref, lb_ref, seg_ref,
                 out_ref, nsh_ref, sl_ref, xp_ref, s_ref, *, cps):
    c = CHUNK
    w = C_WIDTH
    n = C_HEAD_DIM
    gw = MXU_TILE
    ng = w // gw
    tt = cps * c

    @pl.when(pl.program_id(1) == 0)
    def _():
        xp_ref[0:ROW_PAD, :] = sh_ref[...]
        s_ref[...] = s0_ref[...]

    proj = p_ref[...]
    xp_ref[ROW_PAD:ROW_PAD + tt, :] = proj
    prev = xp_ref[ROW_PAD - 1:ROW_PAD - 1 + tt, :]
    last = proj[tt - ROW_PAD:tt, :]
    xp_ref[0:ROW_PAD, :] = last
    nsh_ref[...] = last
    xs = proj + (prev - proj) * mu_ref[...]

    seg = seg_ref[...]

    def head_sum(x):
        return jnp.concatenate([_mm_exact_rhs(x[:, i * gw:(i + 1) * gw], seg) for i in range(ng)], axis=1)

    r = xs[:, 0:w]
    k = xs[:, w:2 * w]
    v = xs[:, 2 * w:3 * w]
    lora_in = xs[:, 3 * w:3 * w + 128]
    ww = -_softplus(-(w0_ref[...] + _mm(jnp.tanh(lora_in), w2_ref[...]))) - 0.5
    a = _sigmoid(a0_ref[...] + _mm(lora_in, a2_ref[...]))
    g = _mm(_sigmoid(xs[:, 3 * w + 128:3 * w + 256]), g2_ref[...])
    kx = k * kk_ref[...]
    k = k * (1.0 + (a - 1.0) * ka_ref[...])
    sums = head_sum(jnp.concatenate([kx * kx, r * k * rk_ref[...]], axis=0))
    kk = kx * lax.rsqrt(sums[0:tt] + 1e-6)
    bonus = sums[tt:2 * tt] * v
    lw = -jnp.exp(ww)
    sa = -kk
    sb = kk * a

    ri = _iota2((c, c), 0)
    ci = _iota2((c, c), 1)
    tri = (ri >= ci).astype(BF16)
    row_i = _iota2((c, gw), 0)
    lane_j = _iota2((c, gw), 1) & (c - 1)
    low_i = row_i >= lane_j
    low_s = row_i > lane_j
    bd = _bd_mask((gw, gw), n, n)

    units = []
    for cc in range(cps):
        rs = slice(cc * c, (cc + 1) * c)
        lw_c = lw[rs]
        cum = _mm_exact_lhs(tri, lw_c)
        cum_last = cum[c - 1:c, :]
        e_out = jnp.exp(-cum)
        e_end = jnp.exp(cum_last - cum)
        a_t = sa[rs] * jnp.exp(cum - lw_c)
        r_t = r[rs] * jnp.exp(cum)
        b_t = sb[rs] * e_out
        k_t = k[rs] * e_out
        b_o = sb[rs] * e_end
        k_o = k[rs] * e_end
        e_last = jnp.exp(cum_last)
        for i in range(ng):
            gs = slice(i * gw, (i + 1) * gw)
            bk = jnp.concatenate([b_o[:, gs], k_o[:, gs]], axis=0)
            e_col = jnp.transpose(jnp.broadcast_to(e_last[:, gs], (2 * c, gw)))
            units.append(dict(cc=cc, i=i, lr=jnp.concatenate([a_t[:, gs], r_t[:, gs]], axis=0),
                              b_t=b_t[:, gs], k_t=k_t[:, gs], v=v[rs, gs], bk_t=jnp.transpose(bk),
                              e_last=jnp.concatenate([e_col] * (gw // (2 * c)), axis=1)))
    for un in units:
        g1 = _mm(un["lr"], _bd_rows(un["b_t"], bd), trans_b=True)
        g2 = _mm(un["lr"], _bd_rows(un["k_t"], bd), trans_b=True)
        un["m_ab"] = jnp.where(low_s, g1[0:c], 0.0)
        un["m_ak"] = jnp.where(low_s, g2[0:c], 0.0)
        un["m_r"] = jnp.concatenate([jnp.where(low_i, g1[c:2 * c], 0.0),
                                     jnp.where(low_i, g2[c:2 * c], 0.0)], axis=1)
        un["bdv"] = _bd_rows(un["v"], bd)
    for un, tmat in zip(units, _tri_inv_cat([un["m_ab"] for un in units], bd)):
        un["tmat"] = tmat
        un["akv"] = _mm(un["m_ak"], un["bdv"])

    state = [s_ref[i] for i in range(ng)]
    outs = [[None] * ng for _ in range(cps)]
    for un in units:
        i = un["i"]
        ls = _mm(un["lr"], state[i])
        u = _mm(un["tmat"], _bd_rows(ls[0:c] + un["akv"], bd))
        outs[un["cc"]][i] = ls[c:2 * c] + _mm(
            un["m_r"], jnp.concatenate([_bd_rows(u, bd), un["bdv"]], axis=0))
        uv = jnp.concatenate([u, un["v"]], axis=0)
        state[i] = state[i] * un["e_last"] + jnp.where(bd, _mm(un["bk_t"], uv), 0.0)
    for i in range(ng):
        s_ref[i] = state[i]
        sl_ref[i] = state[i]

    o = jnp.concatenate([jnp.concatenate(row, axis=1) for row in outs], axis=0)
    inv_n = 1.0 / n
    oc = o - head_sum(o) * inv_n
    var = head_sum(oc * oc) * inv_n
    on = oc * lax.rsqrt(var + C_LNX_EPS) * lg_ref[...] + lb_ref[...]
    out_ref[...] = (on + bonus) * g


def _rwkv(proj, sh8, s0, mu, w0, w2p, a0, a2p, g2, kk, ka, rk, lg, lb, seg, cps):
    b, l, cp = proj.shape
    tt = cps * CHUNK
    w = C_WIDTH
    const = lambda shape: pl.BlockSpec(shape, lambda bi, i: (0,) * len(shape))
    st = (C_HEADS // C_GROUP, MXU_TILE, MXU_TILE)
    return pl.pallas_call(
        functools.partial(_rwkv_kernel, cps=cps),
        grid=(b, l // tt),
        in_specs=[pl.BlockSpec((None, tt, cp), lambda bi, i: (bi, i, 0)),
                  pl.BlockSpec((None, ROW_PAD, cp), lambda bi, i: (bi, 0, 0)),
                  pl.BlockSpec((None,) + st, lambda bi, i: (bi, 0, 0, 0)),
                  const((1, cp)), const((1, w)), const((128, w)), const((1, w)), const((128, w)),
                  const((128, w)), const((1, w)), const((1, w)), const((1, w)), const((1, w)),
                  const((1, w)), const((MXU_TILE, MXU_TILE))],
        out_specs=[pl.BlockSpec((None, tt, w), lambda bi, i: (bi, i, 0)),
                   pl.BlockSpec((None, ROW_PAD, cp), lambda bi, i: (bi, 0, 0)),
                   pl.BlockSpec((None,) + st, lambda bi, i: (bi, 0, 0, 0))],
        out_shape=[jax.ShapeDtypeStruct((b, l, w), F32),
                   jax.ShapeDtypeStruct((b, ROW_PAD, cp), F32),
                   jax.ShapeDtypeStruct((b,) + st, F32)],
        scratch_shapes=[pltpu.VMEM((ROW_PAD + tt, cp), F32), pltpu.VMEM(st, F32)],
        compiler_params=_params(("parallel", "arbitrary")),
        name="rwkv7",
    )(proj, sh8, s0, mu, w0, w2p, a0, a2p, g2, kk, ka, rk, lg, lb, seg)


def _mla_prep_kernel(d_ref, ct_ref, st_ref, kc_ref, qg_ref, wq_ref, wqs_ref, kg_ref,
                     q_ref, c_ref, kr_ref):
    lq = D_Q_LORA
    lkv = D_KV_LORA
    qn = _rms_norm(d_ref[:, 0:lq], qg_ref[...]).astype(BF16)
    ct = ct_ref[...]
    st = st_ref[...]
    for h in range(D_HEADS):
        sl = slice(h * D_QK_PAD, (h + 1) * D_QK_PAD)
        q = jnp.dot(qn, wq_ref[:, sl], preferred_element_type=F32) * ct
        q = q + jnp.dot(qn, wqs_ref[:, sl], preferred_element_type=F32) * st
        q_ref[:, sl] = (q * MLA_SCALE).astype(BF16)
    c_ref[...] = _rms_norm(d_ref[:, lq:lq + lkv], kg_ref[...])
    o = lq + lkv
    kr_ref[...] = d_ref[:, o:o + D_QK_PAD] * kc_ref[...] + d_ref[:, o + D_QK_PAD:o + 2 * D_QK_PAD] * st


def _mla_prep(d, ctab, stab, kctab, qg, wq, wqs, kg, tm):
    b, l, dw = d.shape
    hq = D_HEADS * D_QK_PAD
    const = lambda shape: pl.BlockSpec(shape, lambda bi, i: (0,) * len(shape))
    tab = pl.BlockSpec((tm, D_QK_PAD), lambda bi, i: (i, 0))
    return pl.pallas_call(
        _mla_prep_kernel,
        grid=(b, l // tm),
        in_specs=[pl.BlockSpec((None, tm, dw), lambda bi, i: (bi, i, 0)), tab, tab, tab,
                  const((1, D_Q_LORA)), const((D_Q_LORA, hq)), const((D_Q_LORA, hq)),
                  const((1, D_KV_LORA))],
        out_specs=[pl.BlockSpec((None, tm, hq), lambda bi, i: (bi, i, 0)),
                   pl.BlockSpec((None, tm, D_KV_LORA), lambda bi, i: (bi, i, 0)),
                   pl.BlockSpec((None, tm, D_QK_PAD), lambda bi, i: (bi, i, 0))],
        out_shape=[jax.ShapeDtypeStruct((b, l, hq), BF16),
                   jax.ShapeDtypeStruct((b, l, D_KV_LORA), F32),
                   jax.ShapeDtypeStruct((b, l, D_QK_PAD), F32)],
        compiler_params=_params(("parallel", "parallel")),
        name="mla_prep",
    )(d, ctab, stab, kctab, qg, wq, wqs, kg)


def _kv_kernel(c_ref, kr_ref, wk_ref, wv_ref, k_ref, v_ref):
    cb = c_ref[...].astype(BF16)
    kr = kr_ref[...]
    ones = jnp.ones((cb.shape[0], D_V_PAD - D_V), BF16)
    for h in range(D_HEADS):
        sl = slice(h * D_QK_PAD, (h + 1) * D_QK_PAD)
        k_ref[:, sl] = (jnp.dot(cb, wk_ref[:, sl], preferred_element_type=F32) + kr).astype(BF16)
        vh = jnp.dot(cb, wv_ref[:, h * D_V:(h + 1) * D_V], preferred_element_type=F32).astype(BF16)
        v_ref[:, h * D_V_PAD:h * D_V_PAD + D_V] = vh
        v_ref[:, h * D_V_PAD + D_V:(h + 1) * D_V_PAD] = ones


def _kv(c, krp, wk, wv, tm):
    t = c.shape[0]
    hk = D_HEADS * D_QK_PAD
    hv = D_HEADS * D_V_PAD
    const = lambda shape: pl.BlockSpec(shape, lambda i: (0,) * len(shape))
    return pl.pallas_call(
        _kv_kernel,
        grid=(t // tm,),
        in_specs=[pl.BlockSpec((tm, D_KV_LORA), lambda i: (i, 0)),
                  pl.BlockSpec((tm, D_QK_PAD), lambda i: (i, 0)),
                  const((D_KV_LORA, hk)), const((D_KV_LORA, D_HEADS * D_V))],
        out_specs=[pl.BlockSpec((tm, hk), lambda i: (i, 0)), pl.BlockSpec((tm, hv), lambda i: (i, 0))],
        out_shape=[jax.ShapeDtypeStruct((t, hk), BF16), jax.ShapeDtypeStruct((t, hv), BF16)],
        compiler_params=_params(("parallel",)),
        name="mla_kv",
    )(c, krp, wk, wv)


def _attn_kernel(q_ref, k_ref, v_ref, o_ref, m_ref, acc_ref, *, tq, tk, nk, q_off):
    hps = ATTN_HEADS
    q_first = q_off + pl.program_id(2) * tq
    first_chunk_end = q_first // CHUNK * CHUNK + CHUNK
    last_chunk_end = (q_first + tq - 1) // CHUNK * CHUNK + CHUNK
    n_full = jnp.minimum(nk, first_chunk_end // tk)
    m_ref[...] = jnp.full(m_ref.shape, -jnp.inf, F32)
    acc_ref[...] = jnp.zeros(acc_ref.shape, F32)
    qs = [q_ref[:, h * D_QK_PAD:(h + 1) * D_QK_PAD] for h in range(hps)]

    def block(start, width, masked):
        ss = [lax.dot_general(qs[h], k_ref[pl.ds(start, width), h * D_QK_PAD:(h + 1) * D_QK_PAD],
                              _dims(False, True), preferred_element_type=F32) for h in range(hps)]
        if masked:
            q_chunk = jnp.right_shift(q_first + _iota2((tq, width), 0), CHUNK_SHIFT)
            k_chunk = jnp.right_shift(start + _iota2((tq, width), 1), CHUNK_SHIFT)
            ss = [jnp.where(k_chunk <= q_chunk, s, -jnp.inf) for s in ss]
        m_old = [m_ref[h] for h in range(hps)]
        m_new = [jnp.maximum(m_old[h], jnp.max(ss[h], axis=-1, keepdims=True)) for h in range(hps)]
        ps = [jnp.exp(ss[h] - m_new[h]).astype(BF16) for h in range(hps)]
        for h in range(hps):
            acc_ref[h] = jnp.exp(m_old[h] - m_new[h]) * acc_ref[h] + jnp.dot(
                ps[h], v_ref[pl.ds(start, width), h * D_V_PAD:(h + 1) * D_V_PAD], preferred_element_type=F32)
            m_ref[h] = m_new[h]

    def full_body(j, carry):
        block(pl.multiple_of(j * tk, tk), tk, False)
        return carry

    def edge_body(j, carry):
        block(pl.multiple_of(j * tk, tk), tk, True)
        return carry

    edge_len = jnp.minimum(nk * tk, last_chunk_end) - n_full * tk
    n_edge = jnp.maximum(edge_len - 1, 0) // tk
    lax.fori_loop(0, n_full, full_body, 0)
    lax.fori_loop(n_full, n_full + n_edge, edge_body, 0)
    tail = edge_len - n_edge * tk
    tail_start = pl.multiple_of((n_full + n_edge) * tk, tk)
    ew = tk // ATTN_EDGE_SPLIT
    for i in range(1, ATTN_EDGE_SPLIT + 1):
        @pl.when((tail > (i - 1) * ew) & (tail <= i * ew))
        def _(i=i):
            block(tail_start, i * ew, True)

    for h in range(hps):
        acc = acc_ref[h]
        o_ref[:, h * D_V:(h + 1) * D_V] = acc[:, 0:D_V] / acc[:, D_V:D_V + 1]


def _attention(q, k, v, tq, tk, q_off):
    b, l, _ = q.shape
    lk = k.shape[1]
    hps = ATTN_HEADS
    return pl.pallas_call(
        functools.partial(_attn_kernel, tq=tq, tk=tk, nk=lk // tk, q_off=q_off),
        grid=(b, D_HEADS // hps, l // tq),
        in_specs=[pl.BlockSpec((None, tq, hps * D_QK_PAD), lambda bi, h, i: (bi, i, h)),
                  pl.BlockSpec((None, lk, hps * D_QK_PAD), lambda bi, h, i: (bi, 0, h)),
                  pl.BlockSpec((None, lk, hps * D_V_PAD), lambda bi, h, i: (bi, 0, h))],
        out_specs=pl.BlockSpec((None, tq, hps * D_V), lambda bi, h, i: (bi, i, h)),
        out_shape=jax.ShapeDtypeStruct((b, l, D_HEADS * D_V), F32),
        scratch_shapes=[pltpu.VMEM((hps, tq, 1), F32), pltpu.VMEM((hps, tq, D_V_PAD), F32)],
        compiler_params=_params(("parallel", "parallel", "arbitrary")),
        name="mla_attention",
    )(q, k, v)


def _pad_rows_front(buf):
    return jnp.pad(buf, ((0, 0), (ROW_PAD - buf.shape[1], 0), (0, 0)))


def _block_diag(w):
    nb, bi, bj = w.shape
    eye = jnp.eye(nb, dtype=w.dtype)
    return (eye[:, None, :, None] * w[:, :, None, :]).reshape(nb * bi, nb * bj)


def _state_to_block_diag(s, per):
    b, h, n, m = s.shape
    s5 = s.reshape(b, h // per, per, n, m)
    rows = [jnp.pad(s5[:, :, p], ((0, 0), (0, 0), (0, 0), (p * m, (per - 1 - p) * m))) for p in range(per)]
    return jnp.concatenate(rows, axis=2)


def _state_from_block_diag(sbd, per):
    b, g, pn, pm = sbd.shape
    n, m = pn // per, pm // per
    blocks = [sbd[:, :, p * n:(p + 1) * n, p * m:(p + 1) * m] for p in range(per)]
    return jnp.stack(blocks, axis=2).reshape(b, g * per, n, m)


def _rope_swap(w):
    half = w.shape[-1] // 2
    return jnp.concatenate([-w[..., half:], w[..., :half]], axis=-1)


def _prepare(w_in0, a_w_r, a_w_i, a_b_r, a_b_i, w_in1, c_w2, c_a2, d_w_qb, d_w_kvb):
    p = {}
    o = 0
    p["w_a"] = w_in0[:, o:o + 2 * A_WIDTH].astype(BF16); o += 2 * A_WIDTH
    p["w_qkv"] = w_in0[:, o:o + 3 * B_WIDTH].astype(BF16); o += 3 * B_WIDTH
    p["w_z"] = w_in0[:, o:o + B_WIDTH].astype(BF16); o += B_WIDTH
    p["w_gs"] = jnp.pad(w_in0[:, o:o + 2 * B_HEADS], ((0, 0), (0, 128 - 2 * B_HEADS))).astype(BF16)
    p["a_wg"] = jnp.concatenate([_block_diag(a_w_r), _block_diag(a_w_i)], axis=1).astype(BF16)
    p["a_bg"] = jnp.concatenate([a_b_r, a_b_i])[None, :]
    p["w_c"] = w_in1[:, 0:C_PROJ].astype(BF16)
    o = C_PROJ
    w_qa = w_in1[:, o:o + D_Q_LORA]; o += D_Q_LORA
    w_craw = w_in1[:, o:o + D_KV_LORA]; o += D_KV_LORA
    w_kr = w_in1[:, o:o + D_ROPE]
    place = lambda w: jnp.pad(w, ((0, 0), (D_NOPE, D_QK_PAD - D_NOPE - D_ROPE)))
    p["w_d"] = jnp.concatenate([w_qa, w_craw, place(w_kr), place(_rope_swap(w_kr))], axis=1).astype(BF16)
    p["c_w2p"] = jnp.pad(c_w2, ((0, 64), (0, 0)))
    p["c_a2p"] = jnp.pad(c_a2, ((64, 0), (0, 0)))
    wq = d_w_qb.reshape(D_Q_LORA, D_HEADS, D_NOPE + D_ROPE)
    zpad = jnp.zeros((D_Q_LORA, D_HEADS, D_QK_PAD - D_NOPE - D_ROPE), F32)
    p["wq"] = jnp.concatenate([wq, zpad], axis=-1).reshape(D_Q_LORA, D_HEADS * D_QK_PAD).astype(BF16)
    wq_sw = jnp.concatenate([jnp.zeros((D_Q_LORA, D_HEADS, D_NOPE), F32), _rope_swap(wq[..., D_NOPE:]), zpad], axis=-1)
    p["wqs"] = wq_sw.reshape(D_Q_LORA, D_HEADS * D_QK_PAD).astype(BF16)
    wkv = d_w_kvb.reshape(D_KV_LORA, D_HEADS, D_NOPE + D_V)
    wk = jnp.pad(wkv[..., :D_NOPE], ((0, 0), (0, 0), (0, D_QK_PAD - D_NOPE)))
    p["wk"] = wk.reshape(D_KV_LORA, D_HEADS * D_QK_PAD).astype(BF16)
    p["wv"] = wkv[..., D_NOPE:].reshape(D_KV_LORA, D_HEADS * D_V).astype(BF16)
    lane = jnp.arange(MXU_TILE) // C_HEAD_DIM
    p["seg"] = (lane[:, None] == lane[None, :]).astype(BF16)
    return p


def _rope_tables(pos):
    half = D_ROPE // 2
    inv = ROPE_THETA ** (-jnp.arange(half, dtype=F32) / half)
    ang = pos.astype(F32)[:, None] * inv[None, :]
    cos = jnp.cos(ang)
    sin = jnp.sin(ang)
    n = pos.shape[0]
    tail = jnp.zeros((n, D_QK_PAD - D_NOPE - D_ROPE), F32)
    cos2 = jnp.concatenate([cos, cos], axis=1)
    sin2 = jnp.concatenate([sin, sin], axis=1)
    q_cos = jnp.concatenate([jnp.ones((n, D_NOPE), F32), cos2, tail], axis=1)
    q_sin = jnp.concatenate([jnp.zeros((n, D_NOPE), F32), sin2, tail], axis=1)
    k_cos = jnp.concatenate([jnp.zeros((n, D_NOPE), F32), cos2, tail], axis=1)
    return q_cos, q_sin, k_cos


def _trunk(x, pe, pos, a_conv, a_h, b_conv, b_s, c_shift, c_s, d_ckv, d_krope, f_conv, wts, prm):
    (a_conv_w, a_conv_b, a_lambda, b_conv_w, b_a_log, b_dt_bias, b_norm_g, w_out0, c_mu, c_w0,
     c_a0, c_g2, c_k_k, c_k_a, c_r_k, c_lnx_g, c_lnx_b, d_qa_g, d_kva_g, w_out1, ln_mix_g, ln_mix_b,
     ffn_w_up, ffn_conv_w, ffn_conv_b, ffn_w_down, ln_ffn_g, ln_ffn_b, ple_w_gate, ple_b_gate,
     ple_w_proj) = wts
    b, l, d = x.shape
    t = b * l
    tm = min(512, t)
    tt = min(512, l)
    def chunks_per_step(want):
        cps = want
        while l % (cps * CHUNK):
            cps //= 2
        return cps
    row = lambda vec: vec.reshape(1, -1)
    f_new = []

    def ffn_block(i, mix_a, mix_b, xin, w_out):
        y, nbuf = _ffn(mix_a, mix_b, xin.reshape(b, l, d), pe, i, _pad_rows_front(f_conv[i]),
                       w_out.astype(BF16), row(ln_mix_g[i]), row(ln_mix_b[i]),
                       ffn_w_up[i].astype(BF16), ffn_conv_w[i], row(ffn_conv_b[i]),
                       ffn_w_down[i].astype(BF16), row(ln_ffn_g[i]), row(ln_ffn_b[i]),
                       ple_w_gate[i].astype(BF16), row(ple_b_gate[i]), ple_w_proj[i].astype(BF16), tt)
        f_new.append(nbuf[:, ROW_PAD - 2:, :])
        return y.reshape(t, d)

    x2 = x.reshape(t, d)
    a2, qkv, z, gs = _proj(x2, [prm["w_a"], prm["w_qkv"], prm["w_z"], prm["w_gs"]], tm)
    a_out, a_buf, a_hl = _rglru(a2.reshape(b, l, -1), _pad_rows_front(a_conv), a_h[:, None, :],
                                a_conv_w, row(a_conv_b), prm["a_wg"], prm["a_bg"], row(a_lambda),
                                min(256, l))
    pad8 = lambda vec: jnp.pad(vec, (B_HEADS, 128 - 2 * B_HEADS))[None, :]
    b_out, b_buf, b_sl = _gdn(qkv.reshape(b, l, -1), z.reshape(b, l, -1), gs.reshape(b, l, -1),
                              _pad_rows_front(b_conv), _state_to_block_diag(b_s, B_PAIR), b_conv_w,
                              pad8(b_a_log), pad8(b_dt_bias), row(b_norm_g), chunks_per_step(GDN_CPS))
    x2 = ffn_block(0, a_out, b_out, x2, w_out0)

    c_in, d_in = _proj(x2, [prm["w_c"], prm["w_d"]], tm)
    c_out, c_sh, c_sl = _rwkv(c_in.reshape(b, l, -1), _pad_rows_front(c_shift),
                              _state_to_block_diag(jnp.swapaxes(c_s, 2, 3), C_GROUP), row(c_mu), row(c_w0),
                              prm["c_w2p"], row(c_a0), prm["c_a2p"], c_g2, row(c_k_k), row(c_k_a),
                              row(c_r_k.reshape(-1)), row(c_lnx_g), row(c_lnx_b), prm["seg"],
                              chunks_per_step(RWKV_CPS))
    q_cos, q_sin, k_cos = _rope_tables(pos)
    q, c_new, kr_pad = _mla_prep(d_in.reshape(b, l, -1), q_cos, q_sin, k_cos, row(d_qa_g), prm["wq"],
                                 prm["wqs"], row(d_kva_g), min(512, l))
    past = d_ckv.shape[1]
    if past:
        c_all = jnp.concatenate([d_ckv, c_new], axis=1)
        kr_old = jnp.pad(d_krope, ((0, 0), (0, 0), (D_NOPE, D_QK_PAD - D_NOPE - D_ROPE)))
        kr_all = jnp.concatenate([kr_old, kr_pad], axis=1)
    else:
        c_all, kr_all = c_new, kr_pad
    lk = past + l
    tkv = b * lk
    k, v = _kv(c_all.reshape(tkv, -1), kr_all.reshape(tkv, -1), prm["wk"], prm["wv"],
               512 if tkv % 512 == 0 else lk)
    tq = min(ATTN_TQ, l)
    tk = ATTN_TK if lk % ATTN_TK == 0 else lk
    d_out = _attention(q, k.reshape(b, lk, -1), v.reshape(b, lk, -1), tq, tk, past)
    x2 = ffn_block(1, c_out, d_out, x2, w_out1)

    return (x2.reshape(b, l, d), a_buf[:, ROW_PAD - 3:, :], a_hl[:, 0, :], b_buf[:, ROW_PAD - 3:, :],
            _state_from_block_diag(b_sl, B_PAIR), c_sh[:, ROW_PAD - 1:, :],
            jnp.swapaxes(_state_from_block_diag(c_sl, C_GROUP), 2, 3), c_new,
            kr_pad[:, :, D_NOPE:D_NOPE + D_ROPE],
            jnp.stack(f_new))


def kernel(x_prompt, x_sample, state_a_conv, state_a_h, state_b_conv, state_b_s, state_c_shift, state_c_s, cache_d_ckv, cache_d_krope, state_ffn_conv, p_prompt, p_sample, w_in0, a_conv_w, a_conv_b, a_w_r, a_b_r, a_w_i, a_b_i, a_lambda, b_conv_w, b_a_log, b_dt_bias, b_norm_g, w_out0, w_in1, c_mu, c_w0, c_w2, c_a0, c_a2, c_g2, c_k_k, c_k_a, c_r_k, c_lnx_g, c_lnx_b, d_qa_g, d_w_qb, d_kva_g, d_w_kvb, w_out1, ln_mix_g, ln_mix_b, ffn_w_up, ffn_conv_w, ffn_conv_b, ffn_w_down, ln_ffn_g, ln_ffn_b, ple_w_gate, ple_b_gate, ple_w_proj):
    prm = _prepare(w_in0, a_w_r, a_w_i, a_b_r, a_b_i, w_in1, c_w2, c_a2, d_w_qb, d_w_kvb)
    wts = (a_conv_w, a_conv_b, a_lambda, b_conv_w, b_a_log, b_dt_bias, b_norm_g, w_out0, c_mu, c_w0,
           c_a0, c_g2, c_k_k, c_k_a, c_r_k, c_lnx_g, c_lnx_b, d_qa_g, d_kva_g, w_out1, ln_mix_g, ln_mix_b,
           ffn_w_up, ffn_conv_w, ffn_conv_b, ffn_w_down, ln_ffn_g, ln_ffn_b, ple_w_gate, ple_b_gate,
           ple_w_proj)
    dt = x_prompt.dtype
    bp, lp = x_prompt.shape[0], x_prompt.shape[1]
    zeros = lambda *shape: jnp.zeros(shape, dt)
    prompt = _trunk(x_prompt, p_prompt, jnp.arange(lp),
                    zeros(bp, 3, A_WIDTH), zeros(bp, A_WIDTH), zeros(bp, 3, 3 * B_WIDTH),
                    zeros(bp, B_HEADS, B_HEAD_DIM, B_HEAD_DIM), zeros(bp, 1, C_PROJ),
                    zeros(bp, C_HEADS, C_HEAD_DIM, C_HEAD_DIM), zeros(bp, 0, D_KV_LORA),
                    zeros(bp, 0, D_ROPE), zeros(DEPTH, bp, 2, 2 * D_FF), wts, prm)
    ls = x_sample.shape[1]
    past = cache_d_ckv.shape[1]
    sample = _trunk(x_sample, p_sample, past + jnp.arange(ls), state_a_conv, state_a_h, state_b_conv,
                    state_b_s, state_c_shift, state_c_s, cache_d_ckv, cache_d_krope, state_ffn_conv,
                    wts, prm)
    out = [prompt[0], sample[0]]
    for ps, ss in zip(prompt[1:], sample[1:]):
        out += [ps, ss]
    return tuple(out)
```

```python
import functools

import jax
import jax.numpy as jnp
from jax import lax
from jax.experimental import pallas as pl
from jax.experimental.pallas import tpu as pltpu

F32 = jnp.float32
BF16 = jnp.bfloat16

D_MODEL = 1024
DEPTH = 2
CHUNK = 64
CHUNK_SHIFT = CHUNK.bit_length() - 1
PLE_DIM = 256
ALPHA = (2 * DEPTH) ** 0.25
NORM_EPS = 1e-5
A_WIDTH = 512
A_BLOCKS = 8
A_C = 8.0
SQRT_FLOOR = 1e-30
B_HEADS = 4
B_HEAD_DIM = 128
B_WIDTH = 512
C_HEAD_DIM = 64
C_WIDTH = 512
C_HEADS = 8
C_LNX_EPS = 64e-5
C_PROJ = 3 * C_WIDTH + 64 + 64 + 128
D_HEADS = 4
D_Q_LORA = 384
D_KV_LORA = 256
D_NOPE = 128
D_ROPE = 64
D_V = 128
D_QK_PAD = 256
D_V_PAD = 256
MLA_SCALE = (D_NOPE + D_ROPE) ** -0.5
ROPE_THETA = 10000.0
D_FF = 2816
FF_BLOCK = 256
TRI_BASE = 8
MXU_TILE = 256
ROW_PAD = 8
VMEM_LIMIT = 56 * 1024 * 1024
ATTN_HEADS = 2
ATTN_TQ = 512
GDN_CPS = 8
RWKV_CPS = 4
ATTN_EDGE_SPLIT = 4
ATTN_TK = 2048
B_PAIR = MXU_TILE // B_HEAD_DIM
C_GROUP = MXU_TILE // C_HEAD_DIM


def _params(sem):
    return pltpu.CompilerParams(dimension_semantics=sem, vmem_limit_bytes=VMEM_LIMIT)


def _sigmoid(x):
    return 1.0 / (1.0 + jnp.exp(-x))


def _softplus(x):
    return jnp.maximum(x, 0.0) + jnp.log(1.0 + jnp.exp(-jnp.abs(x)))


def _silu(x):
    return x * _sigmoid(x)


def _gelu_tanh(x):
    return 0.5 * x * (1.0 + jnp.tanh(0.7978845608028654 * (x + 0.044715 * x * x * x)))


def _dims(trans_a, trans_b):
    return (((0 if trans_a else 1,), (1 if trans_b else 0,)), ((), ()))


def _mm(a, b, trans_a=False, trans_b=False):
    return lax.dot_general(a.astype(BF16), b.astype(BF16), _dims(trans_a, trans_b),
                           preferred_element_type=F32)


def _split3(a):
    a1 = a.astype(BF16)
    r1 = a - a1.astype(F32)
    a2 = r1.astype(BF16)
    return a1, a2, (r1 - a2.astype(F32)).astype(BF16)


def _mm_exact_rhs(a, mask):
    a1, a2, a3 = _split3(a)
    mb = mask.astype(BF16)
    out = jnp.dot(a3, mb, preferred_element_type=F32)
    out = out + jnp.dot(a2, mb, preferred_element_type=F32)
    return out + jnp.dot(a1, mb, preferred_element_type=F32)


def _mm_exact_lhs(mask, b):
    b1, b2, b3 = _split3(b)
    mb = mask.astype(BF16)
    out = jnp.dot(mb, b3, preferred_element_type=F32)
    out = out + jnp.dot(mb, b2, preferred_element_type=F32)
    return out + jnp.dot(mb, b1, preferred_element_type=F32)


def _iota2(shape, axis):
    return lax.broadcasted_iota(jnp.int32, shape, axis)


def _blk(idx, size):
    return jnp.right_shift(idx, size.bit_length() - 1)


def _bd_mask(shape, row_block, col_block):
    return _blk(_iota2(shape, 0), row_block) == _blk(_iota2(shape, 1), col_block)


def _bd_rows(x, mask):
    reps = mask.shape[0] // x.shape[0]
    return jnp.where(mask, jnp.concatenate([x] * reps, axis=0), 0.0)


def _tri_inv_cat(xs, bd):
    c, wd = xs[0].shape
    row = _iota2((c, wd), 0)
    col = _iota2((c, wd), 1) & (c - 1)
    eye = (row == col).astype(F32)

    def prod(a, b):
        return jnp.dot(a.astype(BF16), _bd_rows(b, bd).astype(BF16), preferred_element_type=F32)

    base = _blk(row, TRI_BASE) == _blk(col, TRI_BASE)
    ps = [jnp.where(base, x, 0.0) for x in xs]
    ts = [eye + p for p in ps]
    k = 2
    while k < TRI_BASE:
        ps = [prod(p, p) for p in ps]
        ts = [t + prod(t, p) for t, p in zip(ts, ps)]
        k *= 2
    size = TRI_BASE
    while size < c:
        pair = (_blk(row, 2 * size) == _blk(col, 2 * size)) & (_blk(row, size) != _blk(col, size))
        offs = [prod(t, jnp.where(pair, x, 0.0)) for t, x in zip(ts, xs)]
        ts = [t + prod(o, t) for t, o in zip(ts, offs)]
        size *= 2
    return ts


def _layer_norm(x, g, b):
    mu = jnp.mean(x, axis=-1, keepdims=True)
    xc = x - mu
    var = jnp.mean(xc * xc, axis=-1, keepdims=True)
    return xc * lax.rsqrt(var + NORM_EPS) * g + b


def _rms_norm(x, g, eps=1e-6):
    return x * lax.rsqrt(jnp.mean(x * x, axis=-1, keepdims=True) + eps) * g


def _proj_kernel(x_ref, *refs):
    n = len(refs) // 2
    xb = x_ref[...].astype(BF16)
    for w_ref, o_ref in zip(refs[:n], refs[n:]):
        o_ref[...] = jnp.dot(xb, w_ref[...], preferred_element_type=F32)


def _proj(x, ws, tm):
    t, k = x.shape
    return pl.pallas_call(
        _proj_kernel,
        grid=(t // tm,),
        in_specs=[pl.BlockSpec((tm, k), lambda i: (i, 0))]
        + [pl.BlockSpec(w.shape, lambda i: (0, 0)) for w in ws],
        out_specs=[pl.BlockSpec((tm, w.shape[1]), lambda i: (i, 0)) for w in ws],
        out_shape=[jax.ShapeDtypeStruct((t, w.shape[1]), F32) for w in ws],
        compiler_params=_params(("parallel",)),
        name="proj",
    )(x, *ws)


def _scan_rows(a, b):
    n = a.shape[0]
    row = _iota2(a.shape, 0)
    s = 1
    while s < n:
        a_sh = pltpu.roll(a, s, 0)
        b_sh = pltpu.roll(b, s, 0)
        m = row >= s
        b = jnp.where(m, a * b_sh + b, b)
        a = jnp.where(m, a * a_sh, a)
        s *= 2
    return a, b


def _rglru_kernel(a2_ref, buf_ref, h0_ref, cw_ref, cb_ref, wg_ref, bg_ref, lam_ref,
                  out_ref, nbuf_ref, hl_ref, xp_ref, hc_ref, *, tt):
    w = A_WIDTH

    @pl.when(pl.program_id(1) == 0)
    def _():
        xp_ref[0:ROW_PAD, :] = buf_ref[...]
        hc_ref[...] = h0_ref[...]

    gate_in = a2_ref[:, 0:w]
    x_in = a2_ref[:, w:2 * w]
    xp_ref[ROW_PAD:ROW_PAD + tt, :] = x_in
    xc = x_in * cw_ref[3:4, :] + cb_ref[...]
    for j in range(3):
        xc = xc + xp_ref[ROW_PAD - 3 + j:ROW_PAD - 3 + j + tt, :] * cw_ref[j:j + 1, :]
    gates = jnp.dot(xc.astype(BF16), wg_ref[...], preferred_element_type=F32) + bg_ref[...]
    r = _sigmoid(gates[:, 0:w])
    ig = _sigmoid(gates[:, w:2 * w])
    log_a = (-A_C) * r * _softplus(-lam_ref[...])
    a = jnp.exp(log_a)
    y = 1.0 - a * a
    u = y * lax.rsqrt(jnp.maximum(y, SQRT_FLOOR)) * (ig * xc)
    a_cum, h_loc = _scan_rows(a, u)
    h = a_cum * hc_ref[...] + h_loc
    out_ref[...] = h * _gelu_tanh(gate_in)
    hc_ref[...] = h[tt - 1:tt, :]
    hl_ref[...] = h[tt - 1:tt, :]
    last = x_in[tt - ROW_PAD:tt, :]
    xp_ref[0:ROW_PAD, :] = last
    nbuf_ref[...] = last


def _rglru(a2, buf8, h0, cw, cb, wg, bg, lam, tt):
    b, l, _ = a2.shape
    w = A_WIDTH
    const = lambda shape: pl.BlockSpec(shape, lambda bi, i: (0,) * len(shape))
    return pl.pallas_call(
        functools.partial(_rglru_kernel, tt=tt),
        grid=(b, l // tt),
        in_specs=[pl.BlockSpec((None, tt, 2 * w), lambda bi, i: (bi, i, 0)),
                  pl.BlockSpec((None, ROW_PAD, w), lambda bi, i: (bi, 0, 0)),
                  pl.BlockSpec((None, 1, w), lambda bi, i: (bi, 0, 0)),
                  const((4, w)), const((1, w)), const((w, 2 * w)), const((1, 2 * w)), const((1, w))],
        out_specs=[pl.BlockSpec((None, tt, w), lambda bi, i: (bi, i, 0)),
                   pl.BlockSpec((None, ROW_PAD, w), lambda bi, i: (bi, 0, 0)),
                   pl.BlockSpec((None, 1, w), lambda bi, i: (bi, 0, 0))],
        out_shape=[jax.ShapeDtypeStruct((b, l, w), F32),
                   jax.ShapeDtypeStruct((b, ROW_PAD, w), F32),
                   jax.ShapeDtypeStruct((b, 1, w), F32)],
        scratch_shapes=[pltpu.VMEM((ROW_PAD + tt, w), F32), pltpu.VMEM((1, w), F32)],
        compiler_params=_params(("parallel", "arbitrary")),
        name="rglru",
    )(a2, buf8, h0, cw, cb, wg, bg, lam)


def _gdn_kernel(qkv_ref, z_ref, gs_ref, buf_ref, s0_ref, cw_ref, alog_ref, dtb_ref, ng_ref,
                out_ref, nbuf_ref, sl_ref, xp_ref, s_ref, *, cps):
    c = CHUNK
    dk = B_HEAD_DIM
    nh = B_HEADS
    hw = nh * dk
    gw = MXU_TILE
    tt = cps * c

    @pl.when(pl.program_id(1) == 0)
    def _():
        xp_ref[0:ROW_PAD, :] = buf_ref[...]
        s_ref[...] = s0_ref[...]

    x_in = qkv_ref[...]
    xp_ref[ROW_PAD:ROW_PAD + tt, :] = x_in
    xc = x_in * cw_ref[3:4, :]
    for j in range(3):
        xc = xc + xp_ref[ROW_PAD - 3 + j:ROW_PAD - 3 + j + tt, :] * cw_ref[j:j + 1, :]
    xc = _silu(xc)
    last = x_in[tt - ROW_PAD:tt, :]
    xp_ref[0:ROW_PAD, :] = last
    nbuf_ref[...] = last

    def unit(x, scale):
        return x * (lax.rsqrt(jnp.sum(x * x, axis=-1, keepdims=True) + 1e-6) * scale)

    q4 = jnp.concatenate([unit(xc[:, h * dk:(h + 1) * dk], dk ** -0.5) for h in range(nh)], axis=1)
    k4 = jnp.concatenate([unit(xc[:, hw + h * dk:hw + (h + 1) * dk], 1.0) for h in range(nh)], axis=1)
    v4 = xc[:, 2 * hw:3 * hw]
    gs = gs_ref[...]
    beta_all = _sigmoid(gs)
    g_all = -jnp.exp(alog_ref[...]) * _softplus(gs + dtb_ref[...])

    def per_head(x, first):
        return jnp.concatenate([jnp.broadcast_to(x[:, first + h:first + h + 1], (x.shape[0], dk))
                                for h in range(nh)], axis=1)

    ri = _iota2((c, c), 0)
    ci = _iota2((c, c), 1)
    tri = (ri >= ci).astype(BF16)
    upper = (ri > ci).astype(F32)
    row_i = _iota2((c, gw), 0)
    lane_j = _iota2((c, gw), 1) & (c - 1)
    lane_head = _blk(_iota2((c, gw), 1), c)
    low_i = row_i >= lane_j
    low_s = row_i > lane_j
    bd_tok = _bd_mask((gw, gw), c, c)
    bd_kt = _bd_mask((gw, hw), c, dk)
    bd_vn = _bd_mask((B_PAIR * c, gw), c, dk)
    bd_st = _bd_mask((gw, gw), dk, dk)

    pre = []
    for cc in range(cps):
        rs = slice(cc * c, (cc + 1) * c)
        g_c = g_all[rs]
        gmat = _mm_exact_lhs(tri, jnp.concatenate(
            [g_c[:, nh + h:nh + h + 1] * upper for h in range(nh)] + [g_c], axis=1))
        decay = jnp.where(low_i, jnp.exp(gmat[:, 0:gw]), 0.0)
        gc_all = gmat[:, gw:gw + 128]
        gc4 = per_head(gc_all, nh)
        gl4 = per_head(gc_all[c - 1:c, :], nh)
        egc4 = jnp.exp(gc4)
        beta4 = per_head(beta_all[rs], 0)
        kc = k4[rs]
        qc = q4[rs]
        kb = kc * beta4
        gkq = _mm(jnp.concatenate([kb, qc], axis=0), _bd_rows(kc, bd_kt), trans_b=True)
        pre.append(dict(lmat=jnp.where(low_s, gkq[0:c] * decay, 0.0), qk=gkq[c:2 * c] * decay,
                        vb=v4[rs] * beta4, kbe=kb * egc4, q_in=qc * egc4,
                        k_out_t=jnp.transpose(kc * jnp.exp(gl4 - gc4)), egl=jnp.exp(gl4)))
    tmats = _tri_inv_cat([-p["lmat"] for p in pre], bd_tok)
    for p, tmat in zip(pre, tmats):
        us, ws = [], []
        for h in range(nh):
            hs = slice(h * dk, (h + 1) * dk)
            rhs = jnp.concatenate([p["vb"][:, hs], p["kbe"][:, hs]], axis=1)
            uw = _mm(jnp.where(lane_head == h, tmat, 0.0), jnp.concatenate([rhs] * nh, axis=0))
            us.append(uw[:, 0:dk])
            ws.append(uw[:, dk:2 * dk])
        p["u"] = jnp.concatenate(us, axis=1)
        p["w"] = jnp.concatenate(ws, axis=1)

    state = [s_ref[i] for i in range(nh // B_PAIR)]
    for cc, p in enumerate(pre):
        rs = slice(cc * c, (cc + 1) * c)
        outs = []
        for i in range(nh // B_PAIR):
            ps = slice(i * gw, (i + 1) * gw)
            ws = _mm(jnp.concatenate([p["w"][:, ps], p["q_in"][:, ps]], axis=0), state[i])
            v_new = p["u"][:, ps] - ws[0:c]
            qk = p["qk"][:, i * B_PAIR * c:(i + 1) * B_PAIR * c]
            outs.append(ws[c:2 * c] + _mm(qk, _bd_rows(v_new, bd_vn)))
            state[i] = state[i] * p["egl"][:, ps] + jnp.where(bd_st, _mm(p["k_out_t"][ps, :], v_new), 0.0)
        o = jnp.concatenate(outs, axis=1)
        for h in range(nh):
            hs = slice(h * dk, (h + 1) * dk)
            out_ref[rs, hs] = _rms_norm(o[:, hs], ng_ref[...]) * _silu(z_ref[rs, hs])
    for i in range(nh // B_PAIR):
        s_ref[i] = state[i]
        sl_ref[i] = state[i]


def _gdn(qkv, z, gs, buf8, s0, cw, alog, dtb, ng, cps):
    b, l, _ = qkv.shape
    tt = cps * CHUNK
    w3 = 3 * B_WIDTH
    const = lambda shape: pl.BlockSpec(shape, lambda bi, i: (0,) * len(shape))
    st = (B_HEADS // B_PAIR, MXU_TILE, MXU_TILE)
    return pl.pallas_call(
        functools.partial(_gdn_kernel, cps=cps),
        grid=(b, l // tt),
        in_specs=[pl.BlockSpec((None, tt, w3), lambda bi, i: (bi, i, 0)),
                  pl.BlockSpec((None, tt, B_WIDTH), lambda bi, i: (bi, i, 0)),
                  pl.BlockSpec((None, tt, 128), lambda bi, i: (bi, i, 0)),
                  pl.BlockSpec((None, ROW_PAD, w3), lambda bi, i: (bi, 0, 0)),
                  pl.BlockSpec((None,) + st, lambda bi, i: (bi, 0, 0, 0)),
                  const((4, w3)), const((1, 128)), const((1, 128)), const((1, B_HEAD_DIM))],
        out_specs=[pl.BlockSpec((None, tt, B_WIDTH), lambda bi, i: (bi, i, 0)),
                   pl.BlockSpec((None, ROW_PAD, w3), lambda bi, i: (bi, 0, 0)),
                   pl.BlockSpec((None,) + st, lambda bi, i: (bi, 0, 0, 0))],
        out_shape=[jax.ShapeDtypeStruct((b, l, B_WIDTH), F32),
                   jax.ShapeDtypeStruct((b, ROW_PAD, w3), F32),
                   jax.ShapeDtypeStruct((b,) + st, F32)],
        scratch_shapes=[pltpu.VMEM((ROW_PAD + tt, w3), F32), pltpu.VMEM(st, F32)],
        compiler_params=_params(("parallel", "arbitrary")),
        name="gdn",
    )(qkv, z, gs, buf8, s0, cw, alog, dtb, ng)


def _ffn_kernel(ma_ref, mb_ref, x_ref, pe_ref, buf_ref, wo_ref, g0_ref, b0_ref, wup_ref, cw_ref, cb_ref,
                wdn_ref, g_ref, b_ref, wg_ref, bg_ref, wp_ref, o_ref, nbuf_ref, prev_ref, act_ref, *, tt):
    ff = D_FF
    nb = ff // FF_BLOCK
    half = ma_ref.shape[1]

    @pl.when(pl.program_id(1) == 0)
    def _():
        prev_ref[...] = buf_ref[...]

    mix = jnp.dot(ma_ref[...].astype(BF16), wo_ref[0:half, :], preferred_element_type=F32)
    mix = mix + jnp.dot(mb_ref[...].astype(BF16), wo_ref[half:2 * half, :], preferred_element_type=F32)
    x = _layer_norm(ALPHA * x_ref[...] + mix, g0_ref[...], b0_ref[...])
    xb = x.astype(BF16)
    row = _iota2((tt, FF_BLOCK), 0)

    def conv(col):
        sl = slice(col, col + FF_BLOCK)
        h = jnp.dot(xb, wup_ref[:, sl], preferred_element_type=F32)
        p1 = prev_ref[ROW_PAD - 1:ROW_PAD, sl]
        p2 = prev_ref[ROW_PAD - 2:ROW_PAD - 1, sl]
        h1 = jnp.where(row == 0, p1, pltpu.roll(h, 1, 0))
        h2 = jnp.where(row == 0, p2, jnp.where(row == 1, p1, pltpu.roll(h, 2, 0)))
        prev_ref[:, sl] = h[tt - ROW_PAD:tt, :]
        return h2 * cw_ref[0:1, sl] + h1 * cw_ref[1:2, sl] + h * cw_ref[2:3, sl] + cb_ref[:, sl]

    for j in range(nb):
        gate = conv(j * FF_BLOCK)
        val = conv(ff + j * FF_BLOCK)
        act_ref[:, j * FF_BLOCK:(j + 1) * FF_BLOCK] = (_silu(gate) * val).astype(BF16)
    nbuf_ref[...] = prev_ref[...]
    f = jnp.dot(act_ref[...], wdn_ref[...], preferred_element_type=F32)
    y = _layer_norm(ALPHA * x + f, g_ref[...], b_ref[...])
    gate = _sigmoid(jnp.dot(y.astype(BF16), wg_ref[...], preferred_element_type=F32) + bg_ref[...])
    emb = jnp.dot(pe_ref[...].astype(BF16), wp_ref[...], preferred_element_type=F32)
    o_ref[...] = y + gate * emb


def _ffn(ma, mb, x, pe, layer, buf8, wo, g0, b0, wup, cw, cb, wdn, g, bb, wg, bg, wp, tt):
    b, l, d = x.shape
    half = ma.shape[2]
    ff2 = 2 * D_FF
    const = lambda shape: pl.BlockSpec(shape, lambda bi, i: (0,) * len(shape),
                                       pipeline_mode=pl.Buffered(1))
    return pl.pallas_call(
        functools.partial(_ffn_kernel, tt=tt),
        grid=(b, l // tt),
        in_specs=[pl.BlockSpec((None, tt, half), lambda bi, i: (bi, i, 0)),
                  pl.BlockSpec((None, tt, half), lambda bi, i: (bi, i, 0)),
                  pl.BlockSpec((None, tt, d), lambda bi, i: (bi, i, 0)),
                  pl.BlockSpec((None, None, tt, PLE_DIM), lambda bi, i: (layer, bi, i, 0)),
                  pl.BlockSpec((None, ROW_PAD, ff2), lambda bi, i: (bi, 0, 0)),
                  const((2 * half, d)), const((1, d)), const((1, d)),
                  const((d, ff2)), const((3, ff2)), const((1, ff2)), const((D_FF, d)),
                  const((1, d)), const((1, d)), const((d, d)), const((1, d)), const((PLE_DIM, d))],
        out_specs=[pl.BlockSpec((None, tt, d), lambda bi, i: (bi, i, 0)),
                   pl.BlockSpec((None, ROW_PAD, ff2), lambda bi, i: (bi, 0, 0))],
        out_shape=[jax.ShapeDtypeStruct((b, l, d), F32),
                   jax.ShapeDtypeStruct((b, ROW_PAD, ff2), F32)],
        scratch_shapes=[pltpu.VMEM((ROW_PAD, ff2), F32), pltpu.VMEM((tt, D_FF), BF16)],
        compiler_params=_params(("parallel", "arbitrary")),
        name="ffn",
    )(ma, mb, x, pe, buf8, wo, g0, b0, wup, cw, cb, wdn, g, bb, wg, bg, wp)


def _rwkv_kernel(p_ref, sh_ref, s0_ref, mu_ref, w0_ref, w2_ref, a0_ref, a2_ref, g2_ref, kk_ref,
                 ka_ref, rk_ref, lg_ref, lb_ref, seg_ref,
                 out_ref, nsh_ref, sl_ref, xp_ref, s_ref, *, cps):
    c = CHUNK
    w = C_WIDTH
    n = C_HEAD_DIM
    gw = MXU_TILE
    ng = w // gw
    tt = cps * c

    @pl.when(pl.program_id(1) == 0)
    def _():
        xp_ref[0:ROW_PAD, :] = sh_ref[...]
        s_ref[...] = s0_ref[...]

    proj = p_ref[...]
    xp_ref[ROW_PAD:ROW_PAD + tt, :] = proj
    prev = xp_ref[ROW_PAD - 1:ROW_PAD - 1 + tt, :]
    last = proj[tt - ROW_PAD:tt, :]
    xp_ref[0:ROW_PAD, :] = last
    nsh_ref[...] = last
    xs = proj + (prev - proj) * mu_ref[...]

    seg = seg_ref[...]

    def head_sum(x):
        return jnp.concatenate([_mm_exact_rhs(x[:, i * gw:(i + 1) * gw], seg) for i in range(ng)], axis=1)

    r = xs[:, 0:w]
    k = xs[:, w:2 * w]
    v = xs[:, 2 * w:3 * w]
    lora_in = xs[:, 3 * w:3 * w + 128]
    ww = -_softplus(-(w0_ref[...] + _mm(jnp.tanh(lora_in), w2_ref[...]))) - 0.5
    a = _sigmoid(a0_ref[...] + _mm(lora_in, a2_ref[...]))
    g = _mm(_sigmoid(xs[:, 3 * w + 128:3 * w + 256]), g2_ref[...])
    kx = k * kk_ref[...]
    k = k * (1.0 + (a - 1.0) * ka_ref[...])
    sums = head_sum(jnp.concatenate([kx * kx, r * k * rk_ref[...]], axis=0))
    kk = kx * lax.rsqrt(sums[0:tt] + 1e-6)
    bonus = sums[tt:2 * tt] * v
    lw = -jnp.exp(ww)
    sa = -kk
    sb = kk * a

    ri = _iota2((c, c), 0)
    ci = _iota2((c, c), 1)
    tri = (ri >= ci).astype(BF16)
    row_i = _iota2((c, gw), 0)
    lane_j = _iota2((c, gw), 1) & (c - 1)
    low_i = row_i >= lane_j
    low_s = row_i > lane_j
    bd = _bd_mask((gw, gw), n, n)

    units = []
    for cc in range(cps):
        rs = slice(cc * c, (cc + 1) * c)
        lw_c = lw[rs]
        cum = _mm_exact_lhs(tri, lw_c)
        cum_last = cum[c - 1:c, :]
        e_out = jnp.exp(-cum)
        e_end = jnp.exp(cum_last - cum)
        a_t = sa[rs] * jnp.exp(cum - lw_c)
        r_t = r[rs] * jnp.exp(cum)
        b_t = sb[rs] * e_out
        k_t = k[rs] * e_out
        b_o = sb[rs] * e_end
        k_o = k[rs] * e_end
        e_last = jnp.exp(cum_last)
        for i in range(ng):
            gs = slice(i * gw, (i + 1) * gw)
            bk = jnp.concatenate([b_o[:, gs], k_o[:, gs]], axis=0)
            e_col = jnp.transpose(jnp.broadcast_to(e_last[:, gs], (2 * c, gw)))
            units.append(dict(cc=cc, i=i, lr=jnp.concatenate([a_t[:, gs], r_t[:, gs]], axis=0),
                              b_t=b_t[:, gs], k_t=k_t[:, gs], v=v[rs, gs], bk_t=jnp.transpose(bk),
                              e_last=jnp.concatenate([e_col] * (gw // (2 * c)), axis=1)))
    for un in units:
        g1 = _mm(un["lr"], _bd_rows(un["b_t"], bd), trans_b=True)
        g2 = _mm(un["lr"], _bd_rows(un["k_t"], bd), trans_b=True)
        un["m_ab"] = jnp.where(low_s, g1[0:c], 0.0)
        un["m_ak"] = jnp.where(low_s, g2[0:c], 0.0)
        un["m_r"] = jnp.concatenate([jnp.where(low_i, g1[c:2 * c], 0.0),
                                     jnp.where(low_i, g2[c:2 * c], 0.0)], axis=1)
        un["bdv"] = _bd_rows(un["v"], bd)
    for un, tmat in zip(units, _tri_inv_cat([un["m_ab"] for un in units], bd)):
        un["tmat"] = tmat
        un["akv"] = _mm(un["m_ak"], un["bdv"])

    state = [s_ref[i] for i in range(ng)]
    outs = [[None] * ng for _ in range(cps)]
    for un in units:
        i = un["i"]
        ls = _mm(un["lr"], state[i])
        u = _mm(un["tmat"], _bd_rows(ls[0:c] + un["akv"], bd))
        outs[un["cc"]][i] = ls[c:2 * c] + _mm(
            un["m_r"], jnp.concatenate([_bd_rows(u, bd), un["bdv"]], axis=0))
        uv = jnp.concatenate([u, un["v"]], axis=0)
        state[i] = state[i] * un["e_last"] + jnp.where(bd, _mm(un["bk_t"], uv), 0.0)
    for i in range(ng):
        s_ref[i] = state[i]
        sl_ref[i] = state[i]

    o = jnp.concatenate([jnp.concatenate(row, axis=1) for row in outs], axis=0)
    inv_n = 1.0 / n
    oc = o - head_sum(o) * inv_n
    var = head_sum(oc * oc) * inv_n
    on = oc * lax.rsqrt(var + C_LNX_EPS) * lg_ref[...] + lb_ref[...]
    out_ref[...] = (on + bonus) * g


def _rwkv(proj, sh8, s0, mu, w0, w2p, a0, a2p, g2, kk, ka, rk, lg, lb, seg, cps):
    b, l, cp = proj.shape
    tt = cps * CHUNK
    w = C_WIDTH
    const = lambda shape: pl.BlockSpec(shape, lambda bi, i: (0,) * len(shape))
    st = (C_HEADS // C_GROUP, MXU_TILE, MXU_TILE)
    return pl.pallas_call(
        functools.partial(_rwkv_kernel, cps=cps),
        grid=(b, l // tt),
        in_specs=[pl.BlockSpec((None, tt, cp), lambda bi, i: (bi, i, 0)),
                  pl.BlockSpec((None, ROW_PAD, cp), lambda bi, i: (bi, 0, 0)),
                  pl.BlockSpec((None,) + st, lambda bi, i: (bi, 0, 0, 0)),
                  const((1, cp)), const((1, w)), const((128, w)), const((1, w)), const((128, w)),
                  const((128, w)), const((1, w)), const((1, w)), const((1, w)), const((1, w)),
                  const((1, w)), const((MXU_TILE, MXU_TILE))],
        out_specs=[pl.BlockSpec((None, tt, w), lambda bi, i: (bi, i, 0)),
                   pl.BlockSpec((None, ROW_PAD, cp), lambda bi, i: (bi, 0, 0)),
                   pl.BlockSpec((None,) + st, lambda bi, i: (bi, 0, 0, 0))],
        out_shape=[jax.ShapeDtypeStruct((b, l, w), F32),
                   jax.ShapeDtypeStruct((b, ROW_PAD, cp), F32),
                   jax.ShapeDtypeStruct((b,) + st, F32)],
        scratch_shapes=[pltpu.VMEM((ROW_PAD + tt, cp), F32), pltpu.VMEM(st, F32)],
        compiler_params=_params(("parallel", "arbitrary")),
        name="rwkv7",
    )(proj, sh8, s0, mu, w0, w2p, a0, a2p, g2, kk, ka, rk, lg, lb, seg)


def _kv_store(c, kr, wk_ref, wv_ref, k_ref, v_ref):
    cb = c.astype(BF16)
    ones = jnp.ones((cb.shape[0], D_V_PAD - D_V), BF16)
    for h in range(D_HEADS):
        sl = slice(h * D_QK_PAD, (h + 1) * D_QK_PAD)
        k_ref[:, sl] = (jnp.dot(cb, wk_ref[:, sl], preferred_element_type=F32) + kr).astype(BF16)
        vh = jnp.dot(cb, wv_ref[:, h * D_V:(h + 1) * D_V], preferred_element_type=F32).astype(BF16)
        v_ref[:, h * D_V_PAD:h * D_V_PAD + D_V] = vh
        v_ref[:, h * D_V_PAD + D_V:(h + 1) * D_V_PAD] = ones


def _layer1_in_kernel(x_ref, wc_ref, wd_ref, ct_ref, st_ref, kc_ref, qg_ref, wq_ref, wqs_ref, kg_ref,
                      wk_ref, wv_ref, cin_ref, q_ref, c_ref, kr_ref, k_ref, v_ref):
    lq = D_Q_LORA
    lkv = D_KV_LORA
    xb = x_ref[...].astype(BF16)
    cin_ref[...] = jnp.dot(xb, wc_ref[...], preferred_element_type=F32)
    d = jnp.dot(xb, wd_ref[...], preferred_element_type=F32)
    qn = _rms_norm(d[:, 0:lq], qg_ref[...]).astype(BF16)
    ct = ct_ref[...]
    st = st_ref[...]
    for h in range(D_HEADS):
        sl = slice(h * D_QK_PAD, (h + 1) * D_QK_PAD)
        q = jnp.dot(qn, wq_ref[:, sl], preferred_element_type=F32) * ct
        q = q + jnp.dot(qn, wqs_ref[:, sl], preferred_element_type=F32) * st
        q_ref[:, sl] = (q * MLA_SCALE).astype(BF16)
    c_new = _rms_norm(d[:, lq:lq + lkv], kg_ref[...])
    o = lq + lkv
    kr = d[:, o:o + D_QK_PAD] * kc_ref[...] + d[:, o + D_QK_PAD:o + 2 * D_QK_PAD] * st
    c_ref[...] = c_new
    kr_ref[...] = kr
    _kv_store(c_new, kr, wk_ref, wv_ref, k_ref, v_ref)


def _layer1_in(x, wc, wd, ctab, stab, kctab, qg, wq, wqs, kg, wk, wv, tm):
    b, l, dm = x.shape
    hq = D_HEADS * D_QK_PAD
    hv = D_HEADS * D_V_PAD
    const = lambda shape: pl.BlockSpec(shape, lambda bi, i: (0,) * len(shape))
    tab = pl.BlockSpec((tm, D_QK_PAD), lambda bi, i: (i, 0))
    rows = lambda width: pl.BlockSpec((None, tm, width), lambda bi, i: (bi, i, 0))
    widths = [(wc.shape[1], F32), (hq, BF16), (D_KV_LORA, F32), (D_QK_PAD, F32), (hq, BF16), (hv, BF16)]
    return pl.pallas_call(
        _layer1_in_kernel,
        grid=(b, l // tm),
        in_specs=[rows(dm), const(wc.shape), const(wd.shape), tab, tab, tab,
                  const((1, D_Q_LORA)), const((D_Q_LORA, hq)), const((D_Q_LORA, hq)), const((1, D_KV_LORA)),
                  const((D_KV_LORA, hq)), const((D_KV_LORA, D_HEADS * D_V))],
        out_specs=[rows(w) for w, _ in widths],
        out_shape=[jax.ShapeDtypeStruct((b, l, w), dt) for w, dt in widths],
        compiler_params=_params(("parallel", "parallel")),
        name="layer1_in",
    )(x, wc, wd, ctab, stab, kctab, qg, wq, wqs, kg, wk, wv)


def _kv_kernel(c_ref, kr_ref, wk_ref, wv_ref, k_ref, v_ref):
    _kv_store(c_ref[...], kr_ref[...], wk_ref, wv_ref, k_ref, v_ref)


def _kv(c, krp, wk, wv, tm):
    b, n, _ = c.shape
    hk = D_HEADS * D_QK_PAD
    hv = D_HEADS * D_V_PAD
    const = lambda shape: pl.BlockSpec(shape, lambda bi, i: (0,) * len(shape))
    rows = lambda width: pl.BlockSpec((None, tm, width), lambda bi, i: (bi, i, 0))
    return pl.pallas_call(
        _kv_kernel,
        grid=(b, n // tm),
        in_specs=[rows(D_KV_LORA), rows(D_QK_PAD), const((D_KV_LORA, hk)), const((D_KV_LORA, D_HEADS * D_V))],
        out_specs=[rows(hk), rows(hv)],
        out_shape=[jax.ShapeDtypeStruct((b, n, hk), BF16), jax.ShapeDtypeStruct((b, n, hv), BF16)],
        compiler_params=_params(("parallel", "parallel")),
        name="mla_kv",
    )(c, krp, wk, wv)


def _attn_kernel(q_ref, k_ref, v_ref, o_ref, m_ref, acc_ref, *, tq, tk, nk, q_off):
    hps = ATTN_HEADS
    q_first = q_off + pl.program_id(2) * tq
    first_chunk_end = q_first // CHUNK * CHUNK + CHUNK
    last_chunk_end = (q_first + tq - 1) // CHUNK * CHUNK + CHUNK
    n_full = jnp.minimum(nk, first_chunk_end // tk)
    m_ref[...] = jnp.full(m_ref.shape, -jnp.inf, F32)
    acc_ref[...] = jnp.zeros(acc_ref.shape, F32)
    qs = [q_ref[:, h * D_QK_PAD:(h + 1) * D_QK_PAD] for h in range(hps)]

    def block(start, width, masked):
        ss = [lax.dot_general(qs[h], k_ref[pl.ds(start, width), h * D_QK_PAD:(h + 1) * D_QK_PAD],
                              _dims(False, True), preferred_element_type=F32) for h in range(hps)]
        if masked:
            q_chunk = jnp.right_shift(q_first + _iota2((tq, width), 0), CHUNK_SHIFT)
            k_chunk = jnp.right_shift(start + _iota2((tq, width), 1), CHUNK_SHIFT)
            ss = [jnp.where(k_chunk <= q_chunk, s, -jnp.inf) for s in ss]
        m_old = [m_ref[h] for h in range(hps)]
        m_new = [jnp.maximum(m_old[h], jnp.max(ss[h], axis=-1, keepdims=True)) for h in range(hps)]
        ps = [jnp.exp(ss[h] - m_new[h]).astype(BF16) for h in range(hps)]
        for h in range(hps):
            acc_ref[h] = jnp.exp(m_old[h] - m_new[h]) * acc_ref[h] + jnp.dot(
                ps[h], v_ref[pl.ds(start, width), h * D_V_PAD:(h + 1) * D_V_PAD], preferred_element_type=F32)
            m_ref[h] = m_new[h]

    def full_body(j, carry):
        block(pl.multiple_of(j * tk, tk), tk, False)
        return carry

    def edge_body(j, carry):
        block(pl.multiple_of(j * tk, tk), tk, True)
        return carry

    edge_len = jnp.minimum(nk * tk, last_chunk_end) - n_full * tk
    n_edge = jnp.maximum(edge_len - 1, 0) // tk
    lax.fori_loop(0, n_full, full_body, 0)
    lax.fori_loop(n_full, n_full + n_edge, edge_body, 0)
    tail = edge_len - n_edge * tk
    tail_start = pl.multiple_of((n_full + n_edge) * tk, tk)
    ew = tk // ATTN_EDGE_SPLIT
    for i in range(1, ATTN_EDGE_SPLIT + 1):
        @pl.when((tail > (i - 1) * ew) & (tail <= i * ew))
        def _(i=i):
            block(tail_start, i * ew, True)

    for h in range(hps):
        acc = acc_ref[h]
        o_ref[:, h * D_V:(h + 1) * D_V] = acc[:, 0:D_V] / acc[:, D_V:D_V + 1]


def _attention(q, k, v, tq, tk, q_off):
    b, l, _ = q.shape
    lk = k.shape[1]
    hps = ATTN_HEADS
    return pl.pallas_call(
        functools.partial(_attn_kernel, tq=tq, tk=tk, nk=lk // tk, q_off=q_off),
        grid=(b, D_HEADS // hps, l // tq),
        in_specs=[pl.BlockSpec((None, tq, hps * D_QK_PAD), lambda bi, h, i: (bi, i, h)),
                  pl.BlockSpec((None, lk, hps * D_QK_PAD), lambda bi, h, i: (bi, 0, h)),
                  pl.BlockSpec((None, lk, hps * D_V_PAD), lambda bi, h, i: (bi, 0, h))],
        out_specs=pl.BlockSpec((None, tq, hps * D_V), lambda bi, h, i: (bi, i, h)),
        out_shape=jax.ShapeDtypeStruct((b, l, D_HEADS * D_V), F32),
        scratch_shapes=[pltpu.VMEM((hps, tq, 1), F32), pltpu.VMEM((hps, tq, D_V_PAD), F32)],
        compiler_params=_params(("parallel", "parallel", "arbitrary")),
        name="mla_attention",
    )(q, k, v)


def _pad_rows_front(buf):
    return jnp.pad(buf, ((0, 0), (ROW_PAD - buf.shape[1], 0), (0, 0)))


def _block_diag(w):
    nb, bi, bj = w.shape
    eye = jnp.eye(nb, dtype=w.dtype)
    return (eye[:, None, :, None] * w[:, :, None, :]).reshape(nb * bi, nb * bj)


def _state_to_block_diag(s, per):
    b, h, n, m = s.shape
    s5 = s.reshape(b, h // per, per, n, m)
    rows = [jnp.pad(s5[:, :, p], ((0, 0), (0, 0), (0, 0), (p * m, (per - 1 - p) * m))) for p in range(per)]
    return jnp.concatenate(rows, axis=2)


def _state_from_block_diag(sbd, per):
    b, g, pn, pm = sbd.shape
    n, m = pn // per, pm // per
    blocks = [sbd[:, :, p * n:(p + 1) * n, p * m:(p + 1) * m] for p in range(per)]
    return jnp.stack(blocks, axis=2).reshape(b, g * per, n, m)


def _rope_swap(w):
    half = w.shape[-1] // 2
    return jnp.concatenate([-w[..., half:], w[..., :half]], axis=-1)


def _prepare(w_in0, a_w_r, a_w_i, a_b_r, a_b_i, w_in1, c_w2, c_a2, d_w_qb, d_w_kvb):
    p = {}
    o = 0
    p["w_a"] = w_in0[:, o:o + 2 * A_WIDTH].astype(BF16); o += 2 * A_WIDTH
    p["w_qkv"] = w_in0[:, o:o + 3 * B_WIDTH].astype(BF16); o += 3 * B_WIDTH
    p["w_z"] = w_in0[:, o:o + B_WIDTH].astype(BF16); o += B_WIDTH
    p["w_gs"] = jnp.pad(w_in0[:, o:o + 2 * B_HEADS], ((0, 0), (0, 128 - 2 * B_HEADS))).astype(BF16)
    p["a_wg"] = jnp.concatenate([_block_diag(a_w_r), _block_diag(a_w_i)], axis=1).astype(BF16)
    p["a_bg"] = jnp.concatenate([a_b_r, a_b_i])[None, :]
    p["w_c"] = w_in1[:, 0:C_PROJ].astype(BF16)
    o = C_PROJ
    w_qa = w_in1[:, o:o + D_Q_LORA]; o += D_Q_LORA
    w_craw = w_in1[:, o:o + D_KV_LORA]; o += D_KV_LORA
    w_kr = w_in1[:, o:o + D_ROPE]
    place = lambda w: jnp.pad(w, ((0, 0), (D_NOPE, D_QK_PAD - D_NOPE - D_ROPE)))
    p["w_d"] = jnp.concatenate([w_qa, w_craw, place(w_kr), place(_rope_swap(w_kr))], axis=1).astype(BF16)
    p["c_w2p"] = jnp.pad(c_w2, ((0, 64), (0, 0)))
    p["c_a2p"] = jnp.pad(c_a2, ((64, 0), (0, 0)))
    wq = d_w_qb.reshape(D_Q_LORA, D_HEADS, D_NOPE + D_ROPE)
    zpad = jnp.zeros((D_Q_LORA, D_HEADS, D_QK_PAD - D_NOPE - D_ROPE), F32)
    p["wq"] = jnp.concatenate([wq, zpad], axis=-1).reshape(D_Q_LORA, D_HEADS * D_QK_PAD).astype(BF16)
    wq_sw = jnp.concatenate([jnp.zeros((D_Q_LORA, D_HEADS, D_NOPE), F32), _rope_swap(wq[..., D_NOPE:]), zpad], axis=-1)
    p["wqs"] = wq_sw.reshape(D_Q_LORA, D_HEADS * D_QK_PAD).astype(BF16)
    wkv = d_w_kvb.reshape(D_KV_LORA, D_HEADS, D_NOPE + D_V)
    wk = jnp.pad(wkv[..., :D_NOPE], ((0, 0), (0, 0), (0, D_QK_PAD - D_NOPE)))
    p["wk"] = wk.reshape(D_KV_LORA, D_HEADS * D_QK_PAD).astype(BF16)
    p["wv"] = wkv[..., D_NOPE:].reshape(D_KV_LORA, D_HEADS * D_V).astype(BF16)
    lane = jnp.arange(MXU_TILE) // C_HEAD_DIM
    p["seg"] = (lane[:, None] == lane[None, :]).astype(BF16)
    return p


def _rope_tables(pos):
    half = D_ROPE // 2
    inv = ROPE_THETA ** (-jnp.arange(half, dtype=F32) / half)
    ang = pos.astype(F32)[:, None] * inv[None, :]
    cos = jnp.cos(ang)
    sin = jnp.sin(ang)
    n = pos.shape[0]
    tail = jnp.zeros((n, D_QK_PAD - D_NOPE - D_ROPE), F32)
    cos2 = jnp.concatenate([cos, cos], axis=1)
    sin2 = jnp.concatenate([sin, sin], axis=1)
    q_cos = jnp.concatenate([jnp.ones((n, D_NOPE), F32), cos2, tail], axis=1)
    q_sin = jnp.concatenate([jnp.zeros((n, D_NOPE), F32), sin2, tail], axis=1)
    k_cos = jnp.concatenate([jnp.zeros((n, D_NOPE), F32), cos2, tail], axis=1)
    return q_cos, q_sin, k_cos


def _trunk(x, pe, pos, a_conv, a_h, b_conv, b_s, c_shift, c_s, d_ckv, d_krope, f_conv, wts, prm):
    (a_conv_w, a_conv_b, a_lambda, b_conv_w, b_a_log, b_dt_bias, b_norm_g, w_out0, c_mu, c_w0,
     c_a0, c_g2, c_k_k, c_k_a, c_r_k, c_lnx_g, c_lnx_b, d_qa_g, d_kva_g, w_out1, ln_mix_g, ln_mix_b,
     ffn_w_up, ffn_conv_w, ffn_conv_b, ffn_w_down, ln_ffn_g, ln_ffn_b, ple_w_gate, ple_b_gate,
     ple_w_proj) = wts
    b, l, d = x.shape
    t = b * l
    tm = min(512, t)
    tt = min(512, l)
    def chunks_per_step(want):
        cps = want
        while l % (cps * CHUNK):
            cps //= 2
        return cps
    row = lambda vec: vec.reshape(1, -1)
    f_new = []

    def ffn_block(i, mix_a, mix_b, xin, w_out):
        y, nbuf = _ffn(mix_a, mix_b, xin.reshape(b, l, d), pe, i, _pad_rows_front(f_conv[i]),
                       w_out.astype(BF16), row(ln_mix_g[i]), row(ln_mix_b[i]),
                       ffn_w_up[i].astype(BF16), ffn_conv_w[i], row(ffn_conv_b[i]),
                       ffn_w_down[i].astype(BF16), row(ln_ffn_g[i]), row(ln_ffn_b[i]),
                       ple_w_gate[i].astype(BF16), row(ple_b_gate[i]), ple_w_proj[i].astype(BF16), tt)
        f_new.append(nbuf[:, ROW_PAD - 2:, :])
        return y.reshape(t, d)

    x2 = x.reshape(t, d)
    a2, qkv, z, gs = _proj(x2, [prm["w_a"], prm["w_qkv"], prm["w_z"], prm["w_gs"]], tm)
    a_out, a_buf, a_hl = _rglru(a2.reshape(b, l, -1), _pad_rows_front(a_conv), a_h[:, None, :],
                                a_conv_w, row(a_conv_b), prm["a_wg"], prm["a_bg"], row(a_lambda),
                                min(256, l))
    pad8 = lambda vec: jnp.pad(vec, (B_HEADS, 128 - 2 * B_HEADS))[None, :]
    b_out, b_buf, b_sl = _gdn(qkv.reshape(b, l, -1), z.reshape(b, l, -1), gs.reshape(b, l, -1),
                              _pad_rows_front(b_conv), _state_to_block_diag(b_s, B_PAIR), b_conv_w,
                              pad8(b_a_log), pad8(b_dt_bias), row(b_norm_g), chunks_per_step(GDN_CPS))
    x2 = ffn_block(0, a_out, b_out, x2, w_out0)

    q_cos, q_sin, k_cos = _rope_tables(pos)
    c_in, q, c_new, kr_pad, k, v = _layer1_in(x2.reshape(b, l, d), prm["w_c"], prm["w_d"], q_cos, q_sin, k_cos,
                                              row(d_qa_g), prm["wq"], prm["wqs"], row(d_kva_g), prm["wk"],
                                              prm["wv"], tt)
    c_out, c_sh, c_sl = _rwkv(c_in, _pad_rows_front(c_shift),
                              _state_to_block_diag(jnp.swapaxes(c_s, 2, 3), C_GROUP), row(c_mu), row(c_w0),
                              prm["c_w2p"], row(c_a0), prm["c_a2p"], c_g2, row(c_k_k), row(c_k_a),
                              row(c_r_k.reshape(-1)), row(c_lnx_g), row(c_lnx_b), prm["seg"],
                              chunks_per_step(RWKV_CPS))
    past = d_ckv.shape[1]
    if past:
        kr_old = jnp.pad(d_krope, ((0, 0), (0, 0), (D_NOPE, D_QK_PAD - D_NOPE - D_ROPE)))
        k_old, v_old = _kv(d_ckv, kr_old, prm["wk"], prm["wv"], min(512, past))
        k = jnp.concatenate([k_old, k], axis=1)
        v = jnp.concatenate([v_old, v], axis=1)
    lk = past + l
    tq = min(ATTN_TQ, l)
    tk = ATTN_TK if lk % ATTN_TK == 0 else lk
    d_out = _attention(q, k, v, tq, tk, past)
    x2 = ffn_block(1, c_out, d_out, x2, w_out1)

    return (x2.reshape(b, l, d), a_buf[:, ROW_PAD - 3:, :], a_hl[:, 0, :], b_buf[:, ROW_PAD - 3:, :],
            _state_from_block_diag(b_sl, B_PAIR), c_sh[:, ROW_PAD - 1:, :],
            jnp.swapaxes(_state_from_block_diag(c_sl, C_GROUP), 2, 3), c_new,
            kr_pad[:, :, D_NOPE:D_NOPE + D_ROPE],
            jnp.stack(f_new))


def kernel(x_prompt, x_sample, state_a_conv, state_a_h, state_b_conv, state_b_s, state_c_shift, state_c_s, cache_d_ckv, cache_d_krope, state_ffn_conv, p_prompt, p_sample, w_in0, a_conv_w, a_conv_b, a_w_r, a_b_r, a_w_i, a_b_i, a_lambda, b_conv_w, b_a_log, b_dt_bias, b_norm_g, w_out0, w_in1, c_mu, c_w0, c_w2, c_a0, c_a2, c_g2, c_k_k, c_k_a, c_r_k, c_lnx_g, c_lnx_b, d_qa_g, d_w_qb, d_kva_g, d_w_kvb, w_out1, ln_mix_g, ln_mix_b, ffn_w_up, ffn_conv_w, ffn_conv_b, ffn_w_down, ln_ffn_g, ln_ffn_b, ple_w_gate, ple_b_gate, ple_w_proj):
    prm = _prepare(w_in0, a_w_r, a_w_i, a_b_r, a_b_i, w_in1, c_w2, c_a2, d_w_qb, d_w_kvb)
    wts = (a_conv_w, a_conv_b, a_lambda, b_conv_w, b_a_log, b_dt_bias, b_norm_g, w_out0, c_mu, c_w0,
           c_a0, c_g2, c_k_k, c_k_a, c_r_k, c_lnx_g, c_lnx_b, d_qa_g, d_kva_g, w_out1, ln_mix_g, ln_mix_b,
           ffn_w_up, ffn_conv_w, ffn_conv_b, ffn_w_down, ln_ffn_g, ln_ffn_b, ple_w_gate, ple_b_gate,
           ple_w_proj)
    dt = x_prompt.dtype
    bp, lp = x_prompt.shape[0], x_prompt.shape[1]
    zeros = lambda *shape: jnp.zeros(shape, dt)
    prompt = _trunk(x_prompt, p_prompt, jnp.arange(lp),
                    zeros(bp, 3, A_WIDTH), zeros(bp, A_WIDTH), zeros(bp, 3, 3 * B_WIDTH),
                    zeros(bp, B_HEADS, B_HEAD_DIM, B_HEAD_DIM), zeros(bp, 1, C_PROJ),
                    zeros(bp, C_HEADS, C_HEAD_DIM, C_HEAD_DIM), zeros(bp, 0, D_KV_LORA),
                    zeros(bp, 0, D_ROPE), zeros(DEPTH, bp, 2, 2 * D_FF), wts, prm)
    ls = x_sample.shape[1]
    past = cache_d_ckv.shape[1]
    sample = _trunk(x_sample, p_sample, past + jnp.arange(ls), state_a_conv, state_a_h, state_b_conv,
                    state_b_s, state_c_shift, state_c_s, cache_d_ckv, cache_d_krope, state_ffn_conv,
                    wts, prm)
    out = [prompt[0], sample[0]]
    for ps, ss in zip(prompt[1:], sample[1:]):
        out += [ps, ss]
    return tuple(out)
```

```python
import functools

import jax
import jax.numpy as jnp
from jax import lax
from jax.experimental import pallas as pl
from jax.experimental.pallas import tpu as pltpu

F32 = jnp.float32
BF16 = jnp.bfloat16

D_MODEL = 1024
DEPTH = 2
CHUNK = 64
CHUNK_SHIFT = CHUNK.bit_length() - 1
PLE_DIM = 256
ALPHA = (2 * DEPTH) ** 0.25
NORM_EPS = 1e-5
A_WIDTH = 512
A_BLOCKS = 8
A_C = 8.0
SQRT_FLOOR = 1e-30
B_HEADS = 4
B_HEAD_DIM = 128
B_WIDTH = 512
C_HEAD_DIM = 64
C_WIDTH = 512
C_HEADS = 8
C_LNX_EPS = 64e-5
C_PROJ = 3 * C_WIDTH + 64 + 64 + 128
D_HEADS = 4
D_Q_LORA = 384
D_KV_LORA = 256
D_NOPE = 128
D_ROPE = 64
D_V = 128
D_QK_PAD = 256
D_V_PAD = 256
MLA_SCALE = (D_NOPE + D_ROPE) ** -0.5
ROPE_THETA = 10000.0
D_FF = 2816
FF_BLOCK = 256
FFN_ROWS = 512
TRI_BASE = 8
MXU_TILE = 256
ROW_PAD = 8
VMEM_LIMIT = 56 * 1024 * 1024
ATTN_HEADS = 2
ATTN_TQ = 512
GDN_CPS = 8
RWKV_CPS = 4
ATTN_EDGE_SPLIT = 4
ATTN_TK = 2048
B_PAIR = MXU_TILE // B_HEAD_DIM
C_GROUP = MXU_TILE // C_HEAD_DIM


def _params(sem):
    return pltpu.CompilerParams(dimension_semantics=sem, vmem_limit_bytes=VMEM_LIMIT)


def _sigmoid(x):
    return 1.0 / (1.0 + jnp.exp(-x))


def _softplus(x):
    return jnp.maximum(x, 0.0) + jnp.log(1.0 + jnp.exp(-jnp.abs(x)))


def _silu(x):
    return x * _sigmoid(x)


def _gelu_tanh(x):
    return 0.5 * x * (1.0 + jnp.tanh(0.7978845608028654 * (x + 0.044715 * x * x * x)))


def _dims(trans_a, trans_b):
    return (((0 if trans_a else 1,), (1 if trans_b else 0,)), ((), ()))


def _mm(a, b, trans_a=False, trans_b=False):
    return lax.dot_general(a.astype(BF16), b.astype(BF16), _dims(trans_a, trans_b),
                           preferred_element_type=F32)


def _split3(a):
    a1 = a.astype(BF16)
    r1 = a - a1.astype(F32)
    a2 = r1.astype(BF16)
    return a1, a2, (r1 - a2.astype(F32)).astype(BF16)


def _mm_masked_sum(a, mask):
    a1 = a.astype(BF16)
    a2 = (a - a1.astype(F32)).astype(BF16)
    mb = mask.astype(BF16)
    return jnp.dot(a2, mb, preferred_element_type=F32) + jnp.dot(a1, mb, preferred_element_type=F32)


def _mm_exact_lhs(mask, b):
    b1, b2, b3 = _split3(b)
    mb = mask.astype(BF16)
    out = jnp.dot(mb, b3, preferred_element_type=F32)
    out = out + jnp.dot(mb, b2, preferred_element_type=F32)
    return out + jnp.dot(mb, b1, preferred_element_type=F32)


def _iota2(shape, axis):
    return lax.broadcasted_iota(jnp.int32, shape, axis)


def _blk(idx, size):
    return jnp.right_shift(idx, size.bit_length() - 1)


def _bd_mask(shape, row_block, col_block):
    return _blk(_iota2(shape, 0), row_block) == _blk(_iota2(shape, 1), col_block)


def _bd_rows(x, mask):
    reps = mask.shape[0] // x.shape[0]
    return jnp.where(mask, jnp.concatenate([x] * reps, axis=0), 0.0)


def _tri_inv_cat(xs, bd):
    c, wd = xs[0].shape
    row = _iota2((c, wd), 0)
    col = _iota2((c, wd), 1) & (c - 1)
    eye = (row == col).astype(F32)

    def prod(a, b):
        return jnp.dot(a.astype(BF16), _bd_rows(b, bd).astype(BF16), preferred_element_type=F32)

    base = _blk(row, TRI_BASE) == _blk(col, TRI_BASE)
    ps = [jnp.where(base, x, 0.0) for x in xs]
    ts = [eye + p for p in ps]
    k = 2
    while k < TRI_BASE:
        ps = [prod(p, p) for p in ps]
        ts = [t + prod(t, p) for t, p in zip(ts, ps)]
        k *= 2
    size = TRI_BASE
    while size < c:
        pair = (_blk(row, 2 * size) == _blk(col, 2 * size)) & (_blk(row, size) != _blk(col, size))
        offs = [prod(t, jnp.where(pair, x, 0.0)) for t, x in zip(ts, xs)]
        ts = [t + prod(o, t) for t, o in zip(ts, offs)]
        size *= 2
    return ts


def _layer_norm(x, g, b):
    mu = jnp.mean(x, axis=-1, keepdims=True)
    xc = x - mu
    var = jnp.mean(xc * xc, axis=-1, keepdims=True)
    return xc * lax.rsqrt(var + NORM_EPS) * g + b


def _rms_norm(x, g, eps=1e-6):
    return x * lax.rsqrt(jnp.mean(x * x, axis=-1, keepdims=True) + eps) * g


def _proj_kernel(x_ref, *refs):
    n = len(refs) // 2
    xb = x_ref[...].astype(BF16)
    for w_ref, o_ref in zip(refs[:n], refs[n:]):
        o_ref[...] = jnp.dot(xb, w_ref[...], preferred_element_type=F32)


def _proj(x, ws, tm):
    t, k = x.shape
    return pl.pallas_call(
        _proj_kernel,
        grid=(t // tm,),
        in_specs=[pl.BlockSpec((tm, k), lambda i: (i, 0))]
        + [pl.BlockSpec(w.shape, lambda i: (0, 0)) for w in ws],
        out_specs=[pl.BlockSpec((tm, w.shape[1]), lambda i: (i, 0)) for w in ws],
        out_shape=[jax.ShapeDtypeStruct((t, w.shape[1]), F32) for w in ws],
        compiler_params=_params(("parallel",)),
        name="proj",
    )(x, *ws)


def _scan_rows(a, b):
    n = a.shape[0]
    row = _iota2(a.shape, 0)
    s = 1
    while s < n:
        a_sh = pltpu.roll(a, s, 0)
        b_sh = pltpu.roll(b, s, 0)
        m = row >= s
        b = jnp.where(m, a * b_sh + b, b)
        a = jnp.where(m, a * a_sh, a)
        s *= 2
    return a, b


def _rglru_kernel(a2_ref, buf_ref, h0_ref, cw_ref, cb_ref, wg_ref, bg_ref, lam_ref,
                  out_ref, nbuf_ref, hl_ref, xp_ref, hc_ref, *, tt):
    w = A_WIDTH

    @pl.when(pl.program_id(1) == 0)
    def _():
        xp_ref[0:ROW_PAD, :] = buf_ref[...]
        hc_ref[...] = h0_ref[...]

    gate_in = a2_ref[:, 0:w]
    x_in = a2_ref[:, w:2 * w]
    xp_ref[ROW_PAD:ROW_PAD + tt, :] = x_in
    xc = x_in * cw_ref[3:4, :] + cb_ref[...]
    for j in range(3):
        xc = xc + xp_ref[ROW_PAD - 3 + j:ROW_PAD - 3 + j + tt, :] * cw_ref[j:j + 1, :]
    gates = jnp.dot(xc.astype(BF16), wg_ref[...], preferred_element_type=F32) + bg_ref[...]
    r = _sigmoid(gates[:, 0:w])
    ig = _sigmoid(gates[:, w:2 * w])
    log_a = (-A_C) * r * _softplus(-lam_ref[...])
    a = jnp.exp(log_a)
    y = 1.0 - a * a
    u = y * lax.rsqrt(jnp.maximum(y, SQRT_FLOOR)) * (ig * xc)
    a_cum, h_loc = _scan_rows(a, u)
    h = a_cum * hc_ref[...] + h_loc
    out_ref[...] = h * _gelu_tanh(gate_in)
    hc_ref[...] = h[tt - 1:tt, :]
    hl_ref[...] = h[tt - 1:tt, :]
    last = x_in[tt - ROW_PAD:tt, :]
    xp_ref[0:ROW_PAD, :] = last
    nbuf_ref[...] = last


def _rglru(a2, buf8, h0, cw, cb, wg, bg, lam, tt):
    b, l, _ = a2.shape
    w = A_WIDTH
    const = lambda shape: pl.BlockSpec(shape, lambda bi, i: (0,) * len(shape))
    return pl.pallas_call(
        functools.partial(_rglru_kernel, tt=tt),
        grid=(b, l // tt),
        in_specs=[pl.BlockSpec((None, tt, 2 * w), lambda bi, i: (bi, i, 0)),
                  pl.BlockSpec((None, ROW_PAD, w), lambda bi, i: (bi, 0, 0)),
                  pl.BlockSpec((None, 1, w), lambda bi, i: (bi, 0, 0)),
                  const((4, w)), const((1, w)), const((w, 2 * w)), const((1, 2 * w)), const((1, w))],
        out_specs=[pl.BlockSpec((None, tt, w), lambda bi, i: (bi, i, 0)),
                   pl.BlockSpec((None, ROW_PAD, w), lambda bi, i: (bi, 0, 0)),
                   pl.BlockSpec((None, 1, w), lambda bi, i: (bi, 0, 0))],
        out_shape=[jax.ShapeDtypeStruct((b, l, w), F32),
                   jax.ShapeDtypeStruct((b, ROW_PAD, w), F32),
                   jax.ShapeDtypeStruct((b, 1, w), F32)],
        scratch_shapes=[pltpu.VMEM((ROW_PAD + tt, w), F32), pltpu.VMEM((1, w), F32)],
        compiler_params=_params(("parallel", "arbitrary")),
        name="rglru",
    )(a2, buf8, h0, cw, cb, wg, bg, lam)


def _gdn_kernel(qkv_ref, z_ref, gs_ref, buf_ref, s0_ref, cw_ref, alog_ref, dtb_ref, ng_ref,
                out_ref, nbuf_ref, sl_ref, xp_ref, s_ref, *, cps):
    c = CHUNK
    dk = B_HEAD_DIM
    nh = B_HEADS
    hw = nh * dk
    gw = MXU_TILE
    tt = cps * c

    @pl.when(pl.program_id(1) == 0)
    def _():
        xp_ref[0:ROW_PAD, :] = buf_ref[...]
        s_ref[...] = s0_ref[...]

    x_in = qkv_ref[...]
    xp_ref[ROW_PAD:ROW_PAD + tt, :] = x_in
    xc = x_in * cw_ref[3:4, :]
    for j in range(3):
        xc = xc + xp_ref[ROW_PAD - 3 + j:ROW_PAD - 3 + j + tt, :] * cw_ref[j:j + 1, :]
    xc = _silu(xc)
    last = x_in[tt - ROW_PAD:tt, :]
    xp_ref[0:ROW_PAD, :] = last
    nbuf_ref[...] = last

    def unit(x, scale):
        return x * (lax.rsqrt(jnp.sum(x * x, axis=-1, keepdims=True) + 1e-6) * scale)

    q4 = jnp.concatenate([unit(xc[:, h * dk:(h + 1) * dk], dk ** -0.5) for h in range(nh)], axis=1)
    k4 = jnp.concatenate([unit(xc[:, hw + h * dk:hw + (h + 1) * dk], 1.0) for h in range(nh)], axis=1)
    v4 = xc[:, 2 * hw:3 * hw]
    gs = gs_ref[...]
    beta_all = _sigmoid(gs)
    g_all = -jnp.exp(alog_ref[...]) * _softplus(gs + dtb_ref[...])

    def per_head(x, first):
        return jnp.concatenate([jnp.broadcast_to(x[:, first + h:first + h + 1], (x.shape[0], dk))
                                for h in range(nh)], axis=1)

    ri = _iota2((c, c), 0)
    ci = _iota2((c, c), 1)
    tri = (ri >= ci).astype(BF16)
    upper = (ri > ci).astype(F32)
    row_i = _iota2((c, gw), 0)
    lane_j = _iota2((c, gw), 1) & (c - 1)
    lane_head = _blk(_iota2((c, gw), 1), c)
    low_i = row_i >= lane_j
    low_s = row_i > lane_j
    bd_tok = _bd_mask((gw, gw), c, c)
    bd_kt = _bd_mask((gw, hw), c, dk)
    bd_vn = _bd_mask((B_PAIR * c, gw), c, dk)
    bd_st = _bd_mask((gw, gw), dk, dk)

    pre = []
    for cc in range(cps):
        rs = slice(cc * c, (cc + 1) * c)
        g_c = g_all[rs]
        gmat = _mm_exact_lhs(tri, jnp.concatenate(
            [g_c[:, nh + h:nh + h + 1] * upper for h in range(nh)] + [g_c], axis=1))
        decay = jnp.where(low_i, jnp.exp(gmat[:, 0:gw]), 0.0)
        gc_all = gmat[:, gw:gw + 128]
        gc4 = per_head(gc_all, nh)
        gl4 = per_head(gc_all[c - 1:c, :], nh)
        egc4 = jnp.exp(gc4)
        beta4 = per_head(beta_all[rs], 0)
        kc = k4[rs]
        qc = q4[rs]
        kb = kc * beta4
        gkq = _mm(jnp.concatenate([kb, qc], axis=0), _bd_rows(kc, bd_kt), trans_b=True)
        pre.append(dict(lmat=jnp.where(low_s, gkq[0:c] * decay, 0.0), qk=gkq[c:2 * c] * decay,
                        vb=v4[rs] * beta4, kbe=kb * egc4, q_in=qc * egc4,
                        k_out_t=jnp.transpose(kc * jnp.exp(gl4 - gc4)), egl=jnp.exp(gl4)))
    tmats = _tri_inv_cat([-p["lmat"] for p in pre], bd_tok)
    for p, tmat in zip(pre, tmats):
        us, ws = [], []
        for h in range(nh):
            hs = slice(h * dk, (h + 1) * dk)
            rhs = jnp.concatenate([p["vb"][:, hs], p["kbe"][:, hs]], axis=1)
            uw = _mm(jnp.where(lane_head == h, tmat, 0.0), jnp.concatenate([rhs] * nh, axis=0))
            us.append(uw[:, 0:dk])
            ws.append(uw[:, dk:2 * dk])
        p["u"] = jnp.concatenate(us, axis=1)
        p["w"] = jnp.concatenate(ws, axis=1)

    state = [s_ref[i] for i in range(nh // B_PAIR)]
    for cc, p in enumerate(pre):
        rs = slice(cc * c, (cc + 1) * c)
        outs = []
        for i in range(nh // B_PAIR):
            ps = slice(i * gw, (i + 1) * gw)
            ws = _mm(jnp.concatenate([p["w"][:, ps], p["q_in"][:, ps]], axis=0), state[i])
            v_new = p["u"][:, ps] - ws[0:c]
            qk = p["qk"][:, i * B_PAIR * c:(i + 1) * B_PAIR * c]
            outs.append(ws[c:2 * c] + _mm(qk, _bd_rows(v_new, bd_vn)))
            state[i] = state[i] * p["egl"][:, ps] + jnp.where(bd_st, _mm(p["k_out_t"][ps, :], v_new), 0.0)
        o = jnp.concatenate(outs, axis=1)
        for h in range(nh):
            hs = slice(h * dk, (h + 1) * dk)
            out_ref[rs, hs] = _rms_norm(o[:, hs], ng_ref[...]) * _silu(z_ref[rs, hs])
    for i in range(nh // B_PAIR):
        s_ref[i] = state[i]
        sl_ref[i] = state[i]


def _gdn(qkv, z, gs, buf8, s0, cw, alog, dtb, ng, cps):
    b, l, _ = qkv.shape
    tt = cps * CHUNK
    w3 = 3 * B_WIDTH
    const = lambda shape: pl.BlockSpec(shape, lambda bi, i: (0,) * len(shape))
    st = (B_HEADS // B_PAIR, MXU_TILE, MXU_TILE)
    return pl.pallas_call(
        functools.partial(_gdn_kernel, cps=cps),
        grid=(b, l // tt),
        in_specs=[pl.BlockSpec((None, tt, w3), lambda bi, i: (bi, i, 0)),
                  pl.BlockSpec((None, tt, B_WIDTH), lambda bi, i: (bi, i, 0)),
                  pl.BlockSpec((None, tt, 128), lambda bi, i: (bi, i, 0)),
                  pl.BlockSpec((None, ROW_PAD, w3), lambda bi, i: (bi, 0, 0)),
                  pl.BlockSpec((None,) + st, lambda bi, i: (bi, 0, 0, 0)),
                  const((4, w3)), const((1, 128)), const((1, 128)), const((1, B_HEAD_DIM))],
        out_specs=[pl.BlockSpec((None, tt, B_WIDTH), lambda bi, i: (bi, i, 0)),
                   pl.BlockSpec((None, ROW_PAD, w3), lambda bi, i: (bi, 0, 0)),
                   pl.BlockSpec((None,) + st, lambda bi, i: (bi, 0, 0, 0))],
        out_shape=[jax.ShapeDtypeStruct((b, l, B_WIDTH), F32),
                   jax.ShapeDtypeStruct((b, ROW_PAD, w3), F32),
                   jax.ShapeDtypeStruct((b,) + st, F32)],
        scratch_shapes=[pltpu.VMEM((ROW_PAD + tt, w3), F32), pltpu.VMEM(st, F32)],
        compiler_params=_params(("parallel", "arbitrary")),
        name="gdn",
    )(qkv, z, gs, buf8, s0, cw, alog, dtb, ng)


def _ffn_kernel(ma_ref, mb_ref, x_ref, pe_ref, buf_ref, wo_ref, g0_ref, b0_ref, wup_ref, cw_ref, cb_ref,
                wdn_ref, g_ref, b_ref, wg_ref, bg_ref, wp_ref, o_ref, nbuf_ref, prev_ref, act_ref, *, tt, nseq):
    ff = D_FF
    nb = ff // FF_BLOCK
    half = ma_ref.shape[1]
    ls = tt // nseq

    @pl.when(pl.program_id(1) == 0)
    def _():
        prev_ref[...] = buf_ref[...]

    mix = jnp.dot(ma_ref[...].astype(BF16), wo_ref[0:half, :], preferred_element_type=F32)
    mix = mix + jnp.dot(mb_ref[...].astype(BF16), wo_ref[half:2 * half, :], preferred_element_type=F32)
    x = _layer_norm(ALPHA * x_ref[...] + mix, g0_ref[...], b0_ref[...])
    xb = x.astype(BF16)
    row = _iota2((tt, FF_BLOCK), 0) & (ls - 1)

    def prev_row(r, sl):
        if nseq == 1:
            return prev_ref[0, r:r + 1, sl]
        return jnp.concatenate([jnp.broadcast_to(prev_ref[s, r:r + 1, sl], (ls, FF_BLOCK)) for s in range(nseq)],
                               axis=0)

    def conv(col):
        sl = slice(col, col + FF_BLOCK)
        h = jnp.dot(xb, wup_ref[:, sl], preferred_element_type=F32)
        p1 = prev_row(ROW_PAD - 1, sl)
        p2 = prev_row(ROW_PAD - 2, sl)
        h1 = jnp.where(row == 0, p1, pltpu.roll(h, 1, 0))
        h2 = jnp.where(row == 0, p2, jnp.where(row == 1, p1, pltpu.roll(h, 2, 0)))
        for s in range(nseq):
            prev_ref[s, :, sl] = h[(s + 1) * ls - ROW_PAD:(s + 1) * ls, :]
        return h2 * cw_ref[0:1, sl] + h1 * cw_ref[1:2, sl] + h * cw_ref[2:3, sl] + cb_ref[:, sl]

    for j in range(nb):
        gate = conv(j * FF_BLOCK)
        val = conv(ff + j * FF_BLOCK)
        act_ref[:, j * FF_BLOCK:(j + 1) * FF_BLOCK] = (_silu(gate) * val).astype(BF16)
    nbuf_ref[...] = prev_ref[...]
    f = jnp.dot(act_ref[...], wdn_ref[...], preferred_element_type=F32)
    y = _layer_norm(ALPHA * x + f, g_ref[...], b_ref[...])
    gate = _sigmoid(jnp.dot(y.astype(BF16), wg_ref[...], preferred_element_type=F32) + bg_ref[...])
    emb = jnp.dot(pe_ref[...].astype(BF16), wp_ref[...], preferred_element_type=F32)
    o_ref[...] = y + gate * emb


def _ffn(ma, mb, x, pe, layer, buf8, wo, g0, b0, wup, cw, cb, wdn, g, bb, wg, bg, wp, tt, nseq):
    b, l, d = x.shape
    half = ma.shape[2]
    ff2 = 2 * D_FF
    groups, rows = b // nseq, nseq * l
    assert nseq == 1 or tt == rows, "several sequences per tile only when the tile holds them whole"
    assert (tt // nseq) & (tt // nseq - 1) == 0, "rows per sequence in a tile must be a power of two"
    fold = lambda a: a.reshape(groups, rows, a.shape[-1])
    const = lambda shape: pl.BlockSpec(shape, lambda bi, i: (0,) * len(shape),
                                       pipeline_mode=pl.Buffered(1))
    tile = lambda width: pl.BlockSpec((None, tt, width), lambda bi, i: (bi, i, 0))
    carried = pl.BlockSpec((None, nseq, ROW_PAD, ff2), lambda bi, i: (bi, 0, 0, 0))
    y, nbuf = pl.pallas_call(
        functools.partial(_ffn_kernel, tt=tt, nseq=nseq),
        grid=(groups, rows // tt),
        in_specs=[tile(half), tile(half), tile(d),
                  pl.BlockSpec((None, None, tt, PLE_DIM), lambda bi, i: (layer, bi, i, 0)),
                  carried,
                  const((2 * half, d)), const((1, d)), const((1, d)),
                  const((d, ff2)), const((3, ff2)), const((1, ff2)), const((D_FF, d)),
                  const((1, d)), const((1, d)), const((d, d)), const((1, d)), const((PLE_DIM, d))],
        out_specs=[tile(d), carried],
        out_shape=[jax.ShapeDtypeStruct((groups, rows, d), F32),
                   jax.ShapeDtypeStruct((groups, nseq, ROW_PAD, ff2), F32)],
        scratch_shapes=[pltpu.VMEM((nseq, ROW_PAD, ff2), F32), pltpu.VMEM((tt, D_FF), BF16)],
        compiler_params=_params(("parallel", "arbitrary")),
        name="ffn",
    )(fold(ma), fold(mb), fold(x), pe.reshape(pe.shape[0], groups, rows, PLE_DIM),
      buf8.reshape(groups, nseq, ROW_PAD, ff2), wo, g0, b0, wup, cw, cb, wdn, g, bb, wg, bg, wp)
    return y.reshape(b, l, d), nbuf.reshape(b, ROW_PAD, ff2)


def _rwkv_kernel(p_ref, sh_ref, s0_ref, mu_ref, w0_ref, w2_ref, a0_ref, a2_ref, g2_ref, kk_ref,
                 ka_ref, rk_ref, lg_ref, lb_ref, seg_ref,
                 out_ref, nsh_ref, sl_ref, xp_ref, s_ref, *, cps):
    c = CHUNK
    w = C_WIDTH
    n = C_HEAD_DIM
    gw = MXU_TILE
    ng = w // gw
    tt = cps * c

    @pl.when(pl.program_id(1) == 0)
    def _():
        xp_ref[0:ROW_PAD, :] = sh_ref[...]
        s_ref[...] = s0_ref[...]

    proj = p_ref[...]
    xp_ref[ROW_PAD:ROW_PAD + tt, :] = proj
    prev = xp_ref[ROW_PAD - 1:ROW_PAD - 1 + tt, :]
    last = proj[tt - ROW_PAD:tt, :]
    xp_ref[0:ROW_PAD, :] = last
    nsh_ref[...] = last
    xs = proj + (prev - proj) * mu_ref[...]

    seg = seg_ref[...]

    def head_sum(x):
        return jnp.concatenate([_mm_masked_sum(x[:, i * gw:(i + 1) * gw], seg) for i in range(ng)], axis=1)

    r = xs[:, 0:w]
    k = xs[:, w:2 * w]
    v = xs[:, 2 * w:3 * w]
    lora_in = xs[:, 3 * w:3 * w + 128]
    ww = -_softplus(-(w0_ref[...] + _mm(jnp.tanh(lora_in), w2_ref[...]))) - 0.5
    a = _sigmoid(a0_ref[...] + _mm(lora_in, a2_ref[...]))
    g = _mm(_sigmoid(xs[:, 3 * w + 128:3 * w + 256]), g2_ref[...])
    kx = k * kk_ref[...]
    k = k * (1.0 + (a - 1.0) * ka_ref[...])
    sums = head_sum(jnp.concatenate([kx * kx, r * k * rk_ref[...]], axis=0))
    kk = kx * lax.rsqrt(sums[0:tt] + 1e-6)
    bonus = sums[tt:2 * tt] * v
    lw = -jnp.exp(ww)
    sa = -kk
    sb = kk * a

    ri = _iota2((c, c), 0)
    ci = _iota2((c, c), 1)
    tri = (ri >= ci).astype(BF16)
    row_i = _iota2((c, gw), 0)
    lane_j = _iota2((c, gw), 1) & (c - 1)
    low_i = row_i >= lane_j
    low_s = row_i > lane_j
    bd = _bd_mask((gw, gw), n, n)

    units = []
    for cc in range(cps):
        rs = slice(cc * c, (cc + 1) * c)
        lw_c = lw[rs]
        cum = _mm_exact_lhs(tri, lw_c)
        cum_last = cum[c - 1:c, :]
        e_out = jnp.exp(-cum)
        e_end = jnp.exp(cum_last - cum)
        a_t = sa[rs] * jnp.exp(cum - lw_c)
        r_t = r[rs] * jnp.exp(cum)
        b_t = sb[rs] * e_out
        k_t = k[rs] * e_out
        b_o = sb[rs] * e_end
        k_o = k[rs] * e_end
        e_last = jnp.exp(cum_last)
        for i in range(ng):
            gs = slice(i * gw, (i + 1) * gw)
            bk = jnp.concatenate([b_o[:, gs], k_o[:, gs]], axis=0)
            e_col = jnp.transpose(jnp.broadcast_to(e_last[:, gs], (2 * c, gw)))
            units.append(dict(cc=cc, i=i, lr=jnp.concatenate([a_t[:, gs], r_t[:, gs]], axis=0),
                              b_t=b_t[:, gs], k_t=k_t[:, gs], v=v[rs, gs], bk_t=jnp.transpose(bk),
                              e_last=jnp.concatenate([e_col] * (gw // (2 * c)), axis=1)))
    for un in units:
        g1 = _mm(un["lr"], _bd_rows(un["b_t"], bd), trans_b=True)
        g2 = _mm(un["lr"], _bd_rows(un["k_t"], bd), trans_b=True)
        un["m_ab"] = jnp.where(low_s, g1[0:c], 0.0)
        un["m_ak"] = jnp.where(low_s, g2[0:c], 0.0)
        un["m_r"] = jnp.concatenate([jnp.where(low_i, g1[c:2 * c], 0.0),
                                     jnp.where(low_i, g2[c:2 * c], 0.0)], axis=1)
        un["bdv"] = _bd_rows(un["v"], bd)
    for un, tmat in zip(units, _tri_inv_cat([un["m_ab"] for un in units], bd)):
        un["tmat"] = tmat
        un["akv"] = _mm(un["m_ak"], un["bdv"])

    state = [s_ref[i] for i in range(ng)]
    outs = [[None] * ng for _ in range(cps)]
    for un in units:
        i = un["i"]
        ls = _mm(un["lr"], state[i])
        u = _mm(un["tmat"], _bd_rows(ls[0:c] + un["akv"], bd))
        outs[un["cc"]][i] = ls[c:2 * c] + _mm(
            un["m_r"], jnp.concatenate([_bd_rows(u, bd), un["bdv"]], axis=0))
        uv = jnp.concatenate([u, un["v"]], axis=0)
        state[i] = state[i] * un["e_last"] + jnp.where(bd, _mm(un["bk_t"], uv), 0.0)
    for i in range(ng):
        s_ref[i] = state[i]
        sl_ref[i] = state[i]

    o = jnp.concatenate([jnp.concatenate(row, axis=1) for row in outs], axis=0)
    inv_n = 1.0 / n
    oc = o - head_sum(o) * inv_n
    var = head_sum(oc * oc) * inv_n
    on = oc * lax.rsqrt(var + C_LNX_EPS) * lg_ref[...] + lb_ref[...]
    out_ref[...] = (on + bonus) * g


def _rwkv(proj, sh8, s0, mu, w0, w2p, a0, a2p, g2, kk, ka, rk, lg, lb, seg, cps):
    b, l, cp = proj.shape
    tt = cps * CHUNK
    w = C_WIDTH
    const = lambda shape: pl.BlockSpec(shape, lambda bi, i: (0,) * len(shape))
    st = (C_HEADS // C_GROUP, MXU_TILE, MXU_TILE)
    return pl.pallas_call(
        functools.partial(_rwkv_kernel, cps=cps),
        grid=(b, l // tt),
        in_specs=[pl.BlockSpec((None, tt, cp), lambda bi, i: (bi, i, 0)),
                  pl.BlockSpec((None, ROW_PAD, cp), lambda bi, i: (bi, 0, 0)),
                  pl.BlockSpec((None,) + st, lambda bi, i: (bi, 0, 0, 0)),
                  const((1, cp)), const((1, w)), const((128, w)), const((1, w)), const((128, w)),
                  const((128, w)), const((1, w)), const((1, w)), const((1, w)), const((1, w)),
                  const((1, w)), const((MXU_TILE, MXU_TILE))],
        out_specs=[pl.BlockSpec((None, tt, w), lambda bi, i: (bi, i, 0)),
                   pl.BlockSpec((None, ROW_PAD, cp), lambda bi, i: (bi, 0, 0)),
                   pl.BlockSpec((None,) + st, lambda bi, i: (bi, 0, 0, 0))],
        out_shape=[jax.ShapeDtypeStruct((b, l, w), F32),
                   jax.ShapeDtypeStruct((b, ROW_PAD, cp), F32),
                   jax.ShapeDtypeStruct((b,) + st, F32)],
        scratch_shapes=[pltpu.VMEM((ROW_PAD + tt, cp), F32), pltpu.VMEM(st, F32)],
        compiler_params=_params(("parallel", "arbitrary")),
        name="rwkv7",
    )(proj, sh8, s0, mu, w0, w2p, a0, a2p, g2, kk, ka, rk, lg, lb, seg)


def _kv_store(c, kr, wk_ref, wv_ref, k_ref, v_ref):
    cb = c.astype(BF16)
    ones = jnp.ones((cb.shape[0], D_V_PAD - D_V), BF16)
    for h in range(D_HEADS):
        sl = slice(h * D_QK_PAD, (h + 1) * D_QK_PAD)
        k_ref[:, sl] = (jnp.dot(cb, wk_ref[:, sl], preferred_element_type=F32) + kr).astype(BF16)
        vh = jnp.dot(cb, wv_ref[:, h * D_V:(h + 1) * D_V], preferred_element_type=F32).astype(BF16)
        v_ref[:, h * D_V_PAD:h * D_V_PAD + D_V] = vh
        v_ref[:, h * D_V_PAD + D_V:(h + 1) * D_V_PAD] = ones


def _layer1_in_kernel(x_ref, wc_ref, wd_ref, ct_ref, st_ref, kc_ref, qg_ref, wq_ref, wqs_ref, kg_ref,
                      wk_ref, wv_ref, cin_ref, q_ref, c_ref, kr_ref, k_ref, v_ref):
    lq = D_Q_LORA
    lkv = D_KV_LORA
    xb = x_ref[...].astype(BF16)
    cin_ref[...] = jnp.dot(xb, wc_ref[...], preferred_element_type=F32)
    d = jnp.dot(xb, wd_ref[...], preferred_element_type=F32)
    qn = _rms_norm(d[:, 0:lq], qg_ref[...]).astype(BF16)
    ct = ct_ref[...]
    st = st_ref[...]
    for h in range(D_HEADS):
        sl = slice(h * D_QK_PAD, (h + 1) * D_QK_PAD)
        q = jnp.dot(qn, wq_ref[:, sl], preferred_element_type=F32) * ct
        q = q + jnp.dot(qn, wqs_ref[:, sl], preferred_element_type=F32) * st
        q_ref[:, sl] = (q * MLA_SCALE).astype(BF16)
    c_new = _rms_norm(d[:, lq:lq + lkv], kg_ref[...])
    o = lq + lkv
    kr = d[:, o:o + D_QK_PAD] * kc_ref[...] + d[:, o + D_QK_PAD:o + 2 * D_QK_PAD] * st
    c_ref[...] = c_new
    kr_ref[...] = kr
    _kv_store(c_new, kr, wk_ref, wv_ref, k_ref, v_ref)


def _layer1_in(x, wc, wd, ctab, stab, kctab, qg, wq, wqs, kg, wk, wv, tm):
    b, l, dm = x.shape
    hq = D_HEADS * D_QK_PAD
    hv = D_HEADS * D_V_PAD
    const = lambda shape: pl.BlockSpec(shape, lambda bi, i: (0,) * len(shape))
    tab = pl.BlockSpec((tm, D_QK_PAD), lambda bi, i: (i, 0))
    rows = lambda width: pl.BlockSpec((None, tm, width), lambda bi, i: (bi, i, 0))
    widths = [(wc.shape[1], F32), (hq, BF16), (D_KV_LORA, F32), (D_QK_PAD, F32), (hq, BF16), (hv, BF16)]
    return pl.pallas_call(
        _layer1_in_kernel,
        grid=(b, l // tm),
        in_specs=[rows(dm), const(wc.shape), const(wd.shape), tab, tab, tab,
                  const((1, D_Q_LORA)), const((D_Q_LORA, hq)), const((D_Q_LORA, hq)), const((1, D_KV_LORA)),
                  const((D_KV_LORA, hq)), const((D_KV_LORA, D_HEADS * D_V))],
        out_specs=[rows(w) for w, _ in widths],
        out_shape=[jax.ShapeDtypeStruct((b, l, w), dt) for w, dt in widths],
        compiler_params=_params(("parallel", "parallel")),
        name="layer1_in",
    )(x, wc, wd, ctab, stab, kctab, qg, wq, wqs, kg, wk, wv)


def _kv_kernel(c_ref, kr_ref, wk_ref, wv_ref, k_ref, v_ref):
    _kv_store(c_ref[...], kr_ref[...], wk_ref, wv_ref, k_ref, v_ref)


def _kv(c, krp, wk, wv, tm):
    b, n, _ = c.shape
    hk = D_HEADS * D_QK_PAD
    hv = D_HEADS * D_V_PAD
    const = lambda shape: pl.BlockSpec(shape, lambda bi, i: (0,) * len(shape))
    rows = lambda width: pl.BlockSpec((None, tm, width), lambda bi, i: (bi, i, 0))
    return pl.pallas_call(
        _kv_kernel,
        grid=(b, n // tm),
        in_specs=[rows(D_KV_LORA), rows(D_QK_PAD), const((D_KV_LORA, hk)), const((D_KV_LORA, D_HEADS * D_V))],
        out_specs=[rows(hk), rows(hv)],
        out_shape=[jax.ShapeDtypeStruct((b, n, hk), BF16), jax.ShapeDtypeStruct((b, n, hv), BF16)],
        compiler_params=_params(("parallel", "parallel")),
        name="mla_kv",
    )(c, krp, wk, wv)


def _attn_kernel(q_ref, k_ref, v_ref, o_ref, m_ref, acc_ref, *, tq, tk, nk, q_off):
    hps = ATTN_HEADS
    q_first = q_off + pl.program_id(2) * tq
    first_chunk_end = q_first // CHUNK * CHUNK + CHUNK
    last_chunk_end = (q_first + tq - 1) // CHUNK * CHUNK + CHUNK
    n_full = jnp.minimum(nk, first_chunk_end // tk)
    m_ref[...] = jnp.full(m_ref.shape, -jnp.inf, F32)
    acc_ref[...] = jnp.zeros(acc_ref.shape, F32)
    qs = [q_ref[:, h * D_QK_PAD:(h + 1) * D_QK_PAD] for h in range(hps)]

    def block(start, width, masked):
        ss = [lax.dot_general(qs[h], k_ref[pl.ds(start, width), h * D_QK_PAD:(h + 1) * D_QK_PAD],
                              _dims(False, True), preferred_element_type=F32) for h in range(hps)]
        if masked:
            q_chunk = jnp.right_shift(q_first + _iota2((tq, width), 0), CHUNK_SHIFT)
            k_chunk = jnp.right_shift(start + _iota2((tq, width), 1), CHUNK_SHIFT)
            ss = [jnp.where(k_chunk <= q_chunk, s, -jnp.inf) for s in ss]
        m_old = [m_ref[h] for h in range(hps)]
        m_new = [jnp.maximum(m_old[h], jnp.max(ss[h], axis=-1, keepdims=True)) for h in range(hps)]
        ps = [jnp.exp(ss[h] - m_new[h]).astype(BF16) for h in range(hps)]
        for h in range(hps):
            acc_ref[h] = jnp.exp(m_old[h] - m_new[h]) * acc_ref[h] + jnp.dot(
                ps[h], v_ref[pl.ds(start, width), h * D_V_PAD:(h + 1) * D_V_PAD], preferred_element_type=F32)
            m_ref[h] = m_new[h]

    def full_body(j, carry):
        block(pl.multiple_of(j * tk, tk), tk, False)
        return carry

    def edge_body(j, carry):
        block(pl.multiple_of(j * tk, tk), tk, True)
        return carry

    edge_len = jnp.minimum(nk * tk, last_chunk_end) - n_full * tk
    n_edge = jnp.maximum(edge_len - 1, 0) // tk
    lax.fori_loop(0, n_full, full_body, 0)
    lax.fori_loop(n_full, n_full + n_edge, edge_body, 0)
    tail = edge_len - n_edge * tk
    tail_start = pl.multiple_of((n_full + n_edge) * tk, tk)
    ew = tk // ATTN_EDGE_SPLIT
    for i in range(1, ATTN_EDGE_SPLIT + 1):
        @pl.when((tail > (i - 1) * ew) & (tail <= i * ew))
        def _(i=i):
            block(tail_start, i * ew, True)

    for h in range(hps):
        acc = acc_ref[h]
        o_ref[:, h * D_V:(h + 1) * D_V] = acc[:, 0:D_V] / acc[:, D_V:D_V + 1]


def _attention(q, k, v, tq, tk, q_off):
    b, l, _ = q.shape
    lk = k.shape[1]
    hps = ATTN_HEADS
    return pl.pallas_call(
        functools.partial(_attn_kernel, tq=tq, tk=tk, nk=lk // tk, q_off=q_off),
        grid=(b, D_HEADS // hps, l // tq),
        in_specs=[pl.BlockSpec((None, tq, hps * D_QK_PAD), lambda bi, h, i: (bi, i, h)),
                  pl.BlockSpec((None, lk, hps * D_QK_PAD), lambda bi, h, i: (bi, 0, h)),
                  pl.BlockSpec((None, lk, hps * D_V_PAD), lambda bi, h, i: (bi, 0, h))],
        out_specs=pl.BlockSpec((None, tq, hps * D_V), lambda bi, h, i: (bi, i, h)),
        out_shape=jax.ShapeDtypeStruct((b, l, D_HEADS * D_V), F32),
        scratch_shapes=[pltpu.VMEM((hps, tq, 1), F32), pltpu.VMEM((hps, tq, D_V_PAD), F32)],
        compiler_params=_params(("parallel", "parallel", "arbitrary")),
        name="mla_attention",
    )(q, k, v)


def _pad_rows_front(buf):
    return jnp.pad(buf, ((0, 0), (ROW_PAD - buf.shape[1], 0), (0, 0)))


def _block_diag(w):
    nb, bi, bj = w.shape
    eye = jnp.eye(nb, dtype=w.dtype)
    return (eye[:, None, :, None] * w[:, :, None, :]).reshape(nb * bi, nb * bj)


def _state_to_block_diag(s, per):
    b, h, n, m = s.shape
    s5 = s.reshape(b, h // per, per, n, m)
    rows = [jnp.pad(s5[:, :, p], ((0, 0), (0, 0), (0, 0), (p * m, (per - 1 - p) * m))) for p in range(per)]
    return jnp.concatenate(rows, axis=2)


def _state_from_block_diag(sbd, per):
    b, g, pn, pm = sbd.shape
    n, m = pn // per, pm // per
    blocks = [sbd[:, :, p * n:(p + 1) * n, p * m:(p + 1) * m] for p in range(per)]
    return jnp.stack(blocks, axis=2).reshape(b, g * per, n, m)


def _rope_swap(w):
    half = w.shape[-1] // 2
    return jnp.concatenate([-w[..., half:], w[..., :half]], axis=-1)


def _prepare(w_in0, a_w_r, a_w_i, a_b_r, a_b_i, w_in1, c_w2, c_a2, d_w_qb, d_w_kvb):
    p = {}
    o = 0
    p["w_a"] = w_in0[:, o:o + 2 * A_WIDTH].astype(BF16); o += 2 * A_WIDTH
    p["w_qkv"] = w_in0[:, o:o + 3 * B_WIDTH].astype(BF16); o += 3 * B_WIDTH
    p["w_z"] = w_in0[:, o:o + B_WIDTH].astype(BF16); o += B_WIDTH
    p["w_gs"] = jnp.pad(w_in0[:, o:o + 2 * B_HEADS], ((0, 0), (0, 128 - 2 * B_HEADS))).astype(BF16)
    p["a_wg"] = jnp.concatenate([_block_diag(a_w_r), _block_diag(a_w_i)], axis=1).astype(BF16)
    p["a_bg"] = jnp.concatenate([a_b_r, a_b_i])[None, :]
    p["w_c"] = w_in1[:, 0:C_PROJ].astype(BF16)
    o = C_PROJ
    w_qa = w_in1[:, o:o + D_Q_LORA]; o += D_Q_LORA
    w_craw = w_in1[:, o:o + D_KV_LORA]; o += D_KV_LORA
    w_kr = w_in1[:, o:o + D_ROPE]
    place = lambda w: jnp.pad(w, ((0, 0), (D_NOPE, D_QK_PAD - D_NOPE - D_ROPE)))
    p["w_d"] = jnp.concatenate([w_qa, w_craw, place(w_kr), place(_rope_swap(w_kr))], axis=1).astype(BF16)
    p["c_w2p"] = jnp.pad(c_w2, ((0, 64), (0, 0)))
    p["c_a2p"] = jnp.pad(c_a2, ((64, 0), (0, 0)))
    wq = d_w_qb.reshape(D_Q_LORA, D_HEADS, D_NOPE + D_ROPE)
    zpad = jnp.zeros((D_Q_LORA, D_HEADS, D_QK_PAD - D_NOPE - D_ROPE), F32)
    p["wq"] = jnp.concatenate([wq, zpad], axis=-1).reshape(D_Q_LORA, D_HEADS * D_QK_PAD).astype(BF16)
    wq_sw = jnp.concatenate([jnp.zeros((D_Q_LORA, D_HEADS, D_NOPE), F32), _rope_swap(wq[..., D_NOPE:]), zpad], axis=-1)
    p["wqs"] = wq_sw.reshape(D_Q_LORA, D_HEADS * D_QK_PAD).astype(BF16)
    wkv = d_w_kvb.reshape(D_KV_LORA, D_HEADS, D_NOPE + D_V)
    wk = jnp.pad(wkv[..., :D_NOPE], ((0, 0), (0, 0), (0, D_QK_PAD - D_NOPE)))
    p["wk"] = wk.reshape(D_KV_LORA, D_HEADS * D_QK_PAD).astype(BF16)
    p["wv"] = wkv[..., D_NOPE:].reshape(D_KV_LORA, D_HEADS * D_V).astype(BF16)
    lane = jnp.arange(MXU_TILE) // C_HEAD_DIM
    p["seg"] = (lane[:, None] == lane[None, :]).astype(BF16)
    return p


def _rope_tables(pos):
    half = D_ROPE // 2
    inv = ROPE_THETA ** (-jnp.arange(half, dtype=F32) / half)
    ang = pos.astype(F32)[:, None] * inv[None, :]
    cos = jnp.cos(ang)
    sin = jnp.sin(ang)
    n = pos.shape[0]
    tail = jnp.zeros((n, D_QK_PAD - D_NOPE - D_ROPE), F32)
    cos2 = jnp.concatenate([cos, cos], axis=1)
    sin2 = jnp.concatenate([sin, sin], axis=1)
    q_cos = jnp.concatenate([jnp.ones((n, D_NOPE), F32), cos2, tail], axis=1)
    q_sin = jnp.concatenate([jnp.zeros((n, D_NOPE), F32), sin2, tail], axis=1)
    k_cos = jnp.concatenate([jnp.zeros((n, D_NOPE), F32), cos2, tail], axis=1)
    return q_cos, q_sin, k_cos


def _trunk(x, pe, pos, a_conv, a_h, b_conv, b_s, c_shift, c_s, d_ckv, d_krope, f_conv, wts, prm):
    (a_conv_w, a_conv_b, a_lambda, b_conv_w, b_a_log, b_dt_bias, b_norm_g, w_out0, c_mu, c_w0,
     c_a0, c_g2, c_k_k, c_k_a, c_r_k, c_lnx_g, c_lnx_b, d_qa_g, d_kva_g, w_out1, ln_mix_g, ln_mix_b,
     ffn_w_up, ffn_conv_w, ffn_conv_b, ffn_w_down, ln_ffn_g, ln_ffn_b, ple_w_gate, ple_b_gate,
     ple_w_proj) = wts
    b, l, d = x.shape
    t = b * l
    tm = min(512, t)
    tt = min(512, l)
    def chunks_per_step(want):
        cps = want
        while l % (cps * CHUNK):
            cps //= 2
        return cps
    row = lambda vec: vec.reshape(1, -1)
    f_new = []
    ffn_nseq = max(1, min(b, FFN_ROWS // l))
    while b % ffn_nseq:
        ffn_nseq -= 1
    ffn_tt = tt if ffn_nseq == 1 else ffn_nseq * l

    def ffn_block(i, mix_a, mix_b, xin, w_out):
        y, nbuf = _ffn(mix_a, mix_b, xin.reshape(b, l, d), pe, i, _pad_rows_front(f_conv[i]),
                       w_out.astype(BF16), row(ln_mix_g[i]), row(ln_mix_b[i]),
                       ffn_w_up[i].astype(BF16), ffn_conv_w[i], row(ffn_conv_b[i]),
                       ffn_w_down[i].astype(BF16), row(ln_ffn_g[i]), row(ln_ffn_b[i]),
                       ple_w_gate[i].astype(BF16), row(ple_b_gate[i]), ple_w_proj[i].astype(BF16),
                       ffn_tt, ffn_nseq)
        f_new.append(nbuf[:, ROW_PAD - 2:, :])
        return y.reshape(t, d)

    x2 = x.reshape(t, d)
    a2, qkv, z, gs = _proj(x2, [prm["w_a"], prm["w_qkv"], prm["w_z"], prm["w_gs"]], tm)
    a_out, a_buf, a_hl = _rglru(a2.reshape(b, l, -1), _pad_rows_front(a_conv), a_h[:, None, :],
                                a_conv_w, row(a_conv_b), prm["a_wg"], prm["a_bg"], row(a_lambda),
                                min(256, l))
    pad8 = lambda vec: jnp.pad(vec, (B_HEADS, 128 - 2 * B_HEADS))[None, :]
    b_out, b_buf, b_sl = _gdn(qkv.reshape(b, l, -1), z.reshape(b, l, -1), gs.reshape(b, l, -1),
                              _pad_rows_front(b_conv), _state_to_block_diag(b_s, B_PAIR), b_conv_w,
                              pad8(b_a_log), pad8(b_dt_bias), row(b_norm_g), chunks_per_step(GDN_CPS))
    x2 = ffn_block(0, a_out, b_out, x2, w_out0)

    q_cos, q_sin, k_cos = _rope_tables(pos)
    c_in, q, c_new, kr_pad, k, v = _layer1_in(x2.reshape(b, l, d), prm["w_c"], prm["w_d"], q_cos, q_sin, k_cos,
                                              row(d_qa_g), prm["wq"], prm["wqs"], row(d_kva_g), prm["wk"],
                                              prm["wv"], tt)
    c_out, c_sh, c_sl = _rwkv(c_in, _pad_rows_front(c_shift),
                              _state_to_block_diag(jnp.swapaxes(c_s, 2, 3), C_GROUP), row(c_mu), row(c_w0),
                              prm["c_w2p"], row(c_a0), prm["c_a2p"], c_g2, row(c_k_k), row(c_k_a),
                              row(c_r_k.reshape(-1)), row(c_lnx_g), row(c_lnx_b), prm["seg"],
                              chunks_per_step(RWKV_CPS))
    past = d_ckv.shape[1]
    if past:
        kr_old = jnp.pad(d_krope, ((0, 0), (0, 0), (D_NOPE, D_QK_PAD - D_NOPE - D_ROPE)))
        k_old, v_old = _kv(d_ckv, kr_old, prm["wk"], prm["wv"], min(512, past))
        k = jnp.concatenate([k_old, k], axis=1)
        v = jnp.concatenate([v_old, v], axis=1)
    lk = past + l
    tq = min(ATTN_TQ, l)
    tk = ATTN_TK if lk % ATTN_TK == 0 else lk
    d_out = _attention(q, k, v, tq, tk, past)
    x2 = ffn_block(1, c_out, d_out, x2, w_out1)

    return (x2.reshape(b, l, d), a_buf[:, ROW_PAD - 3:, :], a_hl[:, 0, :], b_buf[:, ROW_PAD - 3:, :],
            _state_from_block_diag(b_sl, B_PAIR), c_sh[:, ROW_PAD - 1:, :],
            jnp.swapaxes(_state_from_block_diag(c_sl, C_GROUP), 2, 3), c_new,
            kr_pad[:, :, D_NOPE:D_NOPE + D_ROPE],
            jnp.stack(f_new))


def kernel(x_prompt, x_sample, state_a_conv, state_a_h, state_b_conv, state_b_s, state_c_shift, state_c_s, cache_d_ckv, cache_d_krope, state_ffn_conv, p_prompt, p_sample, w_in0, a_conv_w, a_conv_b, a_w_r, a_b_r, a_w_i, a_b_i, a_lambda, b_conv_w, b_a_log, b_dt_bias, b_norm_g, w_out0, w_in1, c_mu, c_w0, c_w2, c_a0, c_a2, c_g2, c_k_k, c_k_a, c_r_k, c_lnx_g, c_lnx_b, d_qa_g, d_w_qb, d_kva_g, d_w_kvb, w_out1, ln_mix_g, ln_mix_b, ffn_w_up, ffn_conv_w, ffn_conv_b, ffn_w_down, ln_ffn_g, ln_ffn_b, ple_w_gate, ple_b_gate, ple_w_proj):
    prm = _prepare(w_in0, a_w_r, a_w_i, a_b_r, a_b_i, w_in1, c_w2, c_a2, d_w_qb, d_w_kvb)
    wts = (a_conv_w, a_conv_b, a_lambda, b_conv_w, b_a_log, b_dt_bias, b_norm_g, w_out0, c_mu, c_w0,
           c_a0, c_g2, c_k_k, c_k_a, c_r_k, c_lnx_g, c_lnx_b, d_qa_g, d_kva_g, w_out1, ln_mix_g, ln_mix_b,
           ffn_w_up, ffn_conv_w, ffn_conv_b, ffn_w_down, ln_ffn_g, ln_ffn_b, ple_w_gate, ple_b_gate,
           ple_w_proj)
    dt = x_prompt.dtype
    bp, lp = x_prompt.shape[0], x_prompt.shape[1]
    zeros = lambda *shape: jnp.zeros(shape, dt)
    prompt = _trunk(x_prompt, p_prompt, jnp.arange(lp),
                    zeros(bp, 3, A_WIDTH), zeros(bp, A_WIDTH), zeros(bp, 3, 3 * B_WIDTH),
                    zeros(bp, B_HEADS, B_HEAD_DIM, B_HEAD_DIM), zeros(bp, 1, C_PROJ),
                    zeros(bp, C_HEADS, C_HEAD_DIM, C_HEAD_DIM), zeros(bp, 0, D_KV_LORA),
                    zeros(bp, 0, D_ROPE), zeros(DEPTH, bp, 2, 2 * D_FF), wts, prm)
    ls = x_sample.shape[1]
    past = cache_d_ckv.shape[1]
    sample = _trunk(x_sample, p_sample, past + jnp.arange(ls), state_a_conv, state_a_h, state_b_conv,
                    state_b_s, state_c_shift, state_c_s, cache_d_ckv, cache_d_krope, state_ffn_conv,
                    wts, prm)
    out = [prompt[0], sample[0]]
    for ps, ss in zip(prompt[1:], sample[1:]):
        out += [ps, ss]
    return tuple(out)
```

```python
import functools

import jax
import jax.numpy as jnp
from jax import lax
from jax.experimental import pallas as pl
from jax.experimental.pallas import tpu as pltpu

F32 = jnp.float32
BF16 = jnp.bfloat16

D_MODEL = 1024
DEPTH = 2
CHUNK = 64
CHUNK_SHIFT = CHUNK.bit_length() - 1
PLE_DIM = 256
ALPHA = (2 * DEPTH) ** 0.25
NORM_EPS = 1e-5
A_WIDTH = 512
A_BLOCKS = 8
A_C = 8.0
SQRT_FLOOR = 1e-30
B_HEADS = 4
B_HEAD_DIM = 128
B_WIDTH = 512
C_HEAD_DIM = 64
C_WIDTH = 512
C_HEADS = 8
C_LNX_EPS = 64e-5
C_PROJ = 3 * C_WIDTH + 64 + 64 + 128
D_HEADS = 4
D_Q_LORA = 384
D_KV_LORA = 256
D_NOPE = 128
D_ROPE = 64
D_V = 128
D_QK_PAD = 256
D_V_PAD = 256
MLA_SCALE = (D_NOPE + D_ROPE) ** -0.5
ROPE_THETA = 10000.0
D_FF = 2816
FF_BLOCK = 256
FFN_ROWS = 512
TRI_BASE = 8
MXU_TILE = 256
ROW_PAD = 8
VMEM_LIMIT = 56 * 1024 * 1024
ATTN_HEADS = 2
ATTN_TQ = 512
GDN_CPS = 8
RWKV_CPS = 8
ATTN_EDGE_SPLIT = 4
ATTN_TK = 2048
B_PAIR = MXU_TILE // B_HEAD_DIM
C_GROUP = MXU_TILE // C_HEAD_DIM


def _params(sem):
    return pltpu.CompilerParams(dimension_semantics=sem, vmem_limit_bytes=VMEM_LIMIT)


def _sigmoid(x):
    return 1.0 / (1.0 + jnp.exp(-x))


def _softplus(x):
    return jnp.maximum(x, 0.0) + jnp.log(1.0 + jnp.exp(-jnp.abs(x)))


def _silu(x):
    return x * _sigmoid(x)


def _gelu_tanh(x):
    return 0.5 * x * (1.0 + jnp.tanh(0.7978845608028654 * (x + 0.044715 * x * x * x)))


def _dims(trans_a, trans_b):
    return (((0 if trans_a else 1,), (1 if trans_b else 0,)), ((), ()))


def _mm(a, b, trans_a=False, trans_b=False):
    return lax.dot_general(a.astype(BF16), b.astype(BF16), _dims(trans_a, trans_b),
                           preferred_element_type=F32)


def _split3(a):
    a1 = a.astype(BF16)
    r1 = a - a1.astype(F32)
    a2 = r1.astype(BF16)
    return a1, a2, (r1 - a2.astype(F32)).astype(BF16)


def _mm_masked_sum(a, mask):
    a1 = a.astype(BF16)
    a2 = (a - a1.astype(F32)).astype(BF16)
    mb = mask.astype(BF16)
    return jnp.dot(a2, mb, preferred_element_type=F32) + jnp.dot(a1, mb, preferred_element_type=F32)


def _mm_exact_lhs(mask, b):
    b1, b2, b3 = _split3(b)
    mb = mask.astype(BF16)
    out = jnp.dot(mb, b3, preferred_element_type=F32)
    out = out + jnp.dot(mb, b2, preferred_element_type=F32)
    return out + jnp.dot(mb, b1, preferred_element_type=F32)


def _iota2(shape, axis):
    return lax.broadcasted_iota(jnp.int32, shape, axis)


def _blk(idx, size):
    return jnp.right_shift(idx, size.bit_length() - 1)


def _bd_mask(shape, row_block, col_block):
    return _blk(_iota2(shape, 0), row_block) == _blk(_iota2(shape, 1), col_block)


def _bd_rows(x, mask):
    reps = mask.shape[0] // x.shape[0]
    return jnp.where(mask, jnp.concatenate([x] * reps, axis=0), 0.0)


def _tri_inv_cat(xs, bd):
    c, wd = xs[0].shape
    row = _iota2((c, wd), 0)
    col = _iota2((c, wd), 1) & (c - 1)
    eye = (row == col).astype(F32)

    def prod(a, b):
        return jnp.dot(a.astype(BF16), _bd_rows(b, bd).astype(BF16), preferred_element_type=F32)

    base = _blk(row, TRI_BASE) == _blk(col, TRI_BASE)
    ps = [jnp.where(base, x, 0.0) for x in xs]
    ts = [eye + p for p in ps]
    k = 2
    while k < TRI_BASE:
        ps = [prod(p, p) for p in ps]
        ts = [t + prod(t, p) for t, p in zip(ts, ps)]
        k *= 2
    size = TRI_BASE
    while size < c:
        pair = (_blk(row, 2 * size) == _blk(col, 2 * size)) & (_blk(row, size) != _blk(col, size))
        offs = [prod(t, jnp.where(pair, x, 0.0)) for t, x in zip(ts, xs)]
        ts = [t + prod(o, t) for t, o in zip(ts, offs)]
        size *= 2
    return ts


def _layer_norm(x, g, b):
    mu = jnp.mean(x, axis=-1, keepdims=True)
    xc = x - mu
    var = jnp.mean(xc * xc, axis=-1, keepdims=True)
    return xc * lax.rsqrt(var + NORM_EPS) * g + b


def _rms_norm(x, g, eps=1e-6):
    return x * lax.rsqrt(jnp.mean(x * x, axis=-1, keepdims=True) + eps) * g


def _proj_kernel(x_ref, *refs):
    n = len(refs) // 2
    xb = x_ref[...].astype(BF16)
    for w_ref, o_ref in zip(refs[:n], refs[n:]):
        o_ref[...] = jnp.dot(xb, w_ref[...], preferred_element_type=F32)


def _proj(x, ws, tm):
    t, k = x.shape
    return pl.pallas_call(
        _proj_kernel,
        grid=(t // tm,),
        in_specs=[pl.BlockSpec((tm, k), lambda i: (i, 0))]
        + [pl.BlockSpec(w.shape, lambda i: (0, 0)) for w in ws],
        out_specs=[pl.BlockSpec((tm, w.shape[1]), lambda i: (i, 0)) for w in ws],
        out_shape=[jax.ShapeDtypeStruct((t, w.shape[1]), F32) for w in ws],
        compiler_params=_params(("parallel",)),
        name="proj",
    )(x, *ws)


def _scan_rows(a, b):
    n = a.shape[0]
    row = _iota2(a.shape, 0)
    s = 1
    while s < n:
        a_sh = pltpu.roll(a, s, 0)
        b_sh = pltpu.roll(b, s, 0)
        m = row >= s
        b = jnp.where(m, a * b_sh + b, b)
        a = jnp.where(m, a * a_sh, a)
        s *= 2
    return a, b


def _rglru_kernel(a2_ref, buf_ref, h0_ref, cw_ref, cb_ref, wg_ref, bg_ref, lam_ref,
                  out_ref, nbuf_ref, hl_ref, xp_ref, hc_ref, *, tt):
    w = A_WIDTH

    @pl.when(pl.program_id(1) == 0)
    def _():
        xp_ref[0:ROW_PAD, :] = buf_ref[...]
        hc_ref[...] = h0_ref[...]

    gate_in = a2_ref[:, 0:w]
    x_in = a2_ref[:, w:2 * w]
    xp_ref[ROW_PAD:ROW_PAD + tt, :] = x_in
    xc = x_in * cw_ref[3:4, :] + cb_ref[...]
    for j in range(3):
        xc = xc + xp_ref[ROW_PAD - 3 + j:ROW_PAD - 3 + j + tt, :] * cw_ref[j:j + 1, :]
    gates = jnp.dot(xc.astype(BF16), wg_ref[...], preferred_element_type=F32) + bg_ref[...]
    r = _sigmoid(gates[:, 0:w])
    ig = _sigmoid(gates[:, w:2 * w])
    log_a = (-A_C) * r * _softplus(-lam_ref[...])
    a = jnp.exp(log_a)
    y = 1.0 - a * a
    u = y * lax.rsqrt(jnp.maximum(y, SQRT_FLOOR)) * (ig * xc)
    a_cum, h_loc = _scan_rows(a, u)
    h = a_cum * hc_ref[...] + h_loc
    out_ref[...] = h * _gelu_tanh(gate_in)
    hc_ref[...] = h[tt - 1:tt, :]
    hl_ref[...] = h[tt - 1:tt, :]
    last = x_in[tt - ROW_PAD:tt, :]
    xp_ref[0:ROW_PAD, :] = last
    nbuf_ref[...] = last


def _rglru(a2, buf8, h0, cw, cb, wg, bg, lam, tt):
    b, l, _ = a2.shape
    w = A_WIDTH
    const = lambda shape: pl.BlockSpec(shape, lambda bi, i: (0,) * len(shape))
    return pl.pallas_call(
        functools.partial(_rglru_kernel, tt=tt),
        grid=(b, l // tt),
        in_specs=[pl.BlockSpec((None, tt, 2 * w), lambda bi, i: (bi, i, 0)),
                  pl.BlockSpec((None, ROW_PAD, w), lambda bi, i: (bi, 0, 0)),
                  pl.BlockSpec((None, 1, w), lambda bi, i: (bi, 0, 0)),
                  const((4, w)), const((1, w)), const((w, 2 * w)), const((1, 2 * w)), const((1, w))],
        out_specs=[pl.BlockSpec((None, tt, w), lambda bi, i: (bi, i, 0)),
                   pl.BlockSpec((None, ROW_PAD, w), lambda bi, i: (bi, 0, 0)),
                   pl.BlockSpec((None, 1, w), lambda bi, i: (bi, 0, 0))],
        out_shape=[jax.ShapeDtypeStruct((b, l, w), F32),
                   jax.ShapeDtypeStruct((b, ROW_PAD, w), F32),
                   jax.ShapeDtypeStruct((b, 1, w), F32)],
        scratch_shapes=[pltpu.VMEM((ROW_PAD + tt, w), F32), pltpu.VMEM((1, w), F32)],
        compiler_params=_params(("parallel", "arbitrary")),
        name="rglru",
    )(a2, buf8, h0, cw, cb, wg, bg, lam)


def _gdn_kernel(qkv_ref, z_ref, gs_ref, buf_ref, s0_ref, cw_ref, alog_ref, dtb_ref, ng_ref,
                out_ref, nbuf_ref, sl_ref, xp_ref, s_ref, *, cps):
    c = CHUNK
    dk = B_HEAD_DIM
    nh = B_HEADS
    hw = nh * dk
    gw = MXU_TILE
    tt = cps * c

    @pl.when(pl.program_id(1) == 0)
    def _():
        xp_ref[0:ROW_PAD, :] = buf_ref[...]
        s_ref[...] = s0_ref[...]

    x_in = qkv_ref[...]
    xp_ref[ROW_PAD:ROW_PAD + tt, :] = x_in
    xc = x_in * cw_ref[3:4, :]
    for j in range(3):
        xc = xc + xp_ref[ROW_PAD - 3 + j:ROW_PAD - 3 + j + tt, :] * cw_ref[j:j + 1, :]
    xc = _silu(xc)
    last = x_in[tt - ROW_PAD:tt, :]
    xp_ref[0:ROW_PAD, :] = last
    nbuf_ref[...] = last

    def unit(x, scale):
        return x * (lax.rsqrt(jnp.sum(x * x, axis=-1, keepdims=True) + 1e-6) * scale)

    q4 = jnp.concatenate([unit(xc[:, h * dk:(h + 1) * dk], dk ** -0.5) for h in range(nh)], axis=1)
    k4 = jnp.concatenate([unit(xc[:, hw + h * dk:hw + (h + 1) * dk], 1.0) for h in range(nh)], axis=1)
    v4 = xc[:, 2 * hw:3 * hw]
    gs = gs_ref[...]
    beta_all = _sigmoid(gs)
    g_all = -jnp.exp(alog_ref[...]) * _softplus(gs + dtb_ref[...])

    def per_head(x, first):
        return jnp.concatenate([jnp.broadcast_to(x[:, first + h:first + h + 1], (x.shape[0], dk))
                                for h in range(nh)], axis=1)

    ri = _iota2((c, c), 0)
    ci = _iota2((c, c), 1)
    tri = (ri >= ci).astype(BF16)
    upper = (ri > ci).astype(F32)
    row_i = _iota2((c, gw), 0)
    lane_j = _iota2((c, gw), 1) & (c - 1)
    lane_head = _blk(_iota2((c, gw), 1), c)
    low_i = row_i >= lane_j
    low_s = row_i > lane_j
    bd_tok = _bd_mask((gw, gw), c, c)
    bd_kt = _bd_mask((gw, hw), c, dk)
    bd_vn = _bd_mask((B_PAIR * c, gw), c, dk)
    bd_st = _bd_mask((gw, gw), dk, dk)

    pre = []
    for cc in range(cps):
        rs = slice(cc * c, (cc + 1) * c)
        g_c = g_all[rs]
        gmat = _mm_exact_lhs(tri, jnp.concatenate(
            [g_c[:, nh + h:nh + h + 1] * upper for h in range(nh)] + [g_c], axis=1))
        decay = jnp.where(low_i, jnp.exp(gmat[:, 0:gw]), 0.0)
        gc_all = gmat[:, gw:gw + 128]
        gc4 = per_head(gc_all, nh)
        gl4 = per_head(gc_all[c - 1:c, :], nh)
        egc4 = jnp.exp(gc4)
        beta4 = per_head(beta_all[rs], 0)
        kc = k4[rs]
        qc = q4[rs]
        kb = kc * beta4
        gkq = _mm(jnp.concatenate([kb, qc], axis=0), _bd_rows(kc, bd_kt), trans_b=True)
        pre.append(dict(lmat=jnp.where(low_s, gkq[0:c] * decay, 0.0), qk=gkq[c:2 * c] * decay,
                        vb=v4[rs] * beta4, kbe=kb * egc4, q_in=qc * egc4,
                        k_out_t=jnp.transpose(kc * jnp.exp(gl4 - gc4)), egl=jnp.exp(gl4)))
    tmats = _tri_inv_cat([-p["lmat"] for p in pre], bd_tok)
    for p, tmat in zip(pre, tmats):
        us, ws = [], []
        for h in range(nh):
            hs = slice(h * dk, (h + 1) * dk)
            rhs = jnp.concatenate([p["vb"][:, hs], p["kbe"][:, hs]], axis=1)
            uw = _mm(jnp.where(lane_head == h, tmat, 0.0), jnp.concatenate([rhs] * nh, axis=0))
            us.append(uw[:, 0:dk])
            ws.append(uw[:, dk:2 * dk])
        p["u"] = jnp.concatenate(us, axis=1)
        p["w"] = jnp.concatenate(ws, axis=1)

    state = [s_ref[i] for i in range(nh // B_PAIR)]
    for cc, p in enumerate(pre):
        rs = slice(cc * c, (cc + 1) * c)
        outs = []
        for i in range(nh // B_PAIR):
            ps = slice(i * gw, (i + 1) * gw)
            ws = _mm(jnp.concatenate([p["w"][:, ps], p["q_in"][:, ps]], axis=0), state[i])
            v_new = p["u"][:, ps] - ws[0:c]
            qk = p["qk"][:, i * B_PAIR * c:(i + 1) * B_PAIR * c]
            outs.append(ws[c:2 * c] + _mm(qk, _bd_rows(v_new, bd_vn)))
            state[i] = state[i] * p["egl"][:, ps] + jnp.where(bd_st, _mm(p["k_out_t"][ps, :], v_new), 0.0)
        o = jnp.concatenate(outs, axis=1)
        for h in range(nh):
            hs = slice(h * dk, (h + 1) * dk)
            out_ref[rs, hs] = _rms_norm(o[:, hs], ng_ref[...]) * _silu(z_ref[rs, hs])
    for i in range(nh // B_PAIR):
        s_ref[i] = state[i]
        sl_ref[i] = state[i]


def _gdn(qkv, z, gs, buf8, s0, cw, alog, dtb, ng, cps):
    b, l, _ = qkv.shape
    tt = cps * CHUNK
    w3 = 3 * B_WIDTH
    const = lambda shape: pl.BlockSpec(shape, lambda bi, i: (0,) * len(shape))
    st = (B_HEADS // B_PAIR, MXU_TILE, MXU_TILE)
    return pl.pallas_call(
        functools.partial(_gdn_kernel, cps=cps),
        grid=(b, l // tt),
        in_specs=[pl.BlockSpec((None, tt, w3), lambda bi, i: (bi, i, 0)),
                  pl.BlockSpec((None, tt, B_WIDTH), lambda bi, i: (bi, i, 0)),
                  pl.BlockSpec((None, tt, 128), lambda bi, i: (bi, i, 0)),
                  pl.BlockSpec((None, ROW_PAD, w3), lambda bi, i: (bi, 0, 0)),
                  pl.BlockSpec((None,) + st, lambda bi, i: (bi, 0, 0, 0)),
                  const((4, w3)), const((1, 128)), const((1, 128)), const((1, B_HEAD_DIM))],
        out_specs=[pl.BlockSpec((None, tt, B_WIDTH), lambda bi, i: (bi, i, 0)),
                   pl.BlockSpec((None, ROW_PAD, w3), lambda bi, i: (bi, 0, 0)),
                   pl.BlockSpec((None,) + st, lambda bi, i: (bi, 0, 0, 0))],
        out_shape=[jax.ShapeDtypeStruct((b, l, B_WIDTH), F32),
                   jax.ShapeDtypeStruct((b, ROW_PAD, w3), F32),
                   jax.ShapeDtypeStruct((b,) + st, F32)],
        scratch_shapes=[pltpu.VMEM((ROW_PAD + tt, w3), F32), pltpu.VMEM(st, F32)],
        compiler_params=_params(("parallel", "arbitrary")),
        name="gdn",
    )(qkv, z, gs, buf8, s0, cw, alog, dtb, ng)


def _ffn_kernel(ma_ref, mb_ref, x_ref, pe_ref, buf_ref, wo_ref, g0_ref, b0_ref, wup_ref, cw_ref, cb_ref,
                wdn_ref, g_ref, b_ref, wg_ref, bg_ref, wp_ref, o_ref, nbuf_ref, prev_ref, act_ref, *, tt, nseq):
    ff = D_FF
    nb = ff // FF_BLOCK
    half = ma_ref.shape[1]
    ls = tt // nseq

    @pl.when(pl.program_id(1) == 0)
    def _():
        prev_ref[...] = buf_ref[...]

    mix = jnp.dot(ma_ref[...].astype(BF16), wo_ref[0:half, :], preferred_element_type=F32)
    mix = mix + jnp.dot(mb_ref[...].astype(BF16), wo_ref[half:2 * half, :], preferred_element_type=F32)
    x = _layer_norm(ALPHA * x_ref[...] + mix, g0_ref[...], b0_ref[...])
    xb = x.astype(BF16)
    row = _iota2((tt, FF_BLOCK), 0) & (ls - 1)

    def prev_row(r, sl):
        if nseq == 1:
            return prev_ref[0, r:r + 1, sl]
        return jnp.concatenate([jnp.broadcast_to(prev_ref[s, r:r + 1, sl], (ls, FF_BLOCK)) for s in range(nseq)],
                               axis=0)

    def conv(col):
        sl = slice(col, col + FF_BLOCK)
        h = jnp.dot(xb, wup_ref[:, sl], preferred_element_type=F32)
        p1 = prev_row(ROW_PAD - 1, sl)
        p2 = prev_row(ROW_PAD - 2, sl)
        h1 = jnp.where(row == 0, p1, pltpu.roll(h, 1, 0))
        h2 = jnp.where(row == 0, p2, jnp.where(row == 1, p1, pltpu.roll(h, 2, 0)))
        for s in range(nseq):
            prev_ref[s, :, sl] = h[(s + 1) * ls - ROW_PAD:(s + 1) * ls, :]
        return h2 * cw_ref[0:1, sl] + h1 * cw_ref[1:2, sl] + h * cw_ref[2:3, sl] + cb_ref[:, sl]

    for j in range(nb):
        gate = conv(j * FF_BLOCK)
        val = conv(ff + j * FF_BLOCK)
        act_ref[:, j * FF_BLOCK:(j + 1) * FF_BLOCK] = (_silu(gate) * val).astype(BF16)
    nbuf_ref[...] = prev_ref[...]
    f = jnp.dot(act_ref[...], wdn_ref[...], preferred_element_type=F32)
    y = _layer_norm(ALPHA * x + f, g_ref[...], b_ref[...])
    gate = _sigmoid(jnp.dot(y.astype(BF16), wg_ref[...], preferred_element_type=F32) + bg_ref[...])
    emb = jnp.dot(pe_ref[...].astype(BF16), wp_ref[...], preferred_element_type=F32)
    o_ref[...] = y + gate * emb


def _ffn(ma, mb, x, pe, layer, buf8, wo, g0, b0, wup, cw, cb, wdn, g, bb, wg, bg, wp, tt, nseq):
    b, l, d = x.shape
    half = ma.shape[2]
    ff2 = 2 * D_FF
    groups, rows = b // nseq, nseq * l
    assert nseq == 1 or tt == rows, "several sequences per tile only when the tile holds them whole"
    assert (tt // nseq) & (tt // nseq - 1) == 0, "rows per sequence in a tile must be a power of two"
    fold = lambda a: a.reshape(groups, rows, a.shape[-1])
    const = lambda shape: pl.BlockSpec(shape, lambda bi, i: (0,) * len(shape),
                                       pipeline_mode=pl.Buffered(1))
    tile = lambda width: pl.BlockSpec((None, tt, width), lambda bi, i: (bi, i, 0))
    carried = pl.BlockSpec((None, nseq, ROW_PAD, ff2), lambda bi, i: (bi, 0, 0, 0))
    y, nbuf = pl.pallas_call(
        functools.partial(_ffn_kernel, tt=tt, nseq=nseq),
        grid=(groups, rows // tt),
        in_specs=[tile(half), tile(half), tile(d),
                  pl.BlockSpec((None, None, tt, PLE_DIM), lambda bi, i: (layer, bi, i, 0)),
                  carried,
                  const((2 * half, d)), const((1, d)), const((1, d)),
                  const((d, ff2)), const((3, ff2)), const((1, ff2)), const((D_FF, d)),
                  const((1, d)), const((1, d)), const((d, d)), const((1, d)), const((PLE_DIM, d))],
        out_specs=[tile(d), carried],
        out_shape=[jax.ShapeDtypeStruct((groups, rows, d), F32),
                   jax.ShapeDtypeStruct((groups, nseq, ROW_PAD, ff2), F32)],
        scratch_shapes=[pltpu.VMEM((nseq, ROW_PAD, ff2), F32), pltpu.VMEM((tt, D_FF), BF16)],
        compiler_params=_params(("parallel", "arbitrary")),
        name="ffn",
    )(fold(ma), fold(mb), fold(x), pe.reshape(pe.shape[0], groups, rows, PLE_DIM),
      buf8.reshape(groups, nseq, ROW_PAD, ff2), wo, g0, b0, wup, cw, cb, wdn, g, bb, wg, bg, wp)
    return y.reshape(b, l, d), nbuf.reshape(b, ROW_PAD, ff2)


def _rwkv_kernel(p_ref, sh_ref, s0_ref, mu_ref, w0_ref, w2_ref, a0_ref, a2_ref, g2_ref, kk_ref,
                 ka_ref, rk_ref, lg_ref, lb_ref, seg_ref,
                 out_ref, nsh_ref, sl_ref, xp_ref, s_ref, *, cps):
    c = CHUNK
    w = C_WIDTH
    n = C_HEAD_DIM
    gw = MXU_TILE
    ng = w // gw
    tt = cps * c

    @pl.when(pl.program_id(1) == 0)
    def _():
        xp_ref[0:ROW_PAD, :] = sh_ref[...]
        s_ref[...] = s0_ref[...]

    proj = p_ref[...]
    xp_ref[ROW_PAD:ROW_PAD + tt, :] = proj
    prev = xp_ref[ROW_PAD - 1:ROW_PAD - 1 + tt, :]
    last = proj[tt - ROW_PAD:tt, :]
    xp_ref[0:ROW_PAD, :] = last
    nsh_ref[...] = last
    xs = proj + (prev - proj) * mu_ref[...]

    seg = seg_ref[...]

    def head_sum(x):
        return jnp.concatenate([_mm_masked_sum(x[:, i * gw:(i + 1) * gw], seg) for i in range(ng)], axis=1)

    r = xs[:, 0:w]
    k = xs[:, w:2 * w]
    v = xs[:, 2 * w:3 * w]
    lora_in = xs[:, 3 * w:3 * w + 128]
    ww = -_softplus(-(w0_ref[...] + _mm(jnp.tanh(lora_in), w2_ref[...]))) - 0.5
    a = _sigmoid(a0_ref[...] + _mm(lora_in, a2_ref[...]))
    g = _mm(_sigmoid(xs[:, 3 * w + 128:3 * w + 256]), g2_ref[...])
    kx = k * kk_ref[...]
    k = k * (1.0 + (a - 1.0) * ka_ref[...])
    sums = head_sum(jnp.concatenate([kx * kx, r * k * rk_ref[...]], axis=0))
    kk = kx * lax.rsqrt(sums[0:tt] + 1e-6)
    bonus = sums[tt:2 * tt] * v
    lw = -jnp.exp(ww)
    sa = -kk
    sb = kk * a

    ri = _iota2((c, c), 0)
    ci = _iota2((c, c), 1)
    tri = (ri >= ci).astype(BF16)
    row_i = _iota2((c, gw), 0)
    lane_j = _iota2((c, gw), 1) & (c - 1)
    low_i = row_i >= lane_j
    low_s = row_i > lane_j
    bd = _bd_mask((gw, gw), n, n)

    units = []
    for cc in range(cps):
        rs = slice(cc * c, (cc + 1) * c)
        lw_c = lw[rs]
        cum = _mm_exact_lhs(tri, lw_c)
        cum_last = cum[c - 1:c, :]
        e_out = jnp.exp(-cum)
        e_end = jnp.exp(cum_last - cum)
        a_t = sa[rs] * jnp.exp(cum - lw_c)
        r_t = r[rs] * jnp.exp(cum)
        b_t = sb[rs] * e_out
        k_t = k[rs] * e_out
        b_o = sb[rs] * e_end
        k_o = k[rs] * e_end
        e_last = jnp.exp(cum_last)
        for i in range(ng):
            gs = slice(i * gw, (i + 1) * gw)
            bk = jnp.concatenate([b_o[:, gs], k_o[:, gs]], axis=0)
            e_col = jnp.transpose(jnp.broadcast_to(e_last[:, gs], (2 * c, gw)))
            units.append(dict(cc=cc, i=i, lr=jnp.concatenate([a_t[:, gs], r_t[:, gs]], axis=0),
                              b_t=b_t[:, gs], k_t=k_t[:, gs], v=v[rs, gs], bk_t=jnp.transpose(bk),
                              e_last=jnp.concatenate([e_col] * (gw // (2 * c)), axis=1)))
    for un in units:
        g1 = _mm(un["lr"], _bd_rows(un["b_t"], bd), trans_b=True)
        g2 = _mm(un["lr"], _bd_rows(un["k_t"], bd), trans_b=True)
        un["m_ab"] = jnp.where(low_s, g1[0:c], 0.0)
        un["m_ak"] = jnp.where(low_s, g2[0:c], 0.0)
        un["m_r"] = jnp.concatenate([jnp.where(low_i, g1[c:2 * c], 0.0),
                                     jnp.where(low_i, g2[c:2 * c], 0.0)], axis=1)
        un["bdv"] = _bd_rows(un["v"], bd)
    for un, tmat in zip(units, _tri_inv_cat([un["m_ab"] for un in units], bd)):
        un["tmat"] = tmat
        un["akv"] = _mm(un["m_ak"], un["bdv"])

    state = [s_ref[i] for i in range(ng)]
    outs = [[None] * ng for _ in range(cps)]
    for un in units:
        i = un["i"]
        ls = _mm(un["lr"], state[i])
        u = _mm(un["tmat"], _bd_rows(ls[0:c] + un["akv"], bd))
        outs[un["cc"]][i] = ls[c:2 * c] + _mm(
            un["m_r"], jnp.concatenate([_bd_rows(u, bd), un["bdv"]], axis=0))
        uv = jnp.concatenate([u, un["v"]], axis=0)
        state[i] = state[i] * un["e_last"] + jnp.where(bd, _mm(un["bk_t"], uv), 0.0)
    for i in range(ng):
        s_ref[i] = state[i]
        sl_ref[i] = state[i]

    o = jnp.concatenate([jnp.concatenate(row, axis=1) for row in outs], axis=0)
    inv_n = 1.0 / n
    oc = o - head_sum(o) * inv_n
    var = head_sum(oc * oc) * inv_n
    on = oc * lax.rsqrt(var + C_LNX_EPS) * lg_ref[...] + lb_ref[...]
    out_ref[...] = (on + bonus) * g


def _rwkv(proj, sh8, s0, mu, w0, w2p, a0, a2p, g2, kk, ka, rk, lg, lb, seg, cps):
    b, l, cp = proj.shape
    tt = cps * CHUNK
    w = C_WIDTH
    const = lambda shape: pl.BlockSpec(shape, lambda bi, i: (0,) * len(shape))
    st = (C_HEADS // C_GROUP, MXU_TILE, MXU_TILE)
    return pl.pallas_call(
        functools.partial(_rwkv_kernel, cps=cps),
        grid=(b, l // tt),
        in_specs=[pl.BlockSpec((None, tt, cp), lambda bi, i: (bi, i, 0)),
                  pl.BlockSpec((None, ROW_PAD, cp), lambda bi, i: (bi, 0, 0)),
                  pl.BlockSpec((None,) + st, lambda bi, i: (bi, 0, 0, 0)),
                  const((1, cp)), const((1, w)), const((128, w)), const((1, w)), const((128, w)),
                  const((128, w)), const((1, w)), const((1, w)), const((1, w)), const((1, w)),
                  const((1, w)), const((MXU_TILE, MXU_TILE))],
        out_specs=[pl.BlockSpec((None, tt, w), lambda bi, i: (bi, i, 0)),
                   pl.BlockSpec((None, ROW_PAD, cp), lambda bi, i: (bi, 0, 0)),
                   pl.BlockSpec((None,) + st, lambda bi, i: (bi, 0, 0, 0))],
        out_shape=[jax.ShapeDtypeStruct((b, l, w), F32),
                   jax.ShapeDtypeStruct((b, ROW_PAD, cp), F32),
                   jax.ShapeDtypeStruct((b,) + st, F32)],
        scratch_shapes=[pltpu.VMEM((ROW_PAD + tt, cp), F32), pltpu.VMEM(st, F32)],
        compiler_params=_params(("parallel", "arbitrary")),
        name="rwkv7",
    )(proj, sh8, s0, mu, w0, w2p, a0, a2p, g2, kk, ka, rk, lg, lb, seg)


def _kv_store(c, kr, wk_ref, wv_ref, k_ref, v_ref):
    cb = c.astype(BF16)
    ones = jnp.ones((cb.shape[0], D_V_PAD - D_V), BF16)
    for h in range(D_HEADS):
        sl = slice(h * D_QK_PAD, (h + 1) * D_QK_PAD)
        k_ref[:, sl] = (jnp.dot(cb, wk_ref[:, sl], preferred_element_type=F32) + kr).astype(BF16)
        vh = jnp.dot(cb, wv_ref[:, h * D_V:(h + 1) * D_V], preferred_element_type=F32).astype(BF16)
        v_ref[:, h * D_V_PAD:h * D_V_PAD + D_V] = vh
        v_ref[:, h * D_V_PAD + D_V:(h + 1) * D_V_PAD] = ones


def _layer1_in_kernel(x_ref, wc_ref, wd_ref, ct_ref, st_ref, kc_ref, qg_ref, wq_ref, wqs_ref, kg_ref,
                      wk_ref, wv_ref, cin_ref, q_ref, c_ref, kr_ref, k_ref, v_ref):
    lq = D_Q_LORA
    lkv = D_KV_LORA
    xb = x_ref[...].astype(BF16)
    cin_ref[...] = jnp.dot(xb, wc_ref[...], preferred_element_type=F32)
    d = jnp.dot(xb, wd_ref[...], preferred_element_type=F32)
    qn = _rms_norm(d[:, 0:lq], qg_ref[...]).astype(BF16)
    ct = ct_ref[...]
    st = st_ref[...]
    for h in range(D_HEADS):
        sl = slice(h * D_QK_PAD, (h + 1) * D_QK_PAD)
        q = jnp.dot(qn, wq_ref[:, sl], preferred_element_type=F32) * ct
        q = q + jnp.dot(qn, wqs_ref[:, sl], preferred_element_type=F32) * st
        q_ref[:, sl] = (q * MLA_SCALE).astype(BF16)
    c_new = _rms_norm(d[:, lq:lq + lkv], kg_ref[...])
    o = lq + lkv
    kr = d[:, o:o + D_QK_PAD] * kc_ref[...] + d[:, o + D_QK_PAD:o + 2 * D_QK_PAD] * st
    c_ref[...] = c_new
    kr_ref[...] = kr
    _kv_store(c_new, kr, wk_ref, wv_ref, k_ref, v_ref)


def _layer1_in(x, wc, wd, ctab, stab, kctab, qg, wq, wqs, kg, wk, wv, tm):
    b, l, dm = x.shape
    hq = D_HEADS * D_QK_PAD
    hv = D_HEADS * D_V_PAD
    const = lambda shape: pl.BlockSpec(shape, lambda bi, i: (0,) * len(shape))
    tab = pl.BlockSpec((tm, D_QK_PAD), lambda bi, i: (i, 0))
    rows = lambda width: pl.BlockSpec((None, tm, width), lambda bi, i: (bi, i, 0))
    widths = [(wc.shape[1], F32), (hq, BF16), (D_KV_LORA, F32), (D_QK_PAD, F32), (hq, BF16), (hv, BF16)]
    return pl.pallas_call(
        _layer1_in_kernel,
        grid=(b, l // tm),
        in_specs=[rows(dm), const(wc.shape), const(wd.shape), tab, tab, tab,
                  const((1, D_Q_LORA)), const((D_Q_LORA, hq)), const((D_Q_LORA, hq)), const((1, D_KV_LORA)),
                  const((D_KV_LORA, hq)), const((D_KV_LORA, D_HEADS * D_V))],
        out_specs=[rows(w) for w, _ in widths],
        out_shape=[jax.ShapeDtypeStruct((b, l, w), dt) for w, dt in widths],
        compiler_params=_params(("parallel", "parallel")),
        name="layer1_in",
    )(x, wc, wd, ctab, stab, kctab, qg, wq, wqs, kg, wk, wv)


def _kv_kernel(c_ref, kr_ref, wk_ref, wv_ref, k_ref, v_ref):
    _kv_store(c_ref[...], kr_ref[...], wk_ref, wv_ref, k_ref, v_ref)


def _kv(c, krp, wk, wv, tm):
    b, n, _ = c.shape
    hk = D_HEADS * D_QK_PAD
    hv = D_HEADS * D_V_PAD
    const = lambda shape: pl.BlockSpec(shape, lambda bi, i: (0,) * len(shape))
    rows = lambda width: pl.BlockSpec((None, tm, width), lambda bi, i: (bi, i, 0))
    return pl.pallas_call(
        _kv_kernel,
        grid=(b, n // tm),
        in_specs=[rows(D_KV_LORA), rows(D_QK_PAD), const((D_KV_LORA, hk)), const((D_KV_LORA, D_HEADS * D_V))],
        out_specs=[rows(hk), rows(hv)],
        out_shape=[jax.ShapeDtypeStruct((b, n, hk), BF16), jax.ShapeDtypeStruct((b, n, hv), BF16)],
        compiler_params=_params(("parallel", "parallel")),
        name="mla_kv",
    )(c, krp, wk, wv)


def _attn_kernel(q_ref, k_ref, v_ref, o_ref, m_ref, acc_ref, *, tq, tk, nk, q_off):
    hps = ATTN_HEADS
    q_first = q_off + pl.program_id(2) * tq
    first_chunk_end = q_first // CHUNK * CHUNK + CHUNK
    last_chunk_end = (q_first + tq - 1) // CHUNK * CHUNK + CHUNK
    n_full = jnp.minimum(nk, first_chunk_end // tk)
    m_ref[...] = jnp.full(m_ref.shape, -jnp.inf, F32)
    acc_ref[...] = jnp.zeros(acc_ref.shape, F32)
    qs = [q_ref[:, h * D_QK_PAD:(h + 1) * D_QK_PAD] for h in range(hps)]

    def block(start, width, masked):
        ss = [lax.dot_general(qs[h], k_ref[pl.ds(start, width), h * D_QK_PAD:(h + 1) * D_QK_PAD],
                              _dims(False, True), preferred_element_type=F32) for h in range(hps)]
        if masked:
            q_chunk = jnp.right_shift(q_first + _iota2((tq, width), 0), CHUNK_SHIFT)
            k_chunk = jnp.right_shift(start + _iota2((tq, width), 1), CHUNK_SHIFT)
            ss = [jnp.where(k_chunk <= q_chunk, s, -jnp.inf) for s in ss]
        m_old = [m_ref[h] for h in range(hps)]
        m_new = [jnp.maximum(m_old[h], jnp.max(ss[h], axis=-1, keepdims=True)) for h in range(hps)]
        ps = [jnp.exp(ss[h] - m_new[h]).astype(BF16) for h in range(hps)]
        for h in range(hps):
            acc_ref[h] = jnp.exp(m_old[h] - m_new[h]) * acc_ref[h] + jnp.dot(
                ps[h], v_ref[pl.ds(start, width), h * D_V_PAD:(h + 1) * D_V_PAD], preferred_element_type=F32)
            m_ref[h] = m_new[h]

    def full_body(j, carry):
        block(pl.multiple_of(j * tk, tk), tk, False)
        return carry

    def edge_body(j, carry):
        block(pl.multiple_of(j * tk, tk), tk, True)
        return carry

    edge_len = jnp.minimum(nk * tk, last_chunk_end) - n_full * tk
    n_edge = jnp.maximum(edge_len - 1, 0) // tk
    lax.fori_loop(0, n_full, full_body, 0)
    lax.fori_loop(n_full, n_full + n_edge, edge_body, 0)
    tail = edge_len - n_edge * tk
    tail_start = pl.multiple_of((n_full + n_edge) * tk, tk)
    ew = tk // ATTN_EDGE_SPLIT
    for i in range(1, ATTN_EDGE_SPLIT + 1):
        @pl.when((tail > (i - 1) * ew) & (tail <= i * ew))
        def _(i=i):
            block(tail_start, i * ew, True)

    for h in range(hps):
        acc = acc_ref[h]
        o_ref[:, h * D_V:(h + 1) * D_V] = acc[:, 0:D_V] / acc[:, D_V:D_V + 1]


def _attention(q, k, v, tq, tk, q_off):
    b, l, _ = q.shape
    lk = k.shape[1]
    hps = ATTN_HEADS
    return pl.pallas_call(
        functools.partial(_attn_kernel, tq=tq, tk=tk, nk=lk // tk, q_off=q_off),
        grid=(b, D_HEADS // hps, l // tq),
        in_specs=[pl.BlockSpec((None, tq, hps * D_QK_PAD), lambda bi, h, i: (bi, i, h)),
                  pl.BlockSpec((None, lk, hps * D_QK_PAD), lambda bi, h, i: (bi, 0, h)),
                  pl.BlockSpec((None, lk, hps * D_V_PAD), lambda bi, h, i: (bi, 0, h))],
        out_specs=pl.BlockSpec((None, tq, hps * D_V), lambda bi, h, i: (bi, i, h)),
        out_shape=jax.ShapeDtypeStruct((b, l, D_HEADS * D_V), F32),
        scratch_shapes=[pltpu.VMEM((hps, tq, 1), F32), pltpu.VMEM((hps, tq, D_V_PAD), F32)],
        compiler_params=_params(("parallel", "parallel", "arbitrary")),
        name="mla_attention",
    )(q, k, v)


def _pad_rows_front(buf):
    return jnp.pad(buf, ((0, 0), (ROW_PAD - buf.shape[1], 0), (0, 0)))


def _block_diag(w):
    nb, bi, bj = w.shape
    eye = jnp.eye(nb, dtype=w.dtype)
    return (eye[:, None, :, None] * w[:, :, None, :]).reshape(nb * bi, nb * bj)


def _state_to_block_diag(s, per):
    b, h, n, m = s.shape
    s5 = s.reshape(b, h // per, per, n, m)
    rows = [jnp.pad(s5[:, :, p], ((0, 0), (0, 0), (0, 0), (p * m, (per - 1 - p) * m))) for p in range(per)]
    return jnp.concatenate(rows, axis=2)


def _state_from_block_diag(sbd, per):
    b, g, pn, pm = sbd.shape
    n, m = pn // per, pm // per
    blocks = [sbd[:, :, p * n:(p + 1) * n, p * m:(p + 1) * m] for p in range(per)]
    return jnp.stack(blocks, axis=2).reshape(b, g * per, n, m)


def _rope_swap(w):
    half = w.shape[-1] // 2
    return jnp.concatenate([-w[..., half:], w[..., :half]], axis=-1)


def _prepare(w_in0, a_w_r, a_w_i, a_b_r, a_b_i, w_in1, c_w2, c_a2, d_w_qb, d_w_kvb):
    p = {}
    o = 0
    p["w_a"] = w_in0[:, o:o + 2 * A_WIDTH].astype(BF16); o += 2 * A_WIDTH
    p["w_qkv"] = w_in0[:, o:o + 3 * B_WIDTH].astype(BF16); o += 3 * B_WIDTH
    p["w_z"] = w_in0[:, o:o + B_WIDTH].astype(BF16); o += B_WIDTH
    p["w_gs"] = jnp.pad(w_in0[:, o:o + 2 * B_HEADS], ((0, 0), (0, 128 - 2 * B_HEADS))).astype(BF16)
    p["a_wg"] = jnp.concatenate([_block_diag(a_w_r), _block_diag(a_w_i)], axis=1).astype(BF16)
    p["a_bg"] = jnp.concatenate([a_b_r, a_b_i])[None, :]
    p["w_c"] = w_in1[:, 0:C_PROJ].astype(BF16)
    o = C_PROJ
    w_qa = w_in1[:, o:o + D_Q_LORA]; o += D_Q_LORA
    w_craw = w_in1[:, o:o + D_KV_LORA]; o += D_KV_LORA
    w_kr = w_in1[:, o:o + D_ROPE]
    place = lambda w: jnp.pad(w, ((0, 0), (D_NOPE, D_QK_PAD - D_NOPE - D_ROPE)))
    p["w_d"] = jnp.concatenate([w_qa, w_craw, place(w_kr), place(_rope_swap(w_kr))], axis=1).astype(BF16)
    p["c_w2p"] = jnp.pad(c_w2, ((0, 64), (0, 0)))
    p["c_a2p"] = jnp.pad(c_a2, ((64, 0), (0, 0)))
    wq = d_w_qb.reshape(D_Q_LORA, D_HEADS, D_NOPE + D_ROPE)
    zpad = jnp.zeros((D_Q_LORA, D_HEADS, D_QK_PAD - D_NOPE - D_ROPE), F32)
    p["wq"] = jnp.concatenate([wq, zpad], axis=-1).reshape(D_Q_LORA, D_HEADS * D_QK_PAD).astype(BF16)
    wq_sw = jnp.concatenate([jnp.zeros((D_Q_LORA, D_HEADS, D_NOPE), F32), _rope_swap(wq[..., D_NOPE:]), zpad], axis=-1)
    p["wqs"] = wq_sw.reshape(D_Q_LORA, D_HEADS * D_QK_PAD).astype(BF16)
    wkv = d_w_kvb.reshape(D_KV_LORA, D_HEADS, D_NOPE + D_V)
    wk = jnp.pad(wkv[..., :D_NOPE], ((0, 0), (0, 0), (0, D_QK_PAD - D_NOPE)))
    p["wk"] = wk.reshape(D_KV_LORA, D_HEADS * D_QK_PAD).astype(BF16)
    p["wv"] = wkv[..., D_NOPE:].reshape(D_KV_LORA, D_HEADS * D_V).astype(BF16)
    lane = jnp.arange(MXU_TILE) // C_HEAD_DIM
    p["seg"] = (lane[:, None] == lane[None, :]).astype(BF16)
    return p


def _rope_tables(pos):
    half = D_ROPE // 2
    inv = ROPE_THETA ** (-jnp.arange(half, dtype=F32) / half)
    ang = pos.astype(F32)[:, None] * inv[None, :]
    cos = jnp.cos(ang)
    sin = jnp.sin(ang)
    n = pos.shape[0]
    tail = jnp.zeros((n, D_QK_PAD - D_NOPE - D_ROPE), F32)
    cos2 = jnp.concatenate([cos, cos], axis=1)
    sin2 = jnp.concatenate([sin, sin], axis=1)
    q_cos = jnp.concatenate([jnp.ones((n, D_NOPE), F32), cos2, tail], axis=1)
    q_sin = jnp.concatenate([jnp.zeros((n, D_NOPE), F32), sin2, tail], axis=1)
    k_cos = jnp.concatenate([jnp.zeros((n, D_NOPE), F32), cos2, tail], axis=1)
    return q_cos, q_sin, k_cos


def _trunk(x, pe, pos, a_conv, a_h, b_conv, b_s, c_shift, c_s, d_ckv, d_krope, f_conv, wts, prm):
    (a_conv_w, a_conv_b, a_lambda, b_conv_w, b_a_log, b_dt_bias, b_norm_g, w_out0, c_mu, c_w0,
     c_a0, c_g2, c_k_k, c_k_a, c_r_k, c_lnx_g, c_lnx_b, d_qa_g, d_kva_g, w_out1, ln_mix_g, ln_mix_b,
     ffn_w_up, ffn_conv_w, ffn_conv_b, ffn_w_down, ln_ffn_g, ln_ffn_b, ple_w_gate, ple_b_gate,
     ple_w_proj) = wts
    b, l, d = x.shape
    t = b * l
    tm = min(512, t)
    tt = min(512, l)
    def chunks_per_step(want):
        cps = want
        while l % (cps * CHUNK):
            cps //= 2
        return cps
    row = lambda vec: vec.reshape(1, -1)
    f_new = []
    ffn_nseq = max(1, min(b, FFN_ROWS // l))
    while b % ffn_nseq:
        ffn_nseq -= 1
    ffn_tt = tt if ffn_nseq == 1 else ffn_nseq * l

    def ffn_block(i, mix_a, mix_b, xin, w_out):
        y, nbuf = _ffn(mix_a, mix_b, xin.reshape(b, l, d), pe, i, _pad_rows_front(f_conv[i]),
                       w_out.astype(BF16), row(ln_mix_g[i]), row(ln_mix_b[i]),
                       ffn_w_up[i].astype(BF16), ffn_conv_w[i], row(ffn_conv_b[i]),
                       ffn_w_down[i].astype(BF16), row(ln_ffn_g[i]), row(ln_ffn_b[i]),
                       ple_w_gate[i].astype(BF16), row(ple_b_gate[i]), ple_w_proj[i].astype(BF16),
                       ffn_tt, ffn_nseq)
        f_new.append(nbuf[:, ROW_PAD - 2:, :])
        return y.reshape(t, d)

    x2 = x.reshape(t, d)
    a2, qkv, z, gs = _proj(x2, [prm["w_a"], prm["w_qkv"], prm["w_z"], prm["w_gs"]], tm)
    a_out, a_buf, a_hl = _rglru(a2.reshape(b, l, -1), _pad_rows_front(a_conv), a_h[:, None, :],
                                a_conv_w, row(a_conv_b), prm["a_wg"], prm["a_bg"], row(a_lambda),
                                min(256, l))
    pad8 = lambda vec: jnp.pad(vec, (B_HEADS, 128 - 2 * B_HEADS))[None, :]
    b_out, b_buf, b_sl = _gdn(qkv.reshape(b, l, -1), z.reshape(b, l, -1), gs.reshape(b, l, -1),
                              _pad_rows_front(b_conv), _state_to_block_diag(b_s, B_PAIR), b_conv_w,
                              pad8(b_a_log), pad8(b_dt_bias), row(b_norm_g), chunks_per_step(GDN_CPS))
    x2 = ffn_block(0, a_out, b_out, x2, w_out0)

    q_cos, q_sin, k_cos = _rope_tables(pos)
    c_in, q, c_new, kr_pad, k, v = _layer1_in(x2.reshape(b, l, d), prm["w_c"], prm["w_d"], q_cos, q_sin, k_cos,
                                              row(d_qa_g), prm["wq"], prm["wqs"], row(d_kva_g), prm["wk"],
                                              prm["wv"], tt)
    c_out, c_sh, c_sl = _rwkv(c_in, _pad_rows_front(c_shift),
                              _state_to_block_diag(jnp.swapaxes(c_s, 2, 3), C_GROUP), row(c_mu), row(c_w0),
                              prm["c_w2p"], row(c_a0), prm["c_a2p"], c_g2, row(c_k_k), row(c_k_a),
                              row(c_r_k.reshape(-1)), row(c_lnx_g), row(c_lnx_b), prm["seg"],
                              chunks_per_step(RWKV_CPS))
    past = d_ckv.shape[1]
    if past:
        kr_old = jnp.pad(d_krope, ((0, 0), (0, 0), (D_NOPE, D_QK_PAD - D_NOPE - D_ROPE)))
        k_old, v_old = _kv(d_ckv, kr_old, prm["wk"], prm["wv"], min(512, past))
        k = jnp.concatenate([k_old, k], axis=1)
        v = jnp.concatenate([v_old, v], axis=1)
    lk = past + l
    tq = min(ATTN_TQ, l)
    tk = ATTN_TK if lk % ATTN_TK == 0 else lk
    d_out = _attention(q, k, v, tq, tk, past)
    x2 = ffn_block(1, c_out, d_out, x2, w_out1)

    return (x2.reshape(b, l, d), a_buf[:, ROW_PAD - 3:, :], a_hl[:, 0, :], b_buf[:, ROW_PAD - 3:, :],
            _state_from_block_diag(b_sl, B_PAIR), c_sh[:, ROW_PAD - 1:, :],
            jnp.swapaxes(_state_from_block_diag(c_sl, C_GROUP), 2, 3), c_new,
            kr_pad[:, :, D_NOPE:D_NOPE + D_ROPE],
            jnp.stack(f_new))


def kernel(x_prompt, x_sample, state_a_conv, state_a_h, state_b_conv, state_b_s, state_c_shift, state_c_s, cache_d_ckv, cache_d_krope, state_ffn_conv, p_prompt, p_sample, w_in0, a_conv_w, a_conv_b, a_w_r, a_b_r, a_w_i, a_b_i, a_lambda, b_conv_w, b_a_log, b_dt_bias, b_norm_g, w_out0, w_in1, c_mu, c_w0, c_w2, c_a0, c_a2, c_g2, c_k_k, c_k_a, c_r_k, c_lnx_g, c_lnx_b, d_qa_g, d_w_qb, d_kva_g, d_w_kvb, w_out1, ln_mix_g, ln_mix_b, ffn_w_up, ffn_conv_w, ffn_conv_b, ffn_w_down, ln_ffn_g, ln_ffn_b, ple_w_gate, ple_b_gate, ple_w_proj):
    prm = _prepare(w_in0, a_w_r, a_w_i, a_b_r, a_b_i, w_in1, c_w2, c_a2, d_w_qb, d_w_kvb)
    wts = (a_conv_w, a_conv_b, a_lambda, b_conv_w, b_a_log, b_dt_bias, b_norm_g, w_out0, c_mu, c_w0,
           c_a0, c_g2, c_k_k, c_k_a, c_r_k, c_lnx_g, c_lnx_b, d_qa_g, d_kva_g, w_out1, ln_mix_g, ln_mix_b,
           ffn_w_up, ffn_conv_w, ffn_conv_b, ffn_w_down, ln_ffn_g, ln_ffn_b, ple_w_gate, ple_b_gate,
           ple_w_proj)
    dt = x_prompt.dtype
    bp, lp = x_prompt.shape[0], x_prompt.shape[1]
    zeros = lambda *shape: jnp.zeros(shape, dt)
    prompt = _trunk(x_prompt, p_prompt, jnp.arange(lp),
                    zeros(bp, 3, A_WIDTH), zeros(bp, A_WIDTH), zeros(bp, 3, 3 * B_WIDTH),
                    zeros(bp, B_HEADS, B_HEAD_DIM, B_HEAD_DIM), zeros(bp, 1, C_PROJ),
                    zeros(bp, C_HEADS, C_HEAD_DIM, C_HEAD_DIM), zeros(bp, 0, D_KV_LORA),
                    zeros(bp, 0, D_ROPE), zeros(DEPTH, bp, 2, 2 * D_FF), wts, prm)
    ls = x_sample.shape[1]
    past = cache_d_ckv.shape[1]
    sample = _trunk(x_sample, p_sample, past + jnp.arange(ls), state_a_conv, state_a_h, state_b_conv,
                    state_b_s, state_c_shift, state_c_s, cache_d_ckv, cache_d_krope, state_ffn_conv,
                    wts, prm)
    out = [prompt[0], sample[0]]
    for ps, ss in zip(prompt[1:], sample[1:]):
        out += [ps, ss]
    return tuple(out)
```
